```python
import math, functools
import jax, jax.numpy as jnp
from jax import lax
import numpy as np


D_MODEL = 2048
BATCH = 8
SEQ = 4096
DEPTH = 4

CTX_LEN = 256
GRID_W = 64
ROPE_THETA = 10000.0
EPS = 1e-6
Q_BLOCK = 128

HG_HEADS = 4
HG_DK = 128
HG_DV = 128
GLA_CHUNK = 32
DF_HEADS = 4
DF_DIM = 64
DF_DV = 2 * DF_DIM
FN_GROUPS = 4
FN_DIM = 128
GQ_HEADS = 4
GQ_KV_HEADS = 2
GQ_DIM = 128
N_BRANCH = 4
D_FF = 5632
N_EXPERTS = 8
TOP_K = 2
D_FF_EXPERT = 4096

N_DENSE = (DEPTH + 1) // 2
N_MOE = DEPTH // 2
MOD_WIDTH = 6 * D_MODEL

KV_SIZES = (HG_HEADS * HG_DK, HG_HEADS * HG_DK, HG_HEADS * HG_DV,
            DF_HEADS * 2 * DF_DIM, DF_HEADS * DF_DV,
            GQ_KV_HEADS * GQ_DIM, GQ_KV_HEADS * GQ_DIM)
Q_SIZES = (HG_HEADS * HG_DK, HG_HEADS * HG_DV, DF_HEADS * 2 * DF_DIM,
           GQ_HEADS * GQ_DIM, FN_GROUPS * FN_DIM, N_BRANCH * D_MODEL)
KV_COLS = sum(KV_SIZES)
IN_COLS = KV_COLS + sum(Q_SIZES)
BRANCH_W = HG_HEADS * HG_DV

kernel_name = 'hybrid_flow_backbone'


def rmsnorm(x, g):
    xf = x.astype(jnp.float32)
    y = xf * lax.rsqrt(jnp.mean(xf * xf, axis=-1, keepdims=True) + EPS)
    return (y * g.astype(jnp.float32)).astype(x.dtype)


def split_cols(t, sizes):
    idx = [int(i) for i in np.cumsum(sizes)[:-1]]
    return jnp.split(t, idx, axis=-1)


def to_heads(t, n):
    b, l, _ = t.shape
    return t.reshape(b, l, n, -1).transpose(0, 2, 1, 3)


def from_heads(t):
    b, n, l, d = t.shape
    return t.transpose(0, 2, 1, 3).reshape(b, l, n * d)


def flip(t):
    return jnp.flip(t, axis=2)


def diff_split(t):
    b, l, _ = t.shape
    t = t.reshape(b, l, DF_HEADS, 2, DF_DIM)
    return t[:, :, :, 0].transpose(0, 2, 1, 3), t[:, :, :, 1].transpose(0, 2, 1, 3)


def axial_rope_tables(rows, dim):
    quarter = dim // 4
    inv = ROPE_THETA ** (-jnp.arange(quarter, dtype=jnp.float32) / quarter)
    r = jnp.repeat(jnp.arange(rows, dtype=jnp.float32), GRID_W)
    col = jnp.tile(jnp.arange(GRID_W, dtype=jnp.float32), rows)
    ar = r[:, None] * inv
    ac = col[:, None] * inv
    ang = jnp.concatenate([ar, ar, ac, ac], axis=-1)
    return jnp.cos(ang), jnp.sin(ang)


def apply_rope(x, cos, sin):
    x1, x2, x3, x4 = jnp.split(x, 4, axis=-1)
    rot = jnp.concatenate([-x2, x1, -x4, x3], axis=-1)
    return (x * cos + rot * sin).astype(x.dtype)


def block_softmax_attention(q, k, v, scale):
    b, hk, g, lq, d = q.shape
    nb = lq // Q_BLOCK
    qb = jnp.moveaxis(q.reshape(b, hk, g, nb, Q_BLOCK, d), 3, 0)

    def one(qi):
        s = jnp.einsum('bhgqd,bhkd->bhgqk', qi, k).astype(jnp.float32) * scale
        p = jax.nn.softmax(s, axis=-1).astype(v.dtype)
        return jnp.einsum('bhgqk,bhkd->bhgqd', p, v)

    o = lax.map(one, qb)
    return jnp.moveaxis(o, 0, 3).reshape(b, hk, g, lq, -1)


def block_diff_attention(q1, q2, k1, k2, v, lam, scale):
    b, h, lq, d = q1.shape
    nb = lq // Q_BLOCK

    def blocks(t):
        return jnp.moveaxis(t.reshape(b, h, nb, Q_BLOCK, d), 2, 0)

    def one(qs):
        qa, qb_ = qs
        s1 = jnp.einsum('bhqd,bhkd->bhqk', qa, k1).astype(jnp.float32) * scale
        s2 = jnp.einsum('bhqd,bhkd->bhqk', qb_, k2).astype(jnp.float32) * scale
        p = jax.nn.softmax(s1, axis=-1) - lam * jax.nn.softmax(s2, axis=-1)
        return jnp.einsum('bhqk,bhkd->bhqd', p.astype(v.dtype), v)

    o = lax.map(one, (blocks(q1), blocks(q2)))
    return jnp.moveaxis(o, 0, 2).reshape(b, h, lq, -1)


def hgrn_forget(z, lb):
    lb = lb.reshape(HG_HEADS, 1, HG_DK)
    zf = z.astype(jnp.float32)
    log_f = jnp.log(lb + (1.0 - lb) * jax.nn.sigmoid(zf))
    k = (1.0 - lb) * jax.nn.sigmoid(-zf)
    return log_f, k


def gla_chunk_scan(q, k, v, log_f, s0):
    b, h, l, dk = q.shape
    dv = v.shape[-1]
    c = GLA_CHUNK
    n = l // c
    f32 = jnp.float32
    qc = q.astype(f32).reshape(b, h, n, c, dk)
    kc = k.astype(f32).reshape(b, h, n, c, dk)
    vc = v.astype(f32).reshape(b, h, n, c, dv)
    cum = jnp.cumsum(log_f.astype(f32).reshape(b, h, n, c, dk), axis=3)
    mid = cum[:, :, :, c // 2 - 1:c // 2]
    end = cum[:, :, :, c - 1:]
    att = jnp.einsum('bhncd,bhnsd->bhncs', qc * jnp.exp(cum - mid), kc * jnp.exp(mid - cum))
    att = jnp.where(jnp.tril(jnp.ones((c, c), dtype=bool)), att, 0.0)
    o_intra = jnp.einsum('bhncs,bhnse->bhnce', att, vc)
    q_dec = jnp.moveaxis(qc * jnp.exp(cum), 2, 0)
    k_dec = jnp.moveaxis(kc * jnp.exp(end - cum), 2, 0)
    v_n = jnp.moveaxis(vc, 2, 0)
    decay = jnp.moveaxis(jnp.exp(end[:, :, :, 0]), 2, 0)

    def step(s, xs):
        qd, kd, vv, dec = xs
        o = jnp.einsum('bhcd,bhde->bhce', qd, s)
        s = dec[..., None] * s + jnp.einsum('bhcd,bhce->bhde', kd, vv)
        return s, o

    _, o_inter = lax.scan(step, s0.astype(f32), (q_dec, k_dec, v_n, decay))
    o = o_intra + jnp.moveaxis(o_inter, 0, 2)
    return o.reshape(b, h, l, dv).astype(v.dtype)


def gla_final_state(k, v, log_f):
    cum = jnp.cumsum(log_f.astype(jnp.float32), axis=2)
    w = jnp.exp(cum[:, :, -1:] - cum)
    return jnp.einsum('bhld,bhle->bhde', k.astype(jnp.float32) * w, v.astype(jnp.float32))


def fourier_mix(u):
    b, l, _ = u.shape
    uf = u.astype(jnp.float32).reshape(b, l, FN_GROUPS, FN_DIM)
    y = jnp.fft.fftn(uf, axes=(1, 3), norm='ortho').real
    return y.reshape(b, l, FN_GROUPS * FN_DIM).astype(u.dtype)


def parse_keys(kv, lb, k_norm_g):
    f_fwd, f_bwd, hv, dfk, dfv, gqk, gqv = split_cols(kv, KV_SIZES)
    lf_f, k_f = hgrn_forget(to_heads(f_fwd, HG_HEADS), lb[0])
    lf_b, k_b = hgrn_forget(to_heads(f_bwd, HG_HEADS), lb[1])
    k1, k2 = diff_split(dfk)
    rec = (lf_f, k_f, lf_b, k_b, to_heads(hv, HG_HEADS))
    att = (k1, k2, to_heads(dfv, DF_HEADS),
           rmsnorm(to_heads(gqk, GQ_KV_HEADS), k_norm_g), to_heads(gqv, GQ_KV_HEADS))
    return rec, att


def parse_queries(qs, q_norm_g):
    hq, hg, dfq, gqq, fu, gl = split_cols(qs, Q_SIZES)
    q1, q2 = diff_split(dfq)
    return (to_heads(hq, HG_HEADS), hg, q1, q2,
            rmsnorm(to_heads(gqq, GQ_HEADS), q_norm_g), fu, gl)


def stream_output(queries, rec, s_f, s_b, att_keys, lam, lam_init, hg_norm_g, df_norm_g,
                  w_branch, w_out):
    hq, hg, q1, q2, gq, fu, gl = queries
    lf_f, k_f, lf_b, k_b, hv = rec
    k1, k2, dv, gk, gv = att_keys
    o_hg = (gla_chunk_scan(hq, k_f, hv, lf_f, s_f)
            + flip(gla_chunk_scan(flip(hq), flip(k_b), flip(hv), flip(lf_b), s_b)))
    y_hg = from_heads(rmsnorm(o_hg, hg_norm_g)) * jax.nn.silu(hg)
    o_df = block_diff_attention(q1, q2, k1, k2, dv, lam, DF_DIM ** -0.5)
    y_df = from_heads(rmsnorm(o_df, df_norm_g) * (1.0 - lam_init))
    y_fn = fourier_mix(fu)
    b, h, l, d = gq.shape
    o_gq = block_softmax_attention(gq.reshape(b, GQ_KV_HEADS, h // GQ_KV_HEADS, l, d), gk, gv,
                                   GQ_DIM ** -0.5)
    y_gq = from_heads(o_gq.reshape(b, h, l, d))
    gates = jax.nn.sigmoid(gl.astype(jnp.float32)).astype(gl.dtype)
    gates = gates.reshape(gl.shape[0], gl.shape[1], N_BRANCH, D_MODEL)
    merged = None
    for j, y in enumerate((y_hg, y_df, y_fn, y_gq)):
        term = gates[:, :, j, :] * (y @ w_branch[j])
        merged = term if merged is None else merged + term
    return merged @ w_out


def token_mixing(a_lat, a_ctx, w_in, lb, hg_norm_g, lam, lam_init, df_norm_g, q_norm_g, k_norm_g,
                 w_branch, w_out, rope_df, rope_gq, ctx_out):
    cd, sd = rope_df
    cg, sg = rope_gq
    p_lat = a_lat @ w_in
    rec_lat, att_lat = parse_keys(p_lat[..., :KV_COLS], lb, k_norm_g)
    hq, hg, q1, q2, gq, fu, gl = parse_queries(p_lat[..., KV_COLS:], q_norm_g)
    q_lat = (hq, hg, apply_rope(q1, cd, sd), apply_rope(q2, cd, sd), apply_rope(gq, cg, sg), fu, gl)
    p_ctx = a_ctx @ (w_in if ctx_out else w_in[:, :KV_COLS])
    rec_ctx, att_ctx = parse_keys(p_ctx[..., :KV_COLS], lb, k_norm_g)
    lf_f, k_f, lf_b, k_b, hv = rec_ctx
    s_f = gla_final_state(k_f, hv, lf_f)
    s_b = gla_final_state(flip(k_b), flip(hv), flip(lf_b))
    k1, k2, dv, gk, gv = att_lat
    c1, c2, cdv, cgk, cgv = att_ctx
    keys_lat = (jnp.concatenate([c1, apply_rope(k1, cd, sd)], axis=2),
                jnp.concatenate([c2, apply_rope(k2, cd, sd)], axis=2),
                jnp.concatenate([cdv, dv], axis=2),
                jnp.concatenate([cgk, apply_rope(gk, cg, sg)], axis=2),
                jnp.concatenate([cgv, gv], axis=2))
    y_lat = stream_output(q_lat, rec_lat, s_f, s_b, keys_lat, lam, lam_init, hg_norm_g, df_norm_g,
                          w_branch, w_out)
    if not ctx_out:
        return y_lat, None
    zero = jnp.zeros(s_f.shape, jnp.float32)
    y_ctx = stream_output(parse_queries(p_ctx[..., KV_COLS:], q_norm_g), rec_ctx, zero, zero, att_ctx,
                          lam, lam_init, hg_norm_g, df_norm_g, w_branch, w_out)
    return y_lat, y_ctx


def swiglu(t, w_gate, w_up, w_down):
    return (jax.nn.silu(t @ w_gate) * (t @ w_up)) @ w_down


def moe_swiglu(t, w_router, b_router, w_gate, w_up, w_down):
    b, n, d = t.shape
    tok = t.reshape(b * n, d)
    logits = (tok @ w_router + b_router).astype(jnp.float32)
    top_val, top_idx = lax.top_k(logits, TOP_K)
    top_w = jax.nn.softmax(top_val, axis=-1)
    out = jnp.zeros_like(tok)
    for e in range(N_EXPERTS):
        w_e = jnp.sum(jnp.where(top_idx == e, top_w, 0.0), axis=-1, keepdims=True).astype(tok.dtype)
        out = out + w_e * swiglu(tok, w_gate[e], w_up[e], w_down[e])
    return out.reshape(b, n, d)


def setup_inputs(seed: int = 0) -> dict:
    key = jax.random.key(seed)
    ks = jax.random.split(key, 27)
    D = D_MODEL

    def nrm(i, shape, s):
        return jax.random.normal(ks[i], shape, jnp.float32) * s

    return {
        'x': nrm(0, (BATCH, SEQ, D), 1.0),
        'c': nrm(1, (BATCH, D), 1.0),
        'ctx': nrm(2, (BATCH, CTX_LEN, D), 1.0),
        'c_ctx': nrm(3, (D,), 1.0),
        'w_mod': nrm(4, (DEPTH, D, MOD_WIDTH), 0.5 * D ** -0.5),
        'b_mod': nrm(5, (DEPTH, MOD_WIDTH), 0.02),
        'g_pre_mix': 1.0 + nrm(6, (DEPTH, D), 0.05),
        'g_post_mix': 1.0 + nrm(7, (DEPTH, D), 0.05),
        'g_pre_ffn': 1.0 + nrm(8, (DEPTH, D), 0.05),
        'g_post_ffn': 1.0 + nrm(9, (DEPTH, D), 0.05),
        'w_in': nrm(10, (DEPTH, D, IN_COLS), D ** -0.5),
        'hgrn_lb': nrm(11, (DEPTH, 2, HG_HEADS * HG_DK), 0.5),
        'hgrn_norm_g': 1.0 + nrm(12, (DEPTH, HG_DV), 0.05),
        'diff_lambda': nrm(13, (DEPTH, 4, DF_DIM), 0.1),
        'diff_norm_g': 1.0 + nrm(14, (DEPTH, DF_DV), 0.05),
        'qk_norm_q': 1.0 + nrm(15, (DEPTH, GQ_DIM), 0.05),
        'qk_norm_k': 1.0 + nrm(16, (DEPTH, GQ_DIM), 0.05),
        'w_branch': nrm(17, (DEPTH, N_BRANCH, BRANCH_W, D), BRANCH_W ** -0.5),
        'w_out': nrm(18, (DEPTH, D, D), D ** -0.5),
        'w_ff_gate': nrm(19, (N_DENSE, D, D_FF), D ** -0.5),
        'w_ff_up': nrm(20, (N_DENSE, D, D_FF), D ** -0.5),
        'w_ff_down': nrm(21, (N_DENSE, D_FF, D), D_FF ** -0.5),
        'w_router': nrm(22, (N_MOE, D, N_EXPERTS), D ** -0.5),
        'b_router': nrm(23, (N_MOE, N_EXPERTS), 0.01),
        'w_moe_gate': nrm(24, (N_MOE, N_EXPERTS, D, D_FF_EXPERT), D ** -0.5),
        'w_moe_up': nrm(25, (N_MOE, N_EXPERTS, D, D_FF_EXPERT), D ** -0.5),
        'w_moe_down': nrm(26, (N_MOE, N_EXPERTS, D_FF_EXPERT, D), D_FF_EXPERT ** -0.5),
    }


def reference(x, c, ctx, c_ctx, w_mod, b_mod, g_pre_mix, g_post_mix, g_pre_ffn, g_post_ffn, w_in,
              hgrn_lb, hgrn_norm_g, diff_lambda, diff_norm_g, qk_norm_q, qk_norm_k, w_branch, w_out,
              w_ff_gate, w_ff_up, w_ff_down, w_router, b_router, w_moe_gate, w_moe_up, w_moe_down):
    rows = x.shape[1] // GRID_W
    rope_df = axial_rope_tables(rows, DF_DIM)
    rope_gq = axial_rope_tables(rows, GQ_DIM)
    lb_all = jnp.cumsum(jax.nn.softmax(hgrn_lb.astype(jnp.float32), axis=0), axis=0)
    lb_all = lb_all - lb_all[:1]
    s_lat = jax.nn.silu(c)[:, None, :]
    s_ctx = jax.nn.silu(c_ctx)
    h, hc = x, ctx
    for l in range(DEPTH):
        last = l == DEPTH - 1
        m = jnp.split(s_lat @ w_mod[l] + b_mod[l], 6, axis=-1)
        n_mc = 2 if last else 6
        mc = jnp.split(s_ctx @ w_mod[l][:, :n_mc * D_MODEL] + b_mod[l][:n_mc * D_MODEL], n_mc, axis=-1)
        a_lat = rmsnorm(h, g_pre_mix[l]) * (1.0 + m[1]) + m[0]
        a_ctx = rmsnorm(hc, g_pre_mix[l]) * (1.0 + mc[1]) + mc[0]
        lam_init = 0.8 - 0.6 * math.exp(-0.3 * l)
        lv = diff_lambda[l].astype(jnp.float32)
        lam = jnp.exp(jnp.sum(lv[0] * lv[1])) - jnp.exp(jnp.sum(lv[2] * lv[3])) + lam_init
        y_lat, y_ctx = token_mixing(a_lat, a_ctx, w_in[l], lb_all[l], hgrn_norm_g[l], lam, lam_init,
                                    diff_norm_g[l], qk_norm_q[l], qk_norm_k[l], w_branch[l], w_out[l],
                                    rope_df, rope_gq, not last)
        h = h + m[2] * rmsnorm(y_lat, g_post_mix[l])
        j = l // 2
        if l % 2 == 0:
            ffn = functools.partial(swiglu, w_gate=w_ff_gate[j], w_up=w_ff_up[j], w_down=w_ff_down[j])
        else:
            ffn = functools.partial(moe_swiglu, w_router=w_router[j], b_router=b_router[j],
                                    w_gate=w_moe_gate[j], w_up=w_moe_up[j], w_down=w_moe_down[j])
        b_lat = rmsnorm(h, g_pre_ffn[l]) * (1.0 + m[4]) + m[3]
        h = h + m[5] * rmsnorm(ffn(b_lat), g_post_ffn[l])
        if not last:
            hc = hc + mc[2] * rmsnorm(y_ctx, g_post_mix[l])
            b_ctx = rmsnorm(hc, g_pre_ffn[l]) * (1.0 + mc[4]) + mc[3]
            hc = hc + mc[5] * rmsnorm(ffn(b_ctx), g_post_ffn[l])
    return h
```

```python
import functools
import math

import jax
import jax.numpy as jnp
import numpy as np
from jax import lax
from jax.experimental import pallas as pl
from jax.experimental.pallas import tpu as pltpu

F32 = jnp.float32
BF16 = jnp.bfloat16

EPS = 1e-6
GRID_W = 64
ROPE_THETA = 10000.0
HEAD = 128
HG_HEADS = 4
GLA_CHUNK = 32
DF_HEADS = 4
DF_DIM = 64
FN_GROUPS = 4
GQ_HEADS = 4
GQ_KV_HEADS = 2
N_BRANCH = 4
BRANCH_W = 512
TOP_K = 2

V7X_VMEM_LIMIT_BYTES = 56 * 1024 * 1024
SCAN_BLOCK = 256


def _params(*sem):
    return pltpu.CompilerParams(dimension_semantics=sem, vmem_limit_bytes=V7X_VMEM_LIMIT_BYTES)


def _tile(n, pref, align=128):
    if n <= pref:
        return n
    t = pref - pref % align
    while n % t:
        t -= align
    return t


def _mm_kernel(a_ref, w_ref, o_ref):
    o_ref[...] = jnp.dot(a_ref[...], w_ref[...], preferred_element_type=F32).astype(o_ref.dtype)


def _mm_acc_kernel(a_ref, w_ref, o_ref, acc_ref):
    k = pl.program_id(2)

    @pl.when(k == 0)
    def _():
        acc_ref[...] = jnp.zeros_like(acc_ref)

    acc_ref[...] += jnp.dot(a_ref[...], w_ref[...], preferred_element_type=F32)

    @pl.when(k == pl.num_programs(2) - 1)
    def _():
        o_ref[...] = acc_ref[...].astype(o_ref.dtype)


def matmul(a, w, out_dtype, tm=1024, tn=1024, tk=None):
    m, k = a.shape
    n = w.shape[1]
    tm, tn = _tile(m, tm), _tile(n, tn)
    if tk is None or tk >= k:
        return pl.pallas_call(
            _mm_kernel,
            grid=(m // tm, n // tn),
            in_specs=[pl.BlockSpec((tm, k), lambda i, j: (i, 0)),
                      pl.BlockSpec((k, tn), lambda i, j: (0, j))],
            out_specs=pl.BlockSpec((tm, tn), lambda i, j: (i, j)),
            out_shape=jax.ShapeDtypeStruct((m, n), out_dtype),
            compiler_params=_params("parallel", "parallel"),
            name="matmul",
        )(a, w)
    tk = _tile(k, tk)
    return pl.pallas_call(
        _mm_acc_kernel,
        grid=(m // tm, n // tn, k // tk),
        in_specs=[pl.BlockSpec((tm, tk), lambda i, j, q: (i, q)),
                  pl.BlockSpec((tk, tn), lambda i, j, q: (q, j))],
        out_specs=pl.BlockSpec((tm, tn), lambda i, j, q: (i, j)),
        out_shape=jax.ShapeDtypeStruct((m, n), out_dtype),
        scratch_shapes=[pltpu.VMEM((tm, tn), F32)],
        compiler_params=_params("parallel", "parallel", "arbitrary"),
        name="matmul_acc",
    )(a, w)


def _mod_kernel(s_ref, w_ref, b_ref, o_ref):
    s = s_ref[...]
    s = (s * jax.nn.sigmoid(s)).astype(BF16)
    o_ref[0] = jnp.dot(s, w_ref[0].astype(BF16), preferred_element_type=F32) + b_ref[0]


def modulation(cond, w_mod, b_mod):
    depth, d, width = w_mod.shape
    r = cond.shape[0]
    tn = _tile(width, 1024)
    return pl.pallas_call(
        _mod_kernel,
        grid=(depth, width // tn),
        in_specs=[pl.BlockSpec((r, d), lambda l, j: (0, 0)),
                  pl.BlockSpec((1, d, tn), lambda l, j: (l, 0, j)),
                  pl.BlockSpec((1, 1, tn), lambda l, j: (l, 0, j))],
        out_specs=pl.BlockSpec((1, r, tn), lambda l, j: (l, 0, j)),
        out_shape=jax.ShapeDtypeStruct((depth, r, width), F32),
        compiler_params=_params("parallel", "parallel"),
        name="modulation",
    )(cond, w_mod, b_mod.reshape(depth, 1, width))


def _rms(x):
    return x * lax.rsqrt(jnp.mean(x * x, axis=-1, keepdims=True) + EPS)


def _norm_mod_kernel(h_ref, g_ref, mod_ref, o_ref, *, shift_idx, scale_idx):
    y = _rms(h_ref[0]) * g_ref[...]
    shift = mod_ref[0, shift_idx:shift_idx + 1, :]
    scale = mod_ref[0, scale_idx:scale_idx + 1, :]
    o_ref[0] = (y * (1.0 + scale) + shift).astype(o_ref.dtype)


def _mod_index(n_ctx_tiles, n_batch):
    return lambda b, t: (jnp.where(t < n_ctx_tiles, n_batch, b), 0, 0)


def norm_mod(h, g, mods, shift_idx, scale_idx, n_ctx):
    b, t, d = h.shape
    tt = _tile(n_ctx, 256)
    return pl.pallas_call(
        functools.partial(_norm_mod_kernel, shift_idx=shift_idx, scale_idx=scale_idx),
        grid=(b, t // tt),
        in_specs=[pl.BlockSpec((1, tt, d), lambda i, j: (i, j, 0)),
                  pl.BlockSpec((1, d), lambda i, j: (0, 0)),
                  pl.BlockSpec((1, 6, d), _mod_index(n_ctx // tt, b))],
        out_specs=pl.BlockSpec((1, tt, d), lambda i, j: (i, j, 0)),
        out_shape=jax.ShapeDtypeStruct((b, t, d), BF16),
        compiler_params=_params("parallel", "parallel"),
        name="norm_mod",
    )(h, g.reshape(1, d), mods)


def _resid_kernel(h_ref, y_ref, g_ref, mod_ref, o_ref, *, gate_idx):
    gate = mod_ref[0, gate_idx:gate_idx + 1, :]
    o_ref[0] = h_ref[0] + gate * (_rms(y_ref[0]) * g_ref[...])


def residual(h, y, g, mods, gate_idx, n_ctx):
    b, t, d = h.shape
    tt = _tile(n_ctx, 256)
    return pl.pallas_call(
        functools.partial(_resid_kernel, gate_idx=gate_idx),
        grid=(b, t // tt),
        in_specs=[pl.BlockSpec((1, tt, d), lambda i, j: (i, j, 0)),
                  pl.BlockSpec((1, tt, d), lambda i, j: (i, j, 0)),
                  pl.BlockSpec((1, d), lambda i, j: (0, 0)),
                  pl.BlockSpec((1, 6, d), _mod_index(n_ctx // tt, b))],
        out_specs=pl.BlockSpec((1, tt, d), lambda i, j: (i, j, 0)),
        out_shape=jax.ShapeDtypeStruct((b, t, d), F32),
        input_output_aliases={0: 0},
        compiler_params=_params("parallel", "parallel"),
        name="residual",
    )(h, y.reshape(b, t, d), g.reshape(1, d), mods)


def rope_tables(n_ctx, seq, dim):
    quarter = dim // 4
    inv = ROPE_THETA ** (-jnp.arange(quarter, dtype=F32) / quarter)
    pos = jnp.arange(seq)
    ar = (pos // GRID_W).astype(F32)[:, None] * inv
    ac = (pos % GRID_W).astype(F32)[:, None] * inv
    ang = jnp.concatenate([ar, ar, ac, ac], axis=-1)
    ang = jnp.tile(ang, (1, HEAD // dim))
    ang = jnp.concatenate([jnp.zeros((n_ctx, HEAD), F32), ang], axis=0)
    first = ((jnp.arange(HEAD) % dim) % (dim // 2)) < quarter
    cos, sin = jnp.cos(ang), jnp.sin(ang)
    return cos, jnp.where(first, -sin, 0.0), jnp.where(first, 0.0, sin)


def _prep_kernel(x_ref, cos_ref, sa_ref, sb_ref, g_ref, *o_refs, quarter, norm, scale, split):
    x = x_ref[0].astype(F32)
    if norm:
        x = _rms(x) * g_ref[...]
    y = (x * cos_ref[...] + pltpu.roll(x, HEAD - quarter, 1) * sa_ref[...]
         + pltpu.roll(x, quarter, 1) * sb_ref[...])
    if scale != 1.0:
        y = y * scale
    if split:
        lane = lax.broadcasted_iota(jnp.int32, y.shape, 1)
        o_refs[0][0] = jnp.where(lane < DF_DIM, y, 0.0).astype(BF16)
        o_refs[1][0] = jnp.where(lane >= DF_DIM, y, 0.0).astype(BF16)
    else:
        o_refs[0][0] = y.astype(BF16)


def rope_prep(p, col0, ncols, tables, g, *, dim, norm=False, scale=1.0, split=False):
    b, t, _ = p.shape
    tt = _tile(t, 256)
    c0 = col0 // HEAD
    cos, sa, sb = tables
    tab = pl.BlockSpec((tt, HEAD), lambda i, j, c: (j, 0))
    n_out = 2 if split else 1
    out = pl.pallas_call(
        functools.partial(_prep_kernel, quarter=dim // 4, norm=norm, scale=scale, split=split),
        grid=(b, t // tt, ncols // HEAD),
        in_specs=[pl.BlockSpec((1, tt, HEAD), lambda i, j, c: (i, j, c0 + c)),
                  tab, tab, tab,
                  pl.BlockSpec((1, HEAD), lambda i, j, c: (0, 0))],
        out_specs=[pl.BlockSpec((1, tt, HEAD), lambda i, j, c: (i, j, c))] * n_out,
        out_shape=[jax.ShapeDtypeStruct((b, t, ncols), BF16)] * n_out,
        compiler_params=_params("parallel", "parallel", "parallel"),
        name="rope_prep",
    )(p, cos, sa, sb, g.reshape(1, HEAD))
    return out if split else out[0]


def _softmax_pv(q, k, v):
    s = lax.dot_general(q, k, (((1,), (1,)), ((), ())), preferred_element_type=F32)
    e = jnp.exp(s - jnp.max(s, axis=-1, keepdims=True))
    o = jnp.dot(e.astype(BF16), v, preferred_element_type=F32)
    return o / jnp.sum(e, axis=-1, keepdims=True)


def _attn_kernel(*refs, diff, n_ctx, n_ctx_tiles, post_scale):
    if diff:
        lam_ref, q1_ref, q2_ref, k_ref, v_ref, g_ref, o_ref = refs
    else:
        q1_ref, k_ref, v_ref, o_ref = refs

    def run(nk):
        k = k_ref[0, :nk, :]
        v = v_ref[0, :nk, :]
        o = _softmax_pv(q1_ref[0], k, v)
        if diff:
            o = o - lam_ref[...] * _softmax_pv(q2_ref[0], k, v)
            o = _rms(o) * g_ref[...] * post_scale
        o_ref[0] = o.astype(o_ref.dtype)

    t = pl.program_id(2)

    @pl.when(t < n_ctx_tiles)
    def _():
        run(n_ctx)

    @pl.when(t >= n_ctx_tiles)
    def _():
        run(k_ref.shape[1])


def attention(q, k, v, v_col0, n_heads, n_kv_heads, n_ctx, q2=None, lam=None, g=None,
              post_scale=1.0):
    b, t, _ = q.shape
    tq = _tile(n_ctx, 256)
    rep = n_heads // n_kv_heads
    vc = v_col0 // HEAD
    diff = q2 is not None
    qspec = pl.BlockSpec((1, tq, HEAD), lambda i, h, j: (i, j, h))
    kspec = pl.BlockSpec((1, t, HEAD), lambda i, h, j: (i, 0, h // rep))
    vspec = pl.BlockSpec((1, t, HEAD), lambda i, h, j: (i, 0, vc + h // rep))
    if diff:
        one = pl.BlockSpec((1, 1), lambda i, h, j: (0, 0))
        in_specs = [one, qspec, qspec, kspec, vspec, pl.BlockSpec((1, HEAD), lambda i, h, j: (0, 0))]
        args = (lam.reshape(1, 1).astype(F32), q, q2, k, v, g.reshape(1, HEAD))
    else:
        in_specs = [qspec, kspec, vspec]
        args = (q, k, v)
    return pl.pallas_call(
        functools.partial(_attn_kernel, diff=diff, n_ctx=n_ctx, n_ctx_tiles=n_ctx // tq,
                          post_scale=post_scale),
        grid=(b, n_heads, t // tq),
        in_specs=in_specs,
        out_specs=qspec,
        out_shape=jax.ShapeDtypeStruct((b, t, n_heads * HEAD), BF16),
        compiler_params=_params("parallel", "parallel", "parallel"),
        name="diff_attention" if diff else "gq_attention",
    )(*args)


def _split3(x):
    a = x.astype(BF16)
    r = x - a.astype(F32)
    b = r.astype(BF16)
    c = (r - b.astype(F32)).astype(BF16)
    return a, b, c


def _scan_kernel(q_ref, zf_ref, zb_ref, v_ref, hg_ref, lb_ref, g_ref, o_ref,
                 of_ref, ob_ref, sf_ref, sb_ref, *, n_ctx_blocks):
    rows = SCAN_BLOCK
    c = GLA_CHUNK
    n_chunks = rows // c
    n_blocks = q_ref.shape[1] // rows

    r_i = lax.broadcasted_iota(jnp.int32, (rows, rows), 0)
    s_i = lax.broadcasted_iota(jnp.int32, (rows, rows), 1)
    shift = c.bit_length() - 1
    same = (r_i >> shift) == (s_i >> shift)
    half = c // 2

    def sums_matrix(reverse):
        if reverse:
            run = same & (s_i >= r_i)
            mid = same & ((s_i & (c - 1)) >= half)
        else:
            run = same & (s_i <= r_i)
            mid = same & ((s_i & (c - 1)) < half)
        mats = [m.astype(BF16) for m in (run, mid, same)]
        return jnp.concatenate(mats, axis=0), run

    sums_f, mask_f = sums_matrix(False)
    sums_b, mask_b = sums_matrix(True)

    def block(blk, z_ref, lb, s_ref, out_ref, sums, mask, reverse):
        r0 = pl.multiple_of(blk * rows, rows)
        z = z_ref[0, pl.ds(r0, rows), :].astype(F32)
        q = q_ref[0, pl.ds(r0, rows), :].astype(F32)
        v = v_ref[0, pl.ds(r0, rows), :]
        lf = jnp.log(lb + (1.0 - lb) * jax.nn.sigmoid(z))
        kk = (1.0 - lb) * jax.nn.sigmoid(-z)
        tot = sum(jnp.dot(sums, part, preferred_element_type=F32) for part in _split3(lf))
        cum, mid, end = tot[:rows], tot[rows:2 * rows], tot[2 * rows:]
        qa = (q * jnp.exp(cum - mid)).astype(BF16)
        ka = (kk * jnp.exp(mid - cum)).astype(BF16)
        att = lax.dot_general(qa, ka, (((1,), (1,)), ((), ())), preferred_element_type=F32)
        att = jnp.where(mask, att, 0.0).astype(BF16)
        o_intra = jnp.dot(att, v, preferred_element_type=F32)
        q_dec = (q * jnp.exp(cum)).astype(BF16)
        k_dec = (kk * jnp.exp(end - cum)).astype(BF16)
        dec = jnp.exp(end)
        st = s_ref[...]
        outs = [None] * n_chunks
        order = range(n_chunks - 1, -1, -1) if reverse else range(n_chunks)
        for n in order:
            sl = slice(n * c, (n + 1) * c)
            outs[n] = lax.dot_general(q_dec[sl], st.astype(BF16), (((1,), (1,)), ((), ())),
                                      preferred_element_type=F32)
            upd = lax.dot_general(v[sl], k_dec[sl], (((0,), (0,)), ((), ())),
                                  preferred_element_type=F32)
            st = st * dec[n * c:n * c + 1, :] + upd
        s_ref[...] = st
        out_ref[pl.ds(r0, rows), :] = o_intra + jnp.concatenate(outs, axis=0)

    sf_ref[...] = jnp.zeros_like(sf_ref)
    sb_ref[...] = jnp.zeros_like(sb_ref)
    lb_f = lb_ref[0:1, :]
    lb_b = lb_ref[1:2, :]

    def step(i, carry):
        block(i, zf_ref, lb_f, sf_ref, of_ref, sums_f, mask_f, False)
        j = jnp.where(i < n_ctx_blocks, n_ctx_blocks - 1 - i, n_blocks - 1 - i + n_ctx_blocks)
        block(j, zb_ref, lb_b, sb_ref, ob_ref, sums_b, mask_b, True)
        return carry

    lax.fori_loop(0, n_blocks, step, 0)

    def finish(i, carry):
        r0 = pl.multiple_of(i * rows, rows)
        o = of_ref[pl.ds(r0, rows), :] + ob_ref[pl.ds(r0, rows), :]
        gate = hg_ref[0, pl.ds(r0, rows), :].astype(F32)
        y = _rms(o) * g_ref[...] * (gate * jax.nn.sigmoid(gate))
        o_ref[0, pl.ds(r0, rows), :] = y.astype(o_ref.dtype)
        return carry

    lax.fori_loop(0, n_blocks, finish, 0)


def hgrn_scan(p, cols, lb, g, n_ctx):
    b, t, _ = p.shape
    assert t % SCAN_BLOCK == 0 and n_ctx % SCAN_BLOCK == 0

    def col(c0):
        return pl.BlockSpec((1, t, HEAD), lambda i, h: (i, 0, c0 // HEAD + h))

    return pl.pallas_call(
        functools.partial(_scan_kernel, n_ctx_blocks=n_ctx // SCAN_BLOCK),
        grid=(b, HG_HEADS),
        in_specs=[col(c0) for c0 in cols] + [pl.BlockSpec((2, HEAD), lambda i, h: (0, h)),
                                             pl.BlockSpec((1, HEAD), lambda i, h: (0, 0))],
        out_specs=pl.BlockSpec((1, t, HEAD), lambda i, h: (i, 0, h)),
        out_shape=jax.ShapeDtypeStruct((b, t, HG_HEADS * HEAD), BF16),
        scratch_shapes=[pltpu.VMEM((t, HEAD), F32), pltpu.VMEM((t, HEAD), F32),
                        pltpu.VMEM((HEAD, HEAD), F32), pltpu.VMEM((HEAD, HEAD), F32)],
        compiler_params=_params("parallel", "parallel"),
        name="hgrn_scan",
    )(p, p, p, p, p, lb, g.reshape(1, HEAD))


def dft_matrices(n):
    idx = jnp.arange(n, dtype=jnp.int32)
    ang = ((idx[:, None] * idx[None, :]) % n).astype(F32) * (2.0 * math.pi / n)
    s = 1.0 / math.sqrt(n)
    return (jnp.cos(ang) * s).astype(BF16), (jnp.sin(ang) * s).astype(BF16)


def _fn_channel_kernel(u_ref, cc_ref, sc_ref, a_ref, b_ref):
    u = u_ref[0]
    a_ref[0] = jnp.dot(u, cc_ref[...], preferred_element_type=F32).astype(BF16)
    b_ref[0] = jnp.dot(u, sc_ref[...], preferred_element_type=F32).astype(BF16)


def _fn_position_kernel(a_ref, b_ref, cl_ref, sl_ref, cx_ref, sx_ref, o_ref, *, n_ctx):
    i = pl.program_id(1)

    @pl.when(i == 0)
    def _():
        a = a_ref[0, :n_ctx, :]
        b = b_ref[0, :n_ctx, :]
        o_ref[0] = (jnp.dot(cx_ref[...], a, preferred_element_type=F32)
                    - jnp.dot(sx_ref[...], b, preferred_element_type=F32)).astype(o_ref.dtype)

    @pl.when(i > 0)
    def _():
        a = a_ref[0, n_ctx:, :]
        b = b_ref[0, n_ctx:, :]
        o_ref[0] = (jnp.dot(cl_ref[...], a, preferred_element_type=F32)
                    - jnp.dot(sl_ref[...], b, preferred_element_type=F32)).astype(o_ref.dtype)


def fourier_mix(p, col0, n_ctx, dft_chan, dft_ctx, dft_lat):
    b, t, _ = p.shape
    w = FN_GROUPS * HEAD
    tt = _tile(t, 512)
    c0 = col0 // HEAD
    blk = pl.BlockSpec((1, tt, HEAD), lambda i, j, c: (i, j, c))
    mat = pl.BlockSpec((HEAD, HEAD), lambda i, j, c: (0, 0))
    ua, ub = pl.pallas_call(
        _fn_channel_kernel,
        grid=(b, t // tt, FN_GROUPS),
        in_specs=[pl.BlockSpec((1, tt, HEAD), lambda i, j, c: (i, j, c0 + c)), mat, mat],
        out_specs=[blk, blk],
        out_shape=[jax.ShapeDtypeStruct((b, t, w), BF16)] * 2,
        compiler_params=_params("parallel", "parallel", "parallel"),
        name="fourier_channels",
    )(p, *dft_chan)
    seq = t - n_ctx
    tm = n_ctx
    assert seq % tm == 0 and tm % 8 == 0
    full = pl.BlockSpec((1, t, w), lambda i, j: (i, 0, 0))
    lat = pl.BlockSpec((tm, seq), lambda i, j: (jnp.maximum(j - 1, 0), 0))
    ctx = pl.BlockSpec((n_ctx, n_ctx), lambda i, j: (0, 0))
    return pl.pallas_call(
        functools.partial(_fn_position_kernel, n_ctx=n_ctx),
        grid=(b, 1 + seq // tm),
        in_specs=[full, full, lat, lat, ctx, ctx],
        out_specs=pl.BlockSpec((1, tm, w), lambda i, j: (i, j, 0)),
        out_shape=jax.ShapeDtypeStruct((b, t, w), BF16),
        compiler_params=_params("parallel", "parallel"),
        name="fourier_positions",
    )(ua, ub, *dft_lat, *dft_ctx)


def _merge_kernel(*refs):
    ys, gs, ws, o_ref = refs[:N_BRANCH], refs[N_BRANCH:2 * N_BRANCH], refs[2 * N_BRANCH:-1], refs[-1]
    acc = None
    for y_ref, g_ref, w_ref in zip(ys, gs, ws):
        term = jax.nn.sigmoid(g_ref[...].astype(F32)) * jnp.dot(
            y_ref[...], w_ref[0], preferred_element_type=F32)
        acc = term if acc is None else acc + term
    o_ref[...] = acc.astype(o_ref.dtype)


def merge(branches, p2, gate_col0, w_branch):
    m = p2.shape[0]
    d = w_branch.shape[-1]
    tm, tn = _tile(m, 1024), _tile(d, 512)
    y_specs = [pl.BlockSpec((tm, BRANCH_W), lambda i, j: (i, 0))] * N_BRANCH
    g_specs = [pl.BlockSpec((tm, tn), functools.partial(
        lambda i, j, off: (i, off + j), off=(gate_col0 + k * d) // tn)) for k in range(N_BRANCH)]
    w_specs = [pl.BlockSpec((1, BRANCH_W, tn), functools.partial(lambda i, j, k: (k, 0, j), k=k))
               for k in range(N_BRANCH)]
    return pl.pallas_call(
        _merge_kernel,
        grid=(m // tm, d // tn),
        in_specs=y_specs + g_specs + w_specs,
        out_specs=pl.BlockSpec((tm, tn), lambda i, j: (i, j)),
        out_shape=jax.ShapeDtypeStruct((m, d), BF16),
        compiler_params=_params("parallel", "parallel"),
        name="merge",
    )(*[y.reshape(m, BRANCH_W) for y in branches], *([p2] * N_BRANCH), *([w_branch] * N_BRANCH))


def _swiglu_kernel(a_ref, wg_ref, wu_ref, *rest, weighted):
    a = a_ref[...]
    gate = jnp.dot(a, wg_ref[0], preferred_element_type=F32)
    up = jnp.dot(a, wu_ref[0], preferred_element_type=F32)
    hidden = gate * jax.nn.sigmoid(gate) * up
    if weighted:
        hidden = hidden * rest[0][0]
    rest[-1][...] = hidden.astype(BF16)


def swiglu_up(a, wg, wu, row_w=None):
    m, d = a.shape
    e, _, f = wg.shape
    tm, tn = _tile(m, 1024), _tile(f, 512)
    nf = f // tn
    in_specs = [pl.BlockSpec((tm, d), lambda x, i, j: (i, 0)),
                pl.BlockSpec((1, d, tn), lambda x, i, j: (x, 0, j)),
                pl.BlockSpec((1, d, tn), lambda x, i, j: (x, 0, j))]
    args = [a, wg, wu]
    if row_w is not None:
        in_specs.append(pl.BlockSpec((1, tm, 1), lambda x, i, j: (x, i, 0)))
        args.append(row_w)
    return pl.pallas_call(
        functools.partial(_swiglu_kernel, weighted=row_w is not None),
        grid=(e, m // tm, nf),
        in_specs=in_specs,
        out_specs=pl.BlockSpec((tm, tn), lambda x, i, j: (i, x * nf + j)),
        out_shape=jax.ShapeDtypeStruct((m, e * f), BF16),
        compiler_params=_params("parallel", "parallel", "parallel"),
        name="swiglu_up",
    )(*args)


def _router_kernel(a_ref, w_ref, b_ref, o_ref):
    logits = lax.dot_general(w_ref[...], a_ref[...], (((1,), (1,)), ((), ())),
                             preferred_element_type=F32) + b_ref[...]
    n_exp = float(logits.shape[0])
    e = lax.broadcasted_iota(jnp.int32, logits.shape, 0).astype(F32)
    m1 = jnp.max(logits, axis=0, keepdims=True)
    i1 = jnp.min(jnp.where(logits == m1, e, n_exp), axis=0, keepdims=True)
    rest = jnp.where(e == i1, -jnp.inf, logits)
    m2 = jnp.max(rest, axis=0, keepdims=True)
    i2 = jnp.min(jnp.where(rest == m2, e, n_exp), axis=0, keepdims=True)
    x = jnp.exp(m2 - m1)
    w1 = 1.0 / (1.0 + x)
    o_ref[...] = jnp.where(e == i1, w1, 0.0) + jnp.where(e == i2, x * w1, 0.0)


def router_weights(a, w_router, b_router):
    m, d = a.shape
    n_exp = w_router.shape[1]
    tm = _tile(m, 1024)
    return pl.pallas_call(
        _router_kernel,
        grid=(m // tm,),
        in_specs=[pl.BlockSpec((tm, d), lambda i: (i, 0)),
                  pl.BlockSpec((n_exp, d), lambda i: (0, 0)),
                  pl.BlockSpec((n_exp, 1), lambda i: (0, 0))],
        out_specs=pl.BlockSpec((n_exp, tm), lambda i: (0, i)),
        out_shape=jax.ShapeDtypeStruct((n_exp, m), F32),
        compiler_params=_params("parallel"),
        name="router",
    )(a, w_router.T.astype(BF16), b_router.reshape(n_exp, 1).astype(F32))


def kernel(x, c, ctx, c_ctx, w_mod, b_mod, g_pre_mix, g_post_mix, g_pre_ffn, g_post_ffn, w_in,
           hgrn_lb, hgrn_norm_g, diff_lambda, diff_norm_g, qk_norm_q, qk_norm_k, w_branch, w_out,
           w_ff_gate, w_ff_up, w_ff_down, w_router, b_router, w_moe_gate, w_moe_up, w_moe_down):
    n_batch, seq, d = x.shape
    n_ctx = ctx.shape[1]
    depth = w_in.shape[0]
    t = n_ctx + seq
    m = n_batch * t

    hw = HG_HEADS * HEAD
    kv_sizes = (hw, hw, hw, DF_HEADS * 2 * DF_DIM, DF_HEADS * 2 * DF_DIM,
                GQ_KV_HEADS * HEAD, GQ_KV_HEADS * HEAD)
    q_sizes = (hw, hw, DF_HEADS * 2 * DF_DIM, GQ_HEADS * HEAD, FN_GROUPS * HEAD, N_BRANCH * d)
    offs = np.concatenate([[0], np.cumsum(kv_sizes + q_sizes)]).astype(int)
    (c_ff, c_fb, c_hv, c_dfk, c_dfv, c_gqk, c_gqv,
     c_hq, c_hg, c_dfq, c_gqq, c_fu, c_gl) = [int(o) for o in offs[:-1]]

    rope_df = rope_tables(n_ctx, seq, DF_DIM)
    rope_gq = rope_tables(n_ctx, seq, HEAD)
    dft_chan = dft_matrices(HEAD)
    dft_ctx = dft_matrices(n_ctx)
    dft_lat = dft_matrices(seq)

    lb_all = jnp.cumsum(jax.nn.softmax(hgrn_lb.astype(F32), axis=0), axis=0)
    lb_all = lb_all - lb_all[:1]

    n_cond = n_batch + 1
    pad = (-n_cond) % 16
    cond = jnp.concatenate([c, c_ctx[None, :], jnp.zeros((pad, d), F32)], axis=0)
    mods_all = modulation(cond, w_mod, b_mod)[:, :n_cond].reshape(depth, n_cond, 6, d)

    h = jnp.concatenate([ctx, x], axis=1)
    ones = jnp.ones((HEAD,), F32)

    for l in range(depth):
        mods = mods_all[l]
        lam_init = 0.8 - 0.6 * math.exp(-0.3 * l)
        lv = diff_lambda[l].astype(F32)
        lam = jnp.exp(jnp.sum(lv[0] * lv[1])) - jnp.exp(jnp.sum(lv[2] * lv[3])) + lam_init

        a = norm_mod(h, g_pre_mix[l], mods, 0, 1, n_ctx)
        p2 = matmul(a.reshape(m, d), w_in[l].astype(BF16), BF16, tm=1024, tn=1536)
        p = p2.reshape(n_batch, t, -1)

        y_hg = hgrn_scan(p, (c_hq, c_ff, c_fb, c_hv, c_hg), lb_all[l], hgrn_norm_g[l], n_ctx)

        df_q1, df_q2 = rope_prep(p, c_dfq, DF_HEADS * HEAD, rope_df, ones, dim=DF_DIM,
                                 scale=DF_DIM ** -0.5, split=True)
        df_k = rope_prep(p, c_dfk, DF_HEADS * HEAD, rope_df, ones, dim=DF_DIM)
        y_df = attention(df_q1, df_k, p, c_dfv, DF_HEADS, DF_HEADS, n_ctx, q2=df_q2, lam=lam,
                         g=diff_norm_g[l], post_scale=1.0 - lam_init)

        y_fn = fourier_mix(p, c_fu, n_ctx, dft_chan, dft_ctx, dft_lat)

        gq_q = rope_prep(p, c_gqq, GQ_HEADS * HEAD, rope_gq, qk_norm_q[l], dim=HEAD, norm=True,
                         scale=HEAD ** -0.5)
        gq_k = rope_prep(p, c_gqk, GQ_KV_HEADS * HEAD, rope_gq, qk_norm_k[l], dim=HEAD, norm=True)
        y_gq = attention(gq_q, gq_k, p, c_gqv, GQ_HEADS, GQ_KV_HEADS, n_ctx)

        merged = merge((y_hg, y_df, y_fn, y_gq), p2, c_gl, w_branch[l].astype(BF16))
        y = matmul(merged, w_out[l].astype(BF16), F32)
        h = residual(h, y, g_post_mix[l], mods, 2, n_ctx)

        f_in = norm_mod(h, g_pre_ffn[l], mods, 3, 4, n_ctx).reshape(m, d)
        j = l // 2
        if l % 2 == 0:
            hidden = swiglu_up(f_in, w_ff_gate[j][None].astype(BF16), w_ff_up[j][None].astype(BF16))
            y = matmul(hidden, w_ff_down[j].astype(BF16), F32, tm=1024, tn=512)
        else:
            n_exp = w_router.shape[-1]
            row_w = router_weights(f_in, w_router[j], b_router[j]).reshape(n_exp, m, 1)
            hidden = swiglu_up(f_in, w_moe_gate[j].astype(BF16), w_moe_up[j].astype(BF16), row_w)
            y = matmul(hidden, w_moe_down[j].astype(BF16).reshape(-1, d), F32, tk=2048)
        h = residual(h, y, g_post_ffn[l], mods, 5, n_ctx)

    return h[:, n_ctx:]
```

```python
import functools
import math

import jax
import jax.numpy as jnp
import numpy as np
from jax import lax
from jax.experimental import pallas as pl
from jax.experimental.pallas import tpu as pltpu

F32 = jnp.float32
BF16 = jnp.bfloat16

EPS = 1e-6
GRID_W = 64
ROPE_THETA = 10000.0
HEAD = 128
HG_HEADS = 4
GLA_CHUNK = 32
DF_HEADS = 4
DF_DIM = 64
FN_GROUPS = 4
GQ_HEADS = 4
GQ_KV_HEADS = 2
N_BRANCH = 4
BRANCH_W = 512
TOP_K = 2

V7X_VMEM_LIMIT_BYTES = 56 * 1024 * 1024
SCAN_BLOCK = 256


def _params(*sem):
    return pltpu.CompilerParams(dimension_semantics=sem, vmem_limit_bytes=V7X_VMEM_LIMIT_BYTES)


def _tile(n, pref, align=128):
    if n <= pref:
        return n
    t = pref - pref % align
    while n % t:
        t -= align
    return t


def _mm_kernel(a_ref, w_ref, o_ref):
    o_ref[...] = jnp.dot(a_ref[...], w_ref[...], preferred_element_type=F32).astype(o_ref.dtype)


def _mm_acc_kernel(a_ref, w_ref, o_ref, acc_ref):
    k = pl.program_id(2)

    @pl.when(k == 0)
    def _():
        acc_ref[...] = jnp.zeros_like(acc_ref)

    acc_ref[...] += jnp.dot(a_ref[...], w_ref[...], preferred_element_type=F32)

    @pl.when(k == pl.num_programs(2) - 1)
    def _():
        o_ref[...] = acc_ref[...].astype(o_ref.dtype)


def matmul(a, w, out_dtype, tm=1024, tn=1024, tk=None):
    m, k = a.shape
    n = w.shape[1]
    tm, tn = _tile(m, tm), _tile(n, tn)
    if tk is None or tk >= k:
        return pl.pallas_call(
            _mm_kernel,
            grid=(m // tm, n // tn),
            in_specs=[pl.BlockSpec((tm, k), lambda i, j: (i, 0)),
                      pl.BlockSpec((k, tn), lambda i, j: (0, j))],
            out_specs=pl.BlockSpec((tm, tn), lambda i, j: (i, j)),
            out_shape=jax.ShapeDtypeStruct((m, n), out_dtype),
            compiler_params=_params("parallel", "parallel"),
            name="matmul",
        )(a, w)
    tk = _tile(k, tk)
    return pl.pallas_call(
        _mm_acc_kernel,
        grid=(m // tm, n // tn, k // tk),
        in_specs=[pl.BlockSpec((tm, tk), lambda i, j, q: (i, q)),
                  pl.BlockSpec((tk, tn), lambda i, j, q: (q, j))],
        out_specs=pl.BlockSpec((tm, tn), lambda i, j, q: (i, j)),
        out_shape=jax.ShapeDtypeStruct((m, n), out_dtype),
        scratch_shapes=[pltpu.VMEM((tm, tn), F32)],
        compiler_params=_params("parallel", "parallel", "arbitrary"),
        name="matmul_acc",
    )(a, w)


def _mod_kernel(s_ref, w_ref, b_ref, o_ref):
    s = s_ref[...]
    s = (s * jax.nn.sigmoid(s)).astype(BF16)
    o_ref[0] = jnp.dot(s, w_ref[0].astype(BF16), preferred_element_type=F32) + b_ref[0]


def modulation(cond, w_mod, b_mod):
    depth, d, width = w_mod.shape
    r = cond.shape[0]
    tn = _tile(width, 1024)
    return pl.pallas_call(
        _mod_kernel,
        grid=(depth, width // tn),
        in_specs=[pl.BlockSpec((r, d), lambda l, j: (0, 0)),
                  pl.BlockSpec((1, d, tn), lambda l, j: (l, 0, j)),
                  pl.BlockSpec((1, 1, tn), lambda l, j: (l, 0, j))],
        out_specs=pl.BlockSpec((1, r, tn), lambda l, j: (l, 0, j)),
        out_shape=jax.ShapeDtypeStruct((depth, r, width), F32),
        compiler_params=_params("parallel", "parallel"),
        name="modulation",
    )(cond, w_mod, b_mod.reshape(depth, 1, width))


def _rms(x):
    return x * lax.rsqrt(jnp.mean(x * x, axis=-1, keepdims=True) + EPS)


def _norm_mod_kernel(h_ref, g_ref, mod_ref, o_ref, *, shift_idx, scale_idx):
    y = _rms(h_ref[0]) * g_ref[...]
    shift = mod_ref[0, shift_idx:shift_idx + 1, :]
    scale = mod_ref[0, scale_idx:scale_idx + 1, :]
    o_ref[0] = (y * (1.0 + scale) + shift).astype(o_ref.dtype)


def _mod_index(n_ctx_tiles, n_batch):
    return lambda b, t: (jnp.where(t < n_ctx_tiles, n_batch, b), 0, 0)


def norm_mod(h, g, mods, shift_idx, scale_idx, n_ctx, out_dtype=BF16):
    b, t, d = h.shape
    tt = _tile(n_ctx, 256)
    return pl.pallas_call(
        functools.partial(_norm_mod_kernel, shift_idx=shift_idx, scale_idx=scale_idx),
        grid=(b, t // tt),
        in_specs=[pl.BlockSpec((1, tt, d), lambda i, j: (i, j, 0)),
                  pl.BlockSpec((1, d), lambda i, j: (0, 0)),
                  pl.BlockSpec((1, 6, d), _mod_index(n_ctx // tt, b))],
        out_specs=pl.BlockSpec((1, tt, d), lambda i, j: (i, j, 0)),
        out_shape=jax.ShapeDtypeStruct((b, t, d), out_dtype),
        compiler_params=_params("parallel", "parallel"),
        name="norm_mod",
    )(h, g.reshape(1, d), mods)


def _resid_kernel(h_ref, y_ref, g_ref, mod_ref, o_ref, *, gate_idx):
    gate = mod_ref[0, gate_idx:gate_idx + 1, :]
    o_ref[0] = h_ref[0] + gate * (_rms(y_ref[0]) * g_ref[...])


def residual(h, y, g, mods, gate_idx, n_ctx):
    b, t, d = h.shape
    tt = _tile(n_ctx, 256)
    return pl.pallas_call(
        functools.partial(_resid_kernel, gate_idx=gate_idx),
        grid=(b, t // tt),
        in_specs=[pl.BlockSpec((1, tt, d), lambda i, j: (i, j, 0)),
                  pl.BlockSpec((1, tt, d), lambda i, j: (i, j, 0)),
                  pl.BlockSpec((1, d), lambda i, j: (0, 0)),
                  pl.BlockSpec((1, 6, d), _mod_index(n_ctx // tt, b))],
        out_specs=pl.BlockSpec((1, tt, d), lambda i, j: (i, j, 0)),
        out_shape=jax.ShapeDtypeStruct((b, t, d), F32),
        input_output_aliases={0: 0},
        compiler_params=_params("parallel", "parallel"),
        name="residual",
    )(h, y.reshape(b, t, d), g.reshape(1, d), mods)


def rope_tables(n_ctx, seq, dim):
    quarter = dim // 4
    inv = ROPE_THETA ** (-jnp.arange(quarter, dtype=F32) / quarter)
    pos = jnp.arange(seq)
    ar = (pos // GRID_W).astype(F32)[:, None] * inv
    ac = (pos % GRID_W).astype(F32)[:, None] * inv
    ang = jnp.concatenate([ar, ar, ac, ac], axis=-1)
    ang = jnp.tile(ang, (1, HEAD // dim))
    ang = jnp.concatenate([jnp.zeros((n_ctx, HEAD), F32), ang], axis=0)
    first = ((jnp.arange(HEAD) % dim) % (dim // 2)) < quarter
    cos, sin = jnp.cos(ang), jnp.sin(ang)
    return cos, jnp.where(first, -sin, 0.0), jnp.where(first, 0.0, sin)


def _rope(x, cos_ref, sa_ref, sb_ref, quarter):
    return (x * cos_ref[...] + pltpu.roll(x, HEAD - quarter, 1) * sa_ref[...]
            + pltpu.roll(x, quarter, 1) * sb_ref[...])


def _qk_prep_kernel(dfq_ref, dfk_ref, gqq_ref, gqk_ref, dcos, dsa, dsb, gcos, gsa, gsb, gq_ref, gk_ref,
                    q1_ref, q2_ref, dk_ref, gqo_ref, gko_ref):
    lane = lax.broadcasted_iota(jnp.int32, dcos.shape, 1)
    for h in range(DF_HEADS):
        sl = slice(h * HEAD, (h + 1) * HEAD)
        y = _rope(dfq_ref[0, :, sl].astype(F32), dcos, dsa, dsb, DF_DIM // 4) * DF_DIM ** -0.5
        q1_ref[0, :, sl] = jnp.where(lane < DF_DIM, y, 0.0).astype(BF16)
        q2_ref[0, :, sl] = jnp.where(lane >= DF_DIM, y, 0.0).astype(BF16)
        dk_ref[0, :, sl] = _rope(dfk_ref[0, :, sl].astype(F32), dcos, dsa, dsb,
                                 DF_DIM // 4).astype(BF16)
    for h in range(GQ_HEADS):
        sl = slice(h * HEAD, (h + 1) * HEAD)
        x = _rms(gqq_ref[0, :, sl].astype(F32)) * gq_ref[...]
        gqo_ref[0, :, sl] = (_rope(x, gcos, gsa, gsb, HEAD // 4) * HEAD ** -0.5).astype(BF16)
    for h in range(GQ_KV_HEADS):
        sl = slice(h * HEAD, (h + 1) * HEAD)
        x = _rms(gqk_ref[0, :, sl].astype(F32)) * gk_ref[...]
        gko_ref[0, :, sl] = _rope(x, gcos, gsa, gsb, HEAD // 4).astype(BF16)


def qk_prep(p, cols, rope_df, rope_gq, g_q, g_k):
    b, t, _ = p.shape
    tt = _tile(t, 1088, 16)
    widths = (DF_HEADS * HEAD, DF_HEADS * HEAD, GQ_HEADS * HEAD, GQ_KV_HEADS * HEAD)
    assert all(c % w == 0 for c, w in zip(cols, widths))

    def col_spec(c0, w):
        return pl.BlockSpec((1, tt, w), lambda i, j: (i, j, c0 // w))

    def out_spec(w):
        return pl.BlockSpec((1, tt, w), lambda i, j: (i, j, 0))

    tab = pl.BlockSpec((tt, HEAD), lambda i, j: (j, 0))
    vec = pl.BlockSpec((1, HEAD), lambda i, j: (0, 0))
    out_w = (widths[0], widths[0], widths[1], widths[2], widths[3])
    return pl.pallas_call(
        _qk_prep_kernel,
        grid=(b, t // tt),
        in_specs=[col_spec(c, w) for c, w in zip(cols, widths)] + [tab] * 6 + [vec, vec],
        out_specs=[out_spec(w) for w in out_w],
        out_shape=[jax.ShapeDtypeStruct((b, t, w), BF16) for w in out_w],
        compiler_params=_params("parallel", "parallel"),
        name="qk_prep",
    )(p, p, p, p, *rope_df, *rope_gq, g_q.reshape(1, HEAD), g_k.reshape(1, HEAD))


def _softmax_pv(q, k, v):
    s = lax.dot_general(q, k, (((1,), (1,)), ((), ())), preferred_element_type=F32)
    e = jnp.exp(s - jnp.max(s, axis=-1, keepdims=True))
    o = jnp.dot(e.astype(BF16), v, preferred_element_type=F32)
    return o / jnp.sum(e, axis=-1, keepdims=True)


def _attn_kernel(*refs, diff, n_ctx, n_ctx_tiles, post_scale):
    if diff:
        lam_ref, q1_ref, q2_ref, k_ref, v_ref, g_ref, o_ref = refs
    else:
        q1_ref, k_ref, v_ref, o_ref = refs

    def run(nk):
        k = k_ref[0, :nk, :]
        v = v_ref[0, :nk, :]
        o = _softmax_pv(q1_ref[0], k, v)
        if diff:
            o = o - lam_ref[...] * _softmax_pv(q2_ref[0], k, v)
            o = _rms(o) * g_ref[...] * post_scale
        o_ref[0] = o.astype(o_ref.dtype)

    t = pl.program_id(2)

    @pl.when(t < n_ctx_tiles)
    def _():
        run(n_ctx)

    @pl.when(t >= n_ctx_tiles)
    def _():
        run(k_ref.shape[1])


def attention(q, k, v, v_col0, n_heads, n_kv_heads, n_ctx, q2=None, lam=None, g=None,
              post_scale=1.0):
    b, t, _ = q.shape
    tq = _tile(n_ctx, 256)
    rep = n_heads // n_kv_heads
    vc = v_col0 // HEAD
    diff = q2 is not None
    qspec = pl.BlockSpec((1, tq, HEAD), lambda i, h, j: (i, j, h))
    kspec = pl.BlockSpec((1, t, HEAD), lambda i, h, j: (i, 0, h // rep))
    vspec = pl.BlockSpec((1, t, HEAD), lambda i, h, j: (i, 0, vc + h // rep))
    if diff:
        one = pl.BlockSpec((1, 1), lambda i, h, j: (0, 0))
        in_specs = [one, qspec, qspec, kspec, vspec, pl.BlockSpec((1, HEAD), lambda i, h, j: (0, 0))]
        args = (lam.reshape(1, 1).astype(F32), q, q2, k, v, g.reshape(1, HEAD))
    else:
        in_specs = [qspec, kspec, vspec]
        args = (q, k, v)
    return pl.pallas_call(
        functools.partial(_attn_kernel, diff=diff, n_ctx=n_ctx, n_ctx_tiles=n_ctx // tq,
                          post_scale=post_scale),
        grid=(b, n_heads, t // tq),
        in_specs=in_specs,
        out_specs=qspec,
        out_shape=jax.ShapeDtypeStruct((b, t, n_heads * HEAD), BF16),
        compiler_params=_params("parallel", "parallel", "parallel"),
        name="diff_attention" if diff else "gq_attention",
    )(*args)


def _split3(x):
    a = x.astype(BF16)
    r = x - a.astype(F32)
    b = r.astype(BF16)
    c = (r - b.astype(F32)).astype(BF16)
    return a, b, c


def _scan_kernel(q_ref, zf_ref, zb_ref, v_ref, hg_ref, lb_ref, g_ref, o_ref,
                 of_ref, ob_ref, sf_ref, sb_ref, *, n_ctx_blocks):
    rows = SCAN_BLOCK
    c = GLA_CHUNK
    n_chunks = rows // c
    n_blocks = q_ref.shape[1] // rows

    r_i = lax.broadcasted_iota(jnp.int32, (rows, rows), 0)
    s_i = lax.broadcasted_iota(jnp.int32, (rows, rows), 1)
    shift = c.bit_length() - 1
    same = (r_i >> shift) == (s_i >> shift)
    half = c // 2

    def sums_matrix(reverse):
        if reverse:
            run = same & (s_i >= r_i)
            mid = same & ((s_i & (c - 1)) >= half)
        else:
            run = same & (s_i <= r_i)
            mid = same & ((s_i & (c - 1)) < half)
        mats = [m.astype(BF16) for m in (run, mid, same)]
        return jnp.concatenate(mats, axis=0), run

    sums_f, mask_f = sums_matrix(False)
    sums_b, mask_b = sums_matrix(True)

    def block(blk, z_ref, lb, s_ref, out_ref, sums, mask, reverse):
        r0 = pl.multiple_of(blk * rows, rows)
        z = z_ref[0, pl.ds(r0, rows), :].astype(F32)
        q = q_ref[0, pl.ds(r0, rows), :].astype(F32)
        v = v_ref[0, pl.ds(r0, rows), :]
        lf = jnp.log(lb + (1.0 - lb) * jax.nn.sigmoid(z))
        kk = (1.0 - lb) * jax.nn.sigmoid(-z)
        tot = sum(jnp.dot(sums, part, preferred_element_type=F32) for part in _split3(lf))
        cum, mid, end = tot[:rows], tot[rows:2 * rows], tot[2 * rows:]
        qa = (q * jnp.exp(cum - mid)).astype(BF16)
        ka = (kk * jnp.exp(mid - cum)).astype(BF16)
        att = lax.dot_general(qa, ka, (((1,), (1,)), ((), ())), preferred_element_type=F32)
        att = jnp.where(mask, att, 0.0).astype(BF16)
        o_intra = jnp.dot(att, v, preferred_element_type=F32)
        q_dec = (q * jnp.exp(cum)).astype(BF16)
        k_dec = (kk * jnp.exp(end - cum)).astype(BF16)
        dec = jnp.exp(end)
        st = s_ref[...]
        outs = [None] * n_chunks
        order = range(n_chunks - 1, -1, -1) if reverse else range(n_chunks)
        for n in order:
            sl = slice(n * c, (n + 1) * c)
            outs[n] = lax.dot_general(q_dec[sl], st.astype(BF16), (((1,), (1,)), ((), ())),
                                      preferred_element_type=F32)
            upd = lax.dot_general(v[sl], k_dec[sl], (((0,), (0,)), ((), ())),
                                  preferred_element_type=F32)
            st = st * dec[n * c:n * c + 1, :] + upd
        s_ref[...] = st
        out_ref[pl.ds(r0, rows), :] = o_intra + jnp.concatenate(outs, axis=0)

    sf_ref[...] = jnp.zeros_like(sf_ref)
    sb_ref[...] = jnp.zeros_like(sb_ref)
    lb_f = lb_ref[0:1, :]
    lb_b = lb_ref[1:2, :]

    def step(i, carry):
        block(i, zf_ref, lb_f, sf_ref, of_ref, sums_f, mask_f, False)
        j = jnp.where(i < n_ctx_blocks, n_ctx_blocks - 1 - i, n_blocks - 1 - i + n_ctx_blocks)
        block(j, zb_ref, lb_b, sb_ref, ob_ref, sums_b, mask_b, True)
        return carry

    lax.fori_loop(0, n_blocks, step, 0)

    def finish(i, carry):
        r0 = pl.multiple_of(i * rows, rows)
        o = of_ref[pl.ds(r0, rows), :] + ob_ref[pl.ds(r0, rows), :]
        gate = hg_ref[0, pl.ds(r0, rows), :].astype(F32)
        y = _rms(o) * g_ref[...] * (gate * jax.nn.sigmoid(gate))
        o_ref[0, pl.ds(r0, rows), :] = y.astype(o_ref.dtype)
        return carry

    lax.fori_loop(0, n_blocks, finish, 0)


def hgrn_scan(p, cols, lb, g, n_ctx):
    b, t, _ = p.shape
    assert t % SCAN_BLOCK == 0 and n_ctx % SCAN_BLOCK == 0

    def col(c0):
        return pl.BlockSpec((1, t, HEAD), lambda i, h: (i, 0, c0 // HEAD + h))

    return pl.pallas_call(
        functools.partial(_scan_kernel, n_ctx_blocks=n_ctx // SCAN_BLOCK),
        grid=(b, HG_HEADS),
        in_specs=[col(c0) for c0 in cols] + [pl.BlockSpec((2, HEAD), lambda i, h: (0, h)),
                                             pl.BlockSpec((1, HEAD), lambda i, h: (0, 0))],
        out_specs=pl.BlockSpec((1, t, HEAD), lambda i, h: (i, 0, h)),
        out_shape=jax.ShapeDtypeStruct((b, t, HG_HEADS * HEAD), BF16),
        scratch_shapes=[pltpu.VMEM((t, HEAD), F32), pltpu.VMEM((t, HEAD), F32),
                        pltpu.VMEM((HEAD, HEAD), F32), pltpu.VMEM((HEAD, HEAD), F32)],
        compiler_params=_params("parallel", "parallel"),
        name="hgrn_scan",
    )(p, p, p, p, p, lb, g.reshape(1, HEAD))


def dft_matrices(n):
    idx = jnp.arange(n, dtype=jnp.int32)
    ang = ((idx[:, None] * idx[None, :]) % n).astype(F32) * (2.0 * math.pi / n)
    s = 1.0 / math.sqrt(n)
    return (jnp.cos(ang) * s).astype(BF16), (jnp.sin(ang) * s).astype(BF16)


def _fn_channel_kernel(u_ref, cc_ref, sc_ref, a_ref, b_ref):
    for grp in range(FN_GROUPS):
        sl = slice(grp * HEAD, (grp + 1) * HEAD)
        u = u_ref[0, :, sl]
        a_ref[0, :, sl] = jnp.dot(u, cc_ref[...], preferred_element_type=F32).astype(BF16)
        b_ref[0, :, sl] = jnp.dot(u, sc_ref[...], preferred_element_type=F32).astype(BF16)


def _fn_position_kernel(a_ref, b_ref, cl_ref, sl_ref, cx_ref, sx_ref, o_ref, *, n_ctx):
    i = pl.program_id(1)

    @pl.when(i == 0)
    def _():
        a = a_ref[0, :n_ctx, :]
        b = b_ref[0, :n_ctx, :]
        o_ref[0] = (jnp.dot(cx_ref[...], a, preferred_element_type=F32)
                    - jnp.dot(sx_ref[...], b, preferred_element_type=F32)).astype(o_ref.dtype)

    @pl.when(i > 0)
    def _():
        a = a_ref[0, n_ctx:, :]
        b = b_ref[0, n_ctx:, :]
        o_ref[0] = (jnp.dot(cl_ref[...], a, preferred_element_type=F32)
                    - jnp.dot(sl_ref[...], b, preferred_element_type=F32)).astype(o_ref.dtype)


def fourier_mix(p, col0, n_ctx, dft_chan, dft_ctx, dft_lat):
    b, t, _ = p.shape
    w = FN_GROUPS * HEAD
    tt = _tile(t, 1088, 16)
    assert col0 % w == 0
    blk = pl.BlockSpec((1, tt, w), lambda i, j: (i, j, 0))
    mat = pl.BlockSpec((HEAD, HEAD), lambda i, j: (0, 0))
    ua, ub = pl.pallas_call(
        _fn_channel_kernel,
        grid=(b, t // tt),
        in_specs=[pl.BlockSpec((1, tt, w), lambda i, j: (i, j, col0 // w)), mat, mat],
        out_specs=[blk, blk],
        out_shape=[jax.ShapeDtypeStruct((b, t, w), BF16)] * 2,
        compiler_params=_params("parallel", "parallel"),
        name="fourier_channels",
    )(p, *dft_chan)
    seq = t - n_ctx
    tm = n_ctx
    assert seq % tm == 0 and tm % 8 == 0
    full = pl.BlockSpec((1, t, w), lambda i, j: (i, 0, 0))
    lat = pl.BlockSpec((tm, seq), lambda i, j: (jnp.maximum(j - 1, 0), 0))
    ctx = pl.BlockSpec((n_ctx, n_ctx), lambda i, j: (0, 0))
    return pl.pallas_call(
        functools.partial(_fn_position_kernel, n_ctx=n_ctx),
        grid=(b, 1 + seq // tm),
        in_specs=[full, full, lat, lat, ctx, ctx],
        out_specs=pl.BlockSpec((1, tm, w), lambda i, j: (i, j, 0)),
        out_shape=jax.ShapeDtypeStruct((b, t, w), BF16),
        compiler_params=_params("parallel", "parallel"),
        name="fourier_positions",
    )(ua, ub, *dft_lat, *dft_ctx)


def _merge_kernel(*refs):
    ys, gs, ws, o_ref = refs[:N_BRANCH], refs[N_BRANCH:2 * N_BRANCH], refs[2 * N_BRANCH:-1], refs[-1]
    acc = None
    for y_ref, g_ref, w_ref in zip(ys, gs, ws):
        term = jax.nn.sigmoid(g_ref[...].astype(F32)) * jnp.dot(
            y_ref[...], w_ref[0], preferred_element_type=F32)
        acc = term if acc is None else acc + term
    o_ref[...] = acc.astype(o_ref.dtype)


def merge(branches, p2, gate_col0, w_branch):
    m = p2.shape[0]
    d = w_branch.shape[-1]
    tm, tn = _tile(m, 1024), _tile(d, 512)
    y_specs = [pl.BlockSpec((tm, BRANCH_W), lambda i, j: (i, 0))] * N_BRANCH
    g_specs = [pl.BlockSpec((tm, tn), functools.partial(
        lambda i, j, off: (i, off + j), off=(gate_col0 + k * d) // tn)) for k in range(N_BRANCH)]
    w_specs = [pl.BlockSpec((1, BRANCH_W, tn), functools.partial(lambda i, j, k: (k, 0, j), k=k))
               for k in range(N_BRANCH)]
    return pl.pallas_call(
        _merge_kernel,
        grid=(m // tm, d // tn),
        in_specs=y_specs + g_specs + w_specs,
        out_specs=pl.BlockSpec((tm, tn), lambda i, j: (i, j)),
        out_shape=jax.ShapeDtypeStruct((m, d), BF16),
        compiler_params=_params("parallel", "parallel"),
        name="merge",
    )(*[y.reshape(m, BRANCH_W) for y in branches], *([p2] * N_BRANCH), *([w_branch] * N_BRANCH))


def _swiglu_kernel(a_ref, wg_ref, wu_ref, o_ref):
    a = a_ref[...]
    gate = jnp.dot(a, wg_ref[...], preferred_element_type=F32)
    up = jnp.dot(a, wu_ref[...], preferred_element_type=F32)
    o_ref[...] = (gate * jax.nn.sigmoid(gate) * up).astype(BF16)


def swiglu_up(a, wg, wu):
    m, d = a.shape
    f = wg.shape[1]
    tm, tn = _tile(m, 1024), _tile(f, 512)
    return pl.pallas_call(
        _swiglu_kernel,
        grid=(m // tm, f // tn),
        in_specs=[pl.BlockSpec((tm, d), lambda i, j: (i, 0)),
                  pl.BlockSpec((d, tn), lambda i, j: (0, j)),
                  pl.BlockSpec((d, tn), lambda i, j: (0, j))],
        out_specs=pl.BlockSpec((tm, tn), lambda i, j: (i, j)),
        out_shape=jax.ShapeDtypeStruct((m, f), BF16),
        compiler_params=_params("parallel", "parallel"),
        name="swiglu_up",
    )(a, wg, wu)


MOE_ROW_TILE = 512
MOE_INFO_ROWS = 8


def _router_kernel(a_ref, w_ref, b_ref, info_ref, cnt_ref, carry_ref):
    @pl.when(pl.program_id(0) == 0)
    def _():
        carry_ref[...] = jnp.zeros_like(carry_ref)

    logits = lax.dot_general(w_ref[...], a_ref[...].astype(BF16), (((1,), (1,)), ((), ())),
                             preferred_element_type=F32) + b_ref[...]
    n_exp, tm = logits.shape
    e = lax.broadcasted_iota(jnp.int32, logits.shape, 0).astype(F32)
    m1 = jnp.max(logits, axis=0, keepdims=True)
    i1 = jnp.min(jnp.where(logits == m1, e, float(n_exp)), axis=0, keepdims=True)
    rest = jnp.where(e == i1, -jnp.inf, logits)
    m2 = jnp.max(rest, axis=0, keepdims=True)
    i2 = jnp.min(jnp.where(rest == m2, e, float(n_exp)), axis=0, keepdims=True)
    x = jnp.exp(m2 - m1)
    w1 = 1.0 / (1.0 + x)
    w2 = x * w1

    chosen = jnp.where((e == i1) | (e == i2), 1.0, 0.0)
    s_i = lax.broadcasted_iota(jnp.int32, (tm, tm), 0)
    t_i = lax.broadcasted_iota(jnp.int32, (tm, tm), 1)
    earlier = (s_i < t_i).astype(BF16)
    rank = jnp.dot(chosen.astype(BF16), earlier, preferred_element_type=F32) + carry_ref[:, 0:1]
    r1 = jnp.sum(jnp.where(e == i1, rank, 0.0), axis=0, keepdims=True)
    r2 = jnp.sum(jnp.where(e == i2, rank, 0.0), axis=0, keepdims=True)
    carry_ref[...] = carry_ref[...] + jnp.sum(chosen, axis=1, keepdims=True)
    cnt_ref[...] = carry_ref[...]

    row = lax.broadcasted_iota(jnp.int32, (MOE_INFO_ROWS, tm), 0)
    info = jnp.zeros((MOE_INFO_ROWS, tm), F32)
    for k, val in enumerate((i1, i2, w1, w2, r1, r2)):
        info = jnp.where(row == k, val, info)
    info_ref[...] = info


def route(a, w_router, b_router):
    m, d = a.shape
    n_exp = w_router.shape[1]
    tm = _tile(m, 1024)
    return pl.pallas_call(
        _router_kernel,
        grid=(m // tm,),
        in_specs=[pl.BlockSpec((tm, d), lambda i: (i, 0)),
                  pl.BlockSpec((n_exp, d), lambda i: (0, 0)),
                  pl.BlockSpec((n_exp, 1), lambda i: (0, 0))],
        out_specs=[pl.BlockSpec((MOE_INFO_ROWS, tm), lambda i: (0, i)),
                   pl.BlockSpec((n_exp, HEAD), lambda i: (0, 0))],
        out_shape=[jax.ShapeDtypeStruct((MOE_INFO_ROWS, m), F32),
                   jax.ShapeDtypeStruct((n_exp, HEAD), F32)],
        scratch_shapes=[pltpu.VMEM((n_exp, HEAD), F32)],
        compiler_params=_params("arbitrary"),
        name="router",
    )(a, w_router.T.astype(BF16), b_router.reshape(n_exp, 1).astype(F32))


def _row_copies(src_ref, dst_ref, src_row, dst_row, sem):
    return pltpu.make_async_copy(src_ref.at[pl.ds(src_row, 1), :], dst_ref.at[pl.ds(dst_row, 1), :], sem)


def _wait_rows(src_ref, dst_ref, n_rows, sem):
    pltpu.make_async_copy(src_ref.at[pl.ds(0, n_rows), :], dst_ref.at[pl.ds(0, n_rows), :], sem).wait()


def _dispatch_kernel(d1_ref, d2_ref, x_ref, init_ref, xg_ref, sem):
    del init_ref
    tt = x_ref.shape[0]
    base = pl.program_id(0) * tt

    def send(r, carry):
        _row_copies(x_ref, xg_ref, r, d1_ref[base + r], sem).start()
        _row_copies(x_ref, xg_ref, r, d2_ref[base + r], sem).start()
        return carry

    lax.fori_loop(0, tt, send, 0, unroll=8)
    _wait_rows(x_ref, xg_ref, tt, sem)
    _wait_rows(x_ref, xg_ref, tt, sem)


def dispatch(x, dest1, dest2, n_rows):
    m, d = x.shape
    tt = _tile(m, 256)
    any_spec = pl.BlockSpec(memory_space=pl.ANY)
    return pl.pallas_call(
        _dispatch_kernel,
        grid_spec=pltpu.PrefetchScalarGridSpec(
            num_scalar_prefetch=2,
            grid=(m // tt,),
            in_specs=[pl.BlockSpec((tt, d), lambda i, d1, d2: (i, 0)), any_spec],
            out_specs=any_spec,
            scratch_shapes=[pltpu.SemaphoreType.DMA(())]),
        out_shape=jax.ShapeDtypeStruct((n_rows, d), F32),
        input_output_aliases={3: 0},
        compiler_params=_params("arbitrary"),
        name="moe_dispatch",
    )(dest1, dest2, x, jnp.zeros((n_rows, d), F32))


def _expert_ffn_kernel(te_ref, nu_ref, x_ref, wg_ref, wu_ref, wd_ref, o_ref):
    i, j = pl.program_id(0), pl.program_id(1)
    used = i < nu_ref[0]

    @pl.when(used)
    def _():
        a = x_ref[...].astype(BF16)
        gate = jnp.dot(a, wg_ref[0], preferred_element_type=F32)
        up = jnp.dot(a, wu_ref[0], preferred_element_type=F32)
        hidden = (gate * jax.nn.sigmoid(gate) * up).astype(BF16)
        part = jnp.dot(hidden, wd_ref[0], preferred_element_type=F32)

        @pl.when(j == 0)
        def _():
            o_ref[...] = part

        @pl.when(j > 0)
        def _():
            o_ref[...] += part

    @pl.when(jnp.logical_not(used) & (j == 0))
    def _():
        o_ref[...] = jnp.zeros_like(o_ref)


def expert_ffn(xg, tile_expert, n_used, wg, wu, wd):
    r, d = xg.shape
    f = wg.shape[-1]
    tm, tf = MOE_ROW_TILE, _tile(f, 512)
    nf = f // tf

    def hidden_block(i, j, nu):
        return jnp.where(i < nu[0], j, nf - 1)

    return pl.pallas_call(
        _expert_ffn_kernel,
        grid_spec=pltpu.PrefetchScalarGridSpec(
            num_scalar_prefetch=2,
            grid=(r // tm, nf),
            in_specs=[pl.BlockSpec((tm, d), lambda i, j, te, nu: (i, 0)),
                      pl.BlockSpec((1, d, tf), lambda i, j, te, nu: (te[i], 0, hidden_block(i, j, nu))),
                      pl.BlockSpec((1, d, tf), lambda i, j, te, nu: (te[i], 0, hidden_block(i, j, nu))),
                      pl.BlockSpec((1, tf, d), lambda i, j, te, nu: (te[i], hidden_block(i, j, nu), 0))],
            out_specs=pl.BlockSpec((tm, d), lambda i, j, te, nu: (i, 0))),
        out_shape=jax.ShapeDtypeStruct((r, d), F32),
        compiler_params=_params("parallel", "arbitrary"),
        name="expert_ffn",
    )(tile_expert, n_used, xg, wg, wu, wd)


def _combine_kernel(d1_ref, d2_ref, h_ref, y_ref, w1_ref, w2_ref, g_ref, mod_ref, o_ref,
                    y1_ref, y2_ref, sem, *, gate_idx):
    tt = h_ref.shape[1]
    base = (pl.program_id(0) * pl.num_programs(1) + pl.program_id(1)) * tt

    def fetch(r, carry):
        _row_copies(y_ref, y1_ref, d1_ref[base + r], r, sem).start()
        _row_copies(y_ref, y2_ref, d2_ref[base + r], r, sem).start()
        return carry

    lax.fori_loop(0, tt, fetch, 0, unroll=8)
    _wait_rows(y_ref, y1_ref, tt, sem)
    _wait_rows(y_ref, y2_ref, tt, sem)
    y = w1_ref[...] * y1_ref[...] + w2_ref[...] * y2_ref[...]
    gate = mod_ref[0, gate_idx:gate_idx + 1, :]
    o_ref[0] = h_ref[0] + gate * (_rms(y) * g_ref[...])


def combine_residual(h, yg, dest1, dest2, w1, w2, g, mods, gate_idx, n_ctx):
    b, t, d = h.shape
    tt = _tile(n_ctx, 256)
    nt = t // tt
    col = pl.BlockSpec((tt, 1), lambda i, j, d1, d2: (i * nt + j, 0))
    n_ctx_tiles = n_ctx // tt
    return pl.pallas_call(
        functools.partial(_combine_kernel, gate_idx=gate_idx),
        grid_spec=pltpu.PrefetchScalarGridSpec(
            num_scalar_prefetch=2,
            grid=(b, nt),
            in_specs=[pl.BlockSpec((1, tt, d), lambda i, j, d1, d2: (i, j, 0)),
                      pl.BlockSpec(memory_space=pl.ANY),
                      col, col,
                      pl.BlockSpec((1, d), lambda i, j, d1, d2: (0, 0)),
                      pl.BlockSpec((1, 6, d), lambda i, j, d1, d2: (jnp.where(j < n_ctx_tiles, b, i), 0, 0))],
            out_specs=pl.BlockSpec((1, tt, d), lambda i, j, d1, d2: (i, j, 0)),
            scratch_shapes=[pltpu.VMEM((tt, d), F32), pltpu.VMEM((tt, d), F32),
                            pltpu.SemaphoreType.DMA(())]),
        out_shape=jax.ShapeDtypeStruct((b, t, d), F32),
        input_output_aliases={2: 0},
        compiler_params=_params("arbitrary", "arbitrary"),
        name="moe_combine",
    )(dest1, dest2, h, yg, w1.reshape(-1, 1), w2.reshape(-1, 1), g.reshape(1, d), mods)


def moe_ffn_residual(h, a, w_router, b_router, wg, wu, wd, g, mods, gate_idx, n_ctx):
    m, d = a.shape
    n_exp = w_router.shape[1]
    tm = MOE_ROW_TILE
    info, counts = route(a, w_router, b_router)
    i1, i2 = info[0].astype(jnp.int32), info[1].astype(jnp.int32)
    w1, w2 = info[2], info[3]
    r1, r2 = info[4].astype(jnp.int32), info[5].astype(jnp.int32)

    counts = counts[:, 0].astype(jnp.int32)
    padded = (counts + tm - 1) // tm * tm
    ends = jnp.cumsum(padded)
    starts = ends - padded
    n_tiles = (TOP_K * m) // tm + n_exp
    tile_expert = jnp.sum(jnp.arange(n_tiles)[:, None] >= (ends // tm)[None, :], axis=1)
    tile_expert = jnp.minimum(tile_expert, n_exp - 1).astype(jnp.int32)
    n_used = (ends[-1:] // tm).astype(jnp.int32)
    dest1 = starts[i1] + r1
    dest2 = starts[i2] + r2

    xg = dispatch(a, dest1, dest2, n_tiles * tm)
    yg = expert_ffn(xg, tile_expert, n_used, wg, wu, wd)
    return combine_residual(h, yg, dest1, dest2, w1, w2, g, mods, gate_idx, n_ctx)


def kernel(x, c, ctx, c_ctx, w_mod, b_mod, g_pre_mix, g_post_mix, g_pre_ffn, g_post_ffn, w_in,
           hgrn_lb, hgrn_norm_g, diff_lambda, diff_norm_g, qk_norm_q, qk_norm_k, w_branch, w_out,
           w_ff_gate, w_ff_up, w_ff_down, w_router, b_router, w_moe_gate, w_moe_up, w_moe_down):
    n_batch, seq, d = x.shape
    n_ctx = ctx.shape[1]
    depth = w_in.shape[0]
    t = n_ctx + seq
    m = n_batch * t

    hw = HG_HEADS * HEAD
    kv_sizes = (hw, hw, hw, DF_HEADS * 2 * DF_DIM, DF_HEADS * 2 * DF_DIM,
                GQ_KV_HEADS * HEAD, GQ_KV_HEADS * HEAD)
    q_sizes = (hw, hw, DF_HEADS * 2 * DF_DIM, GQ_HEADS * HEAD, FN_GROUPS * HEAD, N_BRANCH * d)
    offs = np.concatenate([[0], np.cumsum(kv_sizes + q_sizes)]).astype(int)
    (c_ff, c_fb, c_hv, c_dfk, c_dfv, c_gqk, c_gqv,
     c_hq, c_hg, c_dfq, c_gqq, c_fu, c_gl) = [int(o) for o in offs[:-1]]

    rope_df = rope_tables(n_ctx, seq, DF_DIM)
    rope_gq = rope_tables(n_ctx, seq, HEAD)
    dft_chan = dft_matrices(HEAD)
    dft_ctx = dft_matrices(n_ctx)
    dft_lat = dft_matrices(seq)

    lb_all = jnp.cumsum(jax.nn.softmax(hgrn_lb.astype(F32), axis=0), axis=0)
    lb_all = lb_all - lb_all[:1]

    n_cond = n_batch + 1
    pad = (-n_cond) % 16
    cond = jnp.concatenate([c, c_ctx[None, :], jnp.zeros((pad, d), F32)], axis=0)
    mods_all = modulation(cond, w_mod, b_mod)[:, :n_cond].reshape(depth, n_cond, 6, d)

    h = jnp.concatenate([ctx, x], axis=1)

    for l in range(depth):
        mods = mods_all[l]
        lam_init = 0.8 - 0.6 * math.exp(-0.3 * l)
        lv = diff_lambda[l].astype(F32)
        lam = jnp.exp(jnp.sum(lv[0] * lv[1])) - jnp.exp(jnp.sum(lv[2] * lv[3])) + lam_init

        a = norm_mod(h, g_pre_mix[l], mods, 0, 1, n_ctx)
        p2 = matmul(a.reshape(m, d), w_in[l].astype(BF16), BF16, tm=1024, tn=1536)
        p = p2.reshape(n_batch, t, -1)

        y_hg = hgrn_scan(p, (c_hq, c_ff, c_fb, c_hv, c_hg), lb_all[l], hgrn_norm_g[l], n_ctx)

        df_q1, df_q2, df_k, gq_q, gq_k = qk_prep(p, (c_dfq, c_dfk, c_gqq, c_gqk), rope_df, rope_gq,
                                                 qk_norm_q[l], qk_norm_k[l])
        y_df = attention(df_q1, df_k, p, c_dfv, DF_HEADS, DF_HEADS, n_ctx, q2=df_q2, lam=lam,
                         g=diff_norm_g[l], post_scale=1.0 - lam_init)

        y_fn = fourier_mix(p, c_fu, n_ctx, dft_chan, dft_ctx, dft_lat)

        y_gq = attention(gq_q, gq_k, p, c_gqv, GQ_HEADS, GQ_KV_HEADS, n_ctx)

        merged = merge((y_hg, y_df, y_fn, y_gq), p2, c_gl, w_branch[l].astype(BF16))
        y = matmul(merged, w_out[l].astype(BF16), F32)
        h = residual(h, y, g_post_mix[l], mods, 2, n_ctx)

        j = l // 2
        if l % 2 == 0:
            f_in = norm_mod(h, g_pre_ffn[l], mods, 3, 4, n_ctx).reshape(m, d)
            hidden = swiglu_up(f_in, w_ff_gate[j].astype(BF16), w_ff_up[j].astype(BF16))
            y = matmul(hidden, w_ff_down[j].astype(BF16), F32, tm=1024, tn=512)
            h = residual(h, y, g_post_ffn[l], mods, 5, n_ctx)
        else:
            f_in = norm_mod(h, g_pre_ffn[l], mods, 3, 4, n_ctx, out_dtype=F32).reshape(m, d)
            h = moe_ffn_residual(h, f_in, w_router[j], b_router[j], w_moe_gate[j].astype(BF16),
                                 w_moe_up[j].astype(BF16), w_moe_down[j].astype(BF16),
                                 g_post_ffn[l], mods, 5, n_ctx)

    return h[:, n_ctx:]
```

```python
import functools
import math

import jax
import jax.numpy as jnp
import numpy as np
from jax import lax
from jax.experimental import pallas as pl
from jax.experimental.pallas import tpu as pltpu

F32 = jnp.float32
BF16 = jnp.bfloat16

EPS = 1e-6
GRID_W = 64
ROPE_THETA = 10000.0
HEAD = 128
HG_HEADS = 4
GLA_CHUNK = 32
DF_HEADS = 4
DF_DIM = 64
FN_GROUPS = 4
GQ_HEADS = 4
GQ_KV_HEADS = 2
N_BRANCH = 4
BRANCH_W = 512
TOP_K = 2

V7X_VMEM_LIMIT_BYTES = 56 * 1024 * 1024
SCAN_BLOCK = 256


def _params(*sem):
    return pltpu.CompilerParams(dimension_semantics=sem, vmem_limit_bytes=V7X_VMEM_LIMIT_BYTES)


def _tile(n, pref, align=128):
    if n <= pref:
        return n
    t = pref - pref % align
    while n % t:
        t -= align
    return t


def _mm_kernel(a_ref, w_ref, o_ref):
    o_ref[...] = jnp.dot(a_ref[...], w_ref[...].astype(BF16),
                         preferred_element_type=F32).astype(o_ref.dtype)


def _mm_acc_kernel(a_ref, w_ref, o_ref, acc_ref):
    k = pl.program_id(2)

    @pl.when(k == 0)
    def _():
        acc_ref[...] = jnp.zeros_like(acc_ref)

    acc_ref[...] += jnp.dot(a_ref[...], w_ref[...].astype(BF16), preferred_element_type=F32)

    @pl.when(k == pl.num_programs(2) - 1)
    def _():
        o_ref[...] = acc_ref[...].astype(o_ref.dtype)


def matmul(a, w, layer, out_dtype, tm=1024, tn=1024, tk=None):
    m, k = a.shape
    n = w.shape[2]
    tm, tn = _tile(m, tm), _tile(n, tn)
    if tk is None or tk >= k:
        return pl.pallas_call(
            _mm_kernel,
            grid=(m // tm, n // tn),
            in_specs=[pl.BlockSpec((tm, k), lambda i, j: (i, 0)),
                      pl.BlockSpec((None, k, tn), lambda i, j: (layer, 0, j))],
            out_specs=pl.BlockSpec((tm, tn), lambda i, j: (i, j)),
            out_shape=jax.ShapeDtypeStruct((m, n), out_dtype),
            compiler_params=_params("parallel", "parallel"),
            name="matmul",
        )(a, w)
    tk = _tile(k, tk)
    return pl.pallas_call(
        _mm_acc_kernel,
        grid=(m // tm, n // tn, k // tk),
        in_specs=[pl.BlockSpec((tm, tk), lambda i, j, q: (i, q)),
                  pl.BlockSpec((None, tk, tn), lambda i, j, q: (layer, q, j))],
        out_specs=pl.BlockSpec((tm, tn), lambda i, j, q: (i, j)),
        out_shape=jax.ShapeDtypeStruct((m, n), out_dtype),
        scratch_shapes=[pltpu.VMEM((tm, tn), F32)],
        compiler_params=_params("parallel", "parallel", "arbitrary"),
        name="matmul_acc",
    )(a, w)


def _mod_kernel(s_ref, w_ref, b_ref, o_ref):
    s = s_ref[...]
    s = (s * jax.nn.sigmoid(s)).astype(BF16)
    o_ref[0] = jnp.dot(s, w_ref[0].astype(BF16), preferred_element_type=F32) + b_ref[0]


def modulation(cond, w_mod, b_mod):
    depth, d, width = w_mod.shape
    r = cond.shape[0]
    tn = _tile(width, 1024)
    return pl.pallas_call(
        _mod_kernel,
        grid=(depth, width // tn),
        in_specs=[pl.BlockSpec((r, d), lambda l, j: (0, 0)),
                  pl.BlockSpec((1, d, tn), lambda l, j: (l, 0, j)),
                  pl.BlockSpec((1, 1, tn), lambda l, j: (l, 0, j))],
        out_specs=pl.BlockSpec((1, r, tn), lambda l, j: (l, 0, j)),
        out_shape=jax.ShapeDtypeStruct((depth, r, width), F32),
        compiler_params=_params("parallel", "parallel"),
        name="modulation",
    )(cond, w_mod, b_mod.reshape(depth, 1, width))


def _rms(x):
    return x * lax.rsqrt(jnp.mean(x * x, axis=-1, keepdims=True) + EPS)


def _norm_mod_kernel(h_ref, g_ref, mod_ref, o_ref, *, shift_idx, scale_idx):
    y = _rms(h_ref[0]) * g_ref[...]
    shift = mod_ref[0, shift_idx:shift_idx + 1, :]
    scale = mod_ref[0, scale_idx:scale_idx + 1, :]
    o_ref[0] = (y * (1.0 + scale) + shift).astype(o_ref.dtype)


def _mod_index(n_ctx_tiles, n_batch):
    return lambda b, t: (jnp.where(t < n_ctx_tiles, n_batch, b), 0, 0)


def norm_mod(h, g, mods, shift_idx, scale_idx, n_ctx, out_dtype=BF16):
    b, t, d = h.shape
    tt = _tile(n_ctx, 256)
    return pl.pallas_call(
        functools.partial(_norm_mod_kernel, shift_idx=shift_idx, scale_idx=scale_idx),
        grid=(b, t // tt),
        in_specs=[pl.BlockSpec((1, tt, d), lambda i, j: (i, j, 0)),
                  pl.BlockSpec((1, d), lambda i, j: (0, 0)),
                  pl.BlockSpec((1, 6, d), _mod_index(n_ctx // tt, b))],
        out_specs=pl.BlockSpec((1, tt, d), lambda i, j: (i, j, 0)),
        out_shape=jax.ShapeDtypeStruct((b, t, d), out_dtype),
        compiler_params=_params("parallel", "parallel"),
        name="norm_mod",
    )(h, g.reshape(1, d), mods)


def _resid_kernel(h_ref, y_ref, g_ref, mod_ref, o_ref, *, gate_idx):
    gate = mod_ref[0, gate_idx:gate_idx + 1, :]
    o_ref[0] = h_ref[0] + gate * (_rms(y_ref[0]) * g_ref[...])


def residual(h, y, g, mods, gate_idx, n_ctx):
    b, t, d = h.shape
    tt = _tile(n_ctx, 256)
    return pl.pallas_call(
        functools.partial(_resid_kernel, gate_idx=gate_idx),
        grid=(b, t // tt),
        in_specs=[pl.BlockSpec((1, tt, d), lambda i, j: (i, j, 0)),
                  pl.BlockSpec((1, tt, d), lambda i, j: (i, j, 0)),
                  pl.BlockSpec((1, d), lambda i, j: (0, 0)),
                  pl.BlockSpec((1, 6, d), _mod_index(n_ctx // tt, b))],
        out_specs=pl.BlockSpec((1, tt, d), lambda i, j: (i, j, 0)),
        out_shape=jax.ShapeDtypeStruct((b, t, d), F32),
        input_output_aliases={0: 0},
        compiler_params=_params("parallel", "parallel"),
        name="residual",
    )(h, y.reshape(b, t, d), g.reshape(1, d), mods)


def rope_tables(n_ctx, seq, dim):
    quarter = dim // 4
    inv = ROPE_THETA ** (-jnp.arange(quarter, dtype=F32) / quarter)
    pos = jnp.arange(seq)
    ar = (pos // GRID_W).astype(F32)[:, None] * inv
    ac = (pos % GRID_W).astype(F32)[:, None] * inv
    ang = jnp.concatenate([ar, ar, ac, ac], axis=-1)
    ang = jnp.tile(ang, (1, HEAD // dim))
    ang = jnp.concatenate([jnp.zeros((n_ctx, HEAD), F32), ang], axis=0)
    first = ((jnp.arange(HEAD) % dim) % (dim // 2)) < quarter
    cos, sin = jnp.cos(ang), jnp.sin(ang)
    return cos, jnp.where(first, -sin, 0.0), jnp.where(first, 0.0, sin)


def _rope(x, cos_ref, sa_ref, sb_ref, quarter):
    return (x * cos_ref[...] + pltpu.roll(x, HEAD - quarter, 1) * sa_ref[...]
            + pltpu.roll(x, quarter, 1) * sb_ref[...])


def _qk_prep_kernel(dfq_ref, dfk_ref, gqq_ref, gqk_ref, dcos, dsa, dsb, gcos, gsa, gsb, gq_ref, gk_ref,
                    q1_ref, q2_ref, dk_ref, gqo_ref, gko_ref):
    lane = lax.broadcasted_iota(jnp.int32, dcos.shape, 1)
    for h in range(DF_HEADS):
        sl = slice(h * HEAD, (h + 1) * HEAD)
        y = _rope(dfq_ref[0, :, sl].astype(F32), dcos, dsa, dsb, DF_DIM // 4) * DF_DIM ** -0.5
        q1_ref[0, :, sl] = jnp.where(lane < DF_DIM, y, 0.0).astype(BF16)
        q2_ref[0, :, sl] = jnp.where(lane >= DF_DIM, y, 0.0).astype(BF16)
        dk_ref[0, :, sl] = _rope(dfk_ref[0, :, sl].astype(F32), dcos, dsa, dsb,
                                 DF_DIM // 4).astype(BF16)
    for h in range(GQ_HEADS):
        sl = slice(h * HEAD, (h + 1) * HEAD)
        x = _rms(gqq_ref[0, :, sl].astype(F32)) * gq_ref[...]
        gqo_ref[0, :, sl] = (_rope(x, gcos, gsa, gsb, HEAD // 4) * HEAD ** -0.5).astype(BF16)
    for h in range(GQ_KV_HEADS):
        sl = slice(h * HEAD, (h + 1) * HEAD)
        x = _rms(gqk_ref[0, :, sl].astype(F32)) * gk_ref[...]
        gko_ref[0, :, sl] = _rope(x, gcos, gsa, gsb, HEAD // 4).astype(BF16)


def qk_prep(p, cols, rope_df, rope_gq, g_q, g_k):
    b, t, _ = p.shape
    tt = _tile(t, 1088, 16)
    widths = (DF_HEADS * HEAD, DF_HEADS * HEAD, GQ_HEADS * HEAD, GQ_KV_HEADS * HEAD)
    assert all(c % w == 0 for c, w in zip(cols, widths))

    def col_spec(c0, w):
        return pl.BlockSpec((1, tt, w), lambda i, j: (i, j, c0 // w))

    def out_spec(w):
        return pl.BlockSpec((1, tt, w), lambda i, j: (i, j, 0))

    tab = pl.BlockSpec((tt, HEAD), lambda i, j: (j, 0))
    vec = pl.BlockSpec((1, HEAD), lambda i, j: (0, 0))
    out_w = (widths[0], widths[0], widths[1], widths[2], widths[3])
    return pl.pallas_call(
        _qk_prep_kernel,
        grid=(b, t // tt),
        in_specs=[col_spec(c, w) for c, w in zip(cols, widths)] + [tab] * 6 + [vec, vec],
        out_specs=[out_spec(w) for w in out_w],
        out_shape=[jax.ShapeDtypeStruct((b, t, w), BF16) for w in out_w],
        compiler_params=_params("parallel", "parallel"),
        name="qk_prep",
    )(p, p, p, p, *rope_df, *rope_gq, g_q.reshape(1, HEAD), g_k.reshape(1, HEAD))


ATTN_Q_TILE = 256
ATTN_K_CHUNK = 512


def _attend(q, k_ref, vx_ref, nk):
    kc = ATTN_K_CHUNK
    m = acc = None
    for c0 in range(0, nk, kc):
        c1 = min(c0 + kc, nk)
        s = lax.dot_general(q, k_ref[0, c0:c1, :], (((1,), (1,)), ((), ())),
                            preferred_element_type=F32)
        m_c = jnp.max(s, axis=-1, keepdims=True)
        m_new = m_c if m is None else jnp.maximum(m, m_c)
        pv = jnp.dot(jnp.exp((s - m_new).astype(BF16)), vx_ref[c0:c1, :],
                     preferred_element_type=F32)
        acc = pv if acc is None else acc * jnp.exp(m - m_new) + pv
        m = m_new
    return acc[:, :HEAD] / acc[:, HEAD:]


def _attn_kernel(*refs, diff, n_ctx, tq, post_scale):
    if diff:
        lam_ref, qa_ref, qb_ref, k_ref, v_ref, g_ref, o_ref, vx_ref = refs
    else:
        qa_ref, k_ref, v_ref, o_ref, vx_ref = refs
    n_keys = k_ref.shape[1]

    def rows(r0, n, nk):
        if diff:
            qa, qb = qa_ref[0, pl.ds(r0, n), :], qb_ref[0, pl.ds(r0, n), :]
        else:
            pair = qa_ref[0, pl.ds(r0, n), :]
            qa, qb = pair[:, :HEAD], pair[:, HEAD:]
        oa = _attend(qa, k_ref, vx_ref, nk)
        ob = _attend(qb, k_ref, vx_ref, nk)
        if diff:
            o = oa - lam_ref[...] * ob
            o = _rms(o) * g_ref[...] * post_scale
        else:
            o = jnp.concatenate([oa, ob], axis=1)
        o_ref[0, pl.ds(r0, n), :] = o.astype(o_ref.dtype)

    j = pl.program_id(2)

    @pl.when(j == 0)
    def _():
        vx_ref[:, :HEAD] = v_ref[0]
        vx_ref[:, HEAD:] = jnp.ones((n_keys, HEAD), BF16)
        rows(0, n_ctx, n_ctx)

    rows(pl.multiple_of(n_ctx + j * tq, math.gcd(n_ctx, tq)), tq, n_keys)


def attention(q, k, v, v_col0, n_groups, n_ctx, q2=None, lam=None, g=None, post_scale=1.0):
    b, t, _ = k.shape
    tq = _tile(t - n_ctx, ATTN_Q_TILE, 16)
    vc = v_col0 // HEAD
    diff = q2 is not None
    qw = HEAD if diff else 2 * HEAD
    qspec = pl.BlockSpec((1, t, qw), lambda i, h, j: (i, 0, h))
    kspec = pl.BlockSpec((1, t, HEAD), lambda i, h, j: (i, 0, h))
    vspec = pl.BlockSpec((1, t, HEAD), lambda i, h, j: (i, 0, vc + h))
    if diff:
        one = pl.BlockSpec((1, 1), lambda i, h, j: (0, 0))
        in_specs = [one, qspec, qspec, kspec, vspec, pl.BlockSpec((1, HEAD), lambda i, h, j: (0, 0))]
        args = (lam.reshape(1, 1).astype(F32), q, q2, k, v, g.reshape(1, HEAD))
    else:
        in_specs = [qspec, kspec, vspec]
        args = (q, k, v)
    return pl.pallas_call(
        functools.partial(_attn_kernel, diff=diff, n_ctx=n_ctx, tq=tq, post_scale=post_scale),
        grid=(b, n_groups, (t - n_ctx) // tq),
        in_specs=in_specs,
        out_specs=qspec,
        out_shape=jax.ShapeDtypeStruct((b, t, n_groups * qw), BF16),
        scratch_shapes=[pltpu.VMEM((t, 2 * HEAD), BF16)],
        compiler_params=_params("parallel", "parallel", "arbitrary"),
        name="diff_attention" if diff else "gq_attention",
    )(*args)


def _chunk_prefix(x, row_in_chunk, c):
    s = 1
    while s < c:
        x = x + jnp.where(row_in_chunk >= s, pltpu.roll(x, s, 0), 0.0)
        s *= 2
    return x


def _scan_kernel(q_ref, zf_ref, zb_ref, v_ref, hg_ref, lb_ref, g_ref, o_ref,
                 of_ref, ob_ref, sf_ref, sb_ref, *, n_ctx_blocks):
    rows = SCAN_BLOCK
    c = GLA_CHUNK
    n_chunks = rows // c
    n_blocks = q_ref.shape[1] // rows
    half = c // 2

    r_i = lax.broadcasted_iota(jnp.int32, (rows, rows), 0)
    s_i = lax.broadcasted_iota(jnp.int32, (rows, rows), 1)
    shift = c.bit_length() - 1
    same = (r_i >> shift) == (s_i >> shift)
    mask_f = same & (s_i <= r_i)
    mask_b = same & (s_i >= r_i)
    row_in_chunk = lax.broadcasted_iota(jnp.int32, (rows, HEAD), 0) & (c - 1)

    def chunk_row(x, row):
        return x.reshape(n_chunks, c, HEAD)[:, row:row + 1, :]

    def spread(x3):
        return jnp.broadcast_to(x3, (n_chunks, c, HEAD)).reshape(rows, HEAD)

    def block(blk, z_ref, lb, s_ref, out_ref, mask, reverse):
        r0 = pl.multiple_of(blk * rows, rows)
        z = z_ref[0, pl.ds(r0, rows), :].astype(F32)
        q = q_ref[0, pl.ds(r0, rows), :].astype(F32)
        v = v_ref[0, pl.ds(r0, rows), :]
        kk = (1.0 - lb) * jax.nn.sigmoid(-z)
        lf = jnp.log(1.0 - kk)
        pre = _chunk_prefix(lf, row_in_chunk, c)
        end3 = chunk_row(pre, c - 1)
        if reverse:
            cum = spread(end3) - pre + lf
            mid3 = chunk_row(cum, half)
        else:
            cum = pre
            mid3 = chunk_row(cum, half - 1)
        mid, end = spread(mid3), spread(end3)
        qa = (q * jnp.exp(cum - mid)).astype(BF16)
        ka = (kk * jnp.exp(mid - cum)).astype(BF16)
        att = lax.dot_general(qa, ka, (((1,), (1,)), ((), ())), preferred_element_type=F32)
        att = jnp.where(mask, att, 0.0).astype(BF16)
        o_intra = jnp.dot(att, v, preferred_element_type=F32)
        q_dec = (q * jnp.exp(cum)).astype(BF16)
        k_dec = (kk * jnp.exp(end - cum)).astype(BF16)
        dec = jnp.exp(end3)
        st = s_ref[...]
        outs = [None] * n_chunks
        order = range(n_chunks - 1, -1, -1) if reverse else range(n_chunks)
        for n in order:
            sl = slice(n * c, (n + 1) * c)
            outs[n] = lax.dot_general(q_dec[sl], st.astype(BF16), (((1,), (1,)), ((), ())),
                                      preferred_element_type=F32)
            upd = lax.dot_general(v[sl], k_dec[sl], (((0,), (0,)), ((), ())),
                                  preferred_element_type=F32)
            st = st * dec[n] + upd
        s_ref[...] = st
        out_ref[pl.ds(r0, rows), :] = o_intra + jnp.concatenate(outs, axis=0)

    sf_ref[...] = jnp.zeros_like(sf_ref)
    sb_ref[...] = jnp.zeros_like(sb_ref)
    lb_f = lb_ref[0:1, :]
    lb_b = lb_ref[1:2, :]

    def step(i, carry):
        block(i, zf_ref, lb_f, sf_ref, of_ref, mask_f, False)
        j = jnp.where(i < n_ctx_blocks, n_ctx_blocks - 1 - i, n_blocks - 1 - i + n_ctx_blocks)
        block(j, zb_ref, lb_b, sb_ref, ob_ref, mask_b, True)
        return carry

    lax.fori_loop(0, n_blocks, step, 0)

    def finish(i, carry):
        r0 = pl.multiple_of(i * rows, rows)
        o = of_ref[pl.ds(r0, rows), :] + ob_ref[pl.ds(r0, rows), :]
        gate = hg_ref[0, pl.ds(r0, rows), :].astype(F32)
        y = _rms(o) * g_ref[...] * (gate * jax.nn.sigmoid(gate))
        o_ref[0, pl.ds(r0, rows), :] = y.astype(o_ref.dtype)
        return carry

    lax.fori_loop(0, n_blocks, finish, 0)


def hgrn_scan(p, cols, lb, g, n_ctx):
    b, t, _ = p.shape
    assert t % SCAN_BLOCK == 0 and n_ctx % SCAN_BLOCK == 0

    def col(c0):
        return pl.BlockSpec((1, t, HEAD), lambda i, h: (i, 0, c0 // HEAD + h))

    return pl.pallas_call(
        functools.partial(_scan_kernel, n_ctx_blocks=n_ctx // SCAN_BLOCK),
        grid=(b, HG_HEADS),
        in_specs=[col(c0) for c0 in cols] + [pl.BlockSpec((2, HEAD), lambda i, h: (0, h)),
                                             pl.BlockSpec((1, HEAD), lambda i, h: (0, 0))],
        out_specs=pl.BlockSpec((1, t, HEAD), lambda i, h: (i, 0, h)),
        out_shape=jax.ShapeDtypeStruct((b, t, HG_HEADS * HEAD), BF16),
        scratch_shapes=[pltpu.VMEM((t, HEAD), F32), pltpu.VMEM((t, HEAD), F32),
                        pltpu.VMEM((HEAD, HEAD), F32), pltpu.VMEM((HEAD, HEAD), F32)],
        compiler_params=_params("parallel", "parallel"),
        name="hgrn_scan",
    )(p, p, p, p, p, lb, g.reshape(1, HEAD))


def dft_matrices(n):
    idx = jnp.arange(n, dtype=jnp.int32)
    ang = ((idx[:, None] * idx[None, :]) % n).astype(F32) * (2.0 * math.pi / n)
    s = 1.0 / math.sqrt(n)
    return (jnp.cos(ang) * s).astype(BF16), (jnp.sin(ang) * s).astype(BF16)


def _fn_channel_kernel(u_ref, cc_ref, sc_ref, a_ref, b_ref):
    for grp in range(FN_GROUPS):
        sl = slice(grp * HEAD, (grp + 1) * HEAD)
        u = u_ref[0, :, sl]
        a_ref[0, :, sl] = jnp.dot(u, cc_ref[...], preferred_element_type=F32).astype(BF16)
        b_ref[0, :, sl] = jnp.dot(u, sc_ref[...], preferred_element_type=F32).astype(BF16)


def _fn_position_kernel(a_ref, b_ref, cl_ref, sl_ref, cx_ref, sx_ref, o_ref, *, n_ctx):
    i = pl.program_id(1)

    @pl.when(i == 0)
    def _():
        a = a_ref[0, :n_ctx, :]
        b = b_ref[0, :n_ctx, :]
        o_ref[0] = (jnp.dot(cx_ref[...], a, preferred_element_type=F32)
                    - jnp.dot(sx_ref[...], b, preferred_element_type=F32)).astype(o_ref.dtype)

    @pl.when(i > 0)
    def _():
        a = a_ref[0, n_ctx:, :]
        b = b_ref[0, n_ctx:, :]
        o_ref[0] = (jnp.dot(cl_ref[...], a, preferred_element_type=F32)
                    - jnp.dot(sl_ref[...], b, preferred_element_type=F32)).astype(o_ref.dtype)


def fourier_mix(p, col0, n_ctx, dft_chan, dft_ctx, dft_lat):
    b, t, _ = p.shape
    w = FN_GROUPS * HEAD
    tt = _tile(t, 1088, 16)
    assert col0 % w == 0
    blk = pl.BlockSpec((1, tt, w), lambda i, j: (i, j, 0))
    mat = pl.BlockSpec((HEAD, HEAD), lambda i, j: (0, 0))
    ua, ub = pl.pallas_call(
        _fn_channel_kernel,
        grid=(b, t // tt),
        in_specs=[pl.BlockSpec((1, tt, w), lambda i, j: (i, j, col0 // w)), mat, mat],
        out_specs=[blk, blk],
        out_shape=[jax.ShapeDtypeStruct((b, t, w), BF16)] * 2,
        compiler_params=_params("parallel", "parallel"),
        name="fourier_channels",
    )(p, *dft_chan)
    seq = t - n_ctx
    tm = n_ctx
    assert seq % tm == 0 and tm % 8 == 0
    full = pl.BlockSpec((1, t, w), lambda i, j: (i, 0, 0))
    lat = pl.BlockSpec((tm, seq), lambda i, j: (jnp.maximum(j - 1, 0), 0))
    ctx = pl.BlockSpec((n_ctx, n_ctx), lambda i, j: (0, 0))
    return pl.pallas_call(
        functools.partial(_fn_position_kernel, n_ctx=n_ctx),
        grid=(b, 1 + seq // tm),
        in_specs=[full, full, lat, lat, ctx, ctx],
        out_specs=pl.BlockSpec((1, tm, w), lambda i, j: (i, j, 0)),
        out_shape=jax.ShapeDtypeStruct((b, t, w), BF16),
        compiler_params=_params("parallel", "parallel"),
        name="fourier_positions",
    )(ua, ub, *dft_lat, *dft_ctx)


def _merge_kernel(*refs):
    ys, gs, ws, o_ref = refs[:N_BRANCH], refs[N_BRANCH:2 * N_BRANCH], refs[2 * N_BRANCH:-1], refs[-1]
    acc = None
    for y_ref, g_ref, w_ref in zip(ys, gs, ws):
        term = jax.nn.sigmoid(g_ref[...].astype(F32)) * jnp.dot(
            y_ref[...], w_ref[0].astype(BF16), preferred_element_type=F32)
        acc = term if acc is None else acc + term
    o_ref[...] = acc.astype(o_ref.dtype)


def merge(branches, p2, gate_col0, w_branch, layer):
    m = p2.shape[0]
    d = w_branch.shape[-1]
    tm, tn = _tile(m, 1024), _tile(d, 512)
    y_specs = [pl.BlockSpec((tm, BRANCH_W), lambda i, j: (i, 0))] * N_BRANCH
    g_specs = [pl.BlockSpec((tm, tn), functools.partial(
        lambda i, j, off: (i, off + j), off=(gate_col0 + k * d) // tn)) for k in range(N_BRANCH)]
    w_specs = [pl.BlockSpec((None, 1, BRANCH_W, tn),
                            functools.partial(lambda i, j, k: (layer, k, 0, j), k=k))
               for k in range(N_BRANCH)]
    return pl.pallas_call(
        _merge_kernel,
        grid=(m // tm, d // tn),
        in_specs=y_specs + g_specs + w_specs,
        out_specs=pl.BlockSpec((tm, tn), lambda i, j: (i, j)),
        out_shape=jax.ShapeDtypeStruct((m, d), BF16),
        compiler_params=_params("parallel", "parallel"),
        name="merge",
    )(*[y.reshape(m, BRANCH_W) for y in branches], *([p2] * N_BRANCH), *([w_branch] * N_BRANCH))


def _swiglu_kernel(a_ref, wg_ref, wu_ref, o_ref):
    a = a_ref[...]
    gate = jnp.dot(a, wg_ref[...].astype(BF16), preferred_element_type=F32)
    up = jnp.dot(a, wu_ref[...].astype(BF16), preferred_element_type=F32)
    o_ref[...] = (gate * jax.nn.sigmoid(gate) * up).astype(BF16)


def swiglu_up(a, wg, wu, layer):
    m, d = a.shape
    f = wg.shape[2]
    tm, tn = _tile(m, 1024), _tile(f, 512)
    return pl.pallas_call(
        _swiglu_kernel,
        grid=(m // tm, f // tn),
        in_specs=[pl.BlockSpec((tm, d), lambda i, j: (i, 0)),
                  pl.BlockSpec((None, d, tn), lambda i, j: (layer, 0, j)),
                  pl.BlockSpec((None, d, tn), lambda i, j: (layer, 0, j))],
        out_specs=pl.BlockSpec((tm, tn), lambda i, j: (i, j)),
        out_shape=jax.ShapeDtypeStruct((m, f), BF16),
        compiler_params=_params("parallel", "parallel"),
        name="swiglu_up",
    )(a, wg, wu)


MOE_ROW_TILE = 512
MOE_INFO_ROWS = 8


def _router_kernel(a_ref, w_ref, b_ref, info_ref, cnt_ref, carry_ref):
    @pl.when(pl.program_id(0) == 0)
    def _():
        carry_ref[...] = jnp.zeros_like(carry_ref)

    logits = lax.dot_general(w_ref[...], a_ref[...].astype(BF16), (((1,), (1,)), ((), ())),
                             preferred_element_type=F32) + b_ref[...]
    n_exp, tm = logits.shape
    e = lax.broadcasted_iota(jnp.int32, logits.shape, 0).astype(F32)
    m1 = jnp.max(logits, axis=0, keepdims=True)
    i1 = jnp.min(jnp.where(logits == m1, e, float(n_exp)), axis=0, keepdims=True)
    rest = jnp.where(e == i1, -jnp.inf, logits)
    m2 = jnp.max(rest, axis=0, keepdims=True)
    i2 = jnp.min(jnp.where(rest == m2, e, float(n_exp)), axis=0, keepdims=True)
    x = jnp.exp(m2 - m1)
    w1 = 1.0 / (1.0 + x)
    w2 = x * w1

    chosen = jnp.where((e == i1) | (e == i2), 1.0, 0.0)
    s_i = lax.broadcasted_iota(jnp.int32, (tm, tm), 0)
    t_i = lax.broadcasted_iota(jnp.int32, (tm, tm), 1)
    earlier = (s_i < t_i).astype(BF16)
    rank = jnp.dot(chosen.astype(BF16), earlier, preferred_element_type=F32) + carry_ref[:, 0:1]
    r1 = jnp.sum(jnp.where(e == i1, rank, 0.0), axis=0, keepdims=True)
    r2 = jnp.sum(jnp.where(e == i2, rank, 0.0), axis=0, keepdims=True)
    carry_ref[...] = carry_ref[...] + jnp.sum(chosen, axis=1, keepdims=True)
    cnt_ref[...] = carry_ref[...]

    row = lax.broadcasted_iota(jnp.int32, (MOE_INFO_ROWS, tm), 0)
    info = jnp.zeros((MOE_INFO_ROWS, tm), F32)
    for k, val in enumerate((i1, i2, w1, w2, r1, r2)):
        info = jnp.where(row == k, val, info)
    info_ref[...] = info


def route(a, w_router, b_router):
    m, d = a.shape
    n_exp = w_router.shape[1]
    tm = _tile(m, 1024)
    return pl.pallas_call(
        _router_kernel,
        grid=(m // tm,),
        in_specs=[pl.BlockSpec((tm, d), lambda i: (i, 0)),
                  pl.BlockSpec((n_exp, d), lambda i: (0, 0)),
                  pl.BlockSpec((n_exp, 1), lambda i: (0, 0))],
        out_specs=[pl.BlockSpec((MOE_INFO_ROWS, tm), lambda i: (0, i)),
                   pl.BlockSpec((n_exp, HEAD), lambda i: (0, 0))],
        out_shape=[jax.ShapeDtypeStruct((MOE_INFO_ROWS, m), F32),
                   jax.ShapeDtypeStruct((n_exp, HEAD), F32)],
        scratch_shapes=[pltpu.VMEM((n_exp, HEAD), F32)],
        compiler_params=_params("arbitrary"),
        name="router",
    )(a, w_router.T.astype(BF16), b_router.reshape(n_exp, 1).astype(F32))


def _row_copies(src_ref, dst_ref, src_row, dst_row, sem):
    return pltpu.make_async_copy(src_ref.at[pl.ds(src_row, 1), :], dst_ref.at[pl.ds(dst_row, 1), :], sem)


def _wait_rows(src_ref, dst_ref, n_rows, sem):
    pltpu.make_async_copy(src_ref.at[pl.ds(0, n_rows), :], dst_ref.at[pl.ds(0, n_rows), :], sem).wait()


def _dispatch_kernel(d1_ref, d2_ref, x_ref, init_ref, xg_ref, sem):
    del init_ref
    tt = x_ref.shape[0]
    base = pl.program_id(0) * tt

    def send(r, carry):
        _row_copies(x_ref, xg_ref, r, d1_ref[base + r], sem).start()
        _row_copies(x_ref, xg_ref, r, d2_ref[base + r], sem).start()
        return carry

    lax.fori_loop(0, tt, send, 0, unroll=8)
    _wait_rows(x_ref, xg_ref, tt, sem)
    _wait_rows(x_ref, xg_ref, tt, sem)


def dispatch(x, dest1, dest2, n_rows):
    m, d = x.shape
    tt = _tile(m, 256)
    any_spec = pl.BlockSpec(memory_space=pl.ANY)
    return pl.pallas_call(
        _dispatch_kernel,
        grid_spec=pltpu.PrefetchScalarGridSpec(
            num_scalar_prefetch=2,
            grid=(m // tt,),
            in_specs=[pl.BlockSpec((tt, d), lambda i, d1, d2: (i, 0)), any_spec],
            out_specs=any_spec,
            scratch_shapes=[pltpu.SemaphoreType.DMA(())]),
        out_shape=jax.ShapeDtypeStruct((n_rows, d), F32),
        input_output_aliases={3: 0},
        compiler_params=_params("arbitrary"),
        name="moe_dispatch",
    )(dest1, dest2, x, jnp.zeros((n_rows, d), F32))


def _expert_ffn_kernel(te_ref, nu_ref, x_ref, wg_ref, wu_ref, wd_ref, o_ref):
    i, j = pl.program_id(0), pl.program_id(1)
    used = i < nu_ref[0]

    @pl.when(used)
    def _():
        a = x_ref[...].astype(BF16)
        gate = jnp.dot(a, wg_ref[0].astype(BF16), preferred_element_type=F32)
        up = jnp.dot(a, wu_ref[0].astype(BF16), preferred_element_type=F32)
        hidden = (gate * jax.nn.sigmoid(gate) * up).astype(BF16)
        part = jnp.dot(hidden, wd_ref[0].astype(BF16), preferred_element_type=F32)

        @pl.when(j == 0)
        def _():
            o_ref[...] = part

        @pl.when(j > 0)
        def _():
            o_ref[...] += part

    @pl.when(jnp.logical_not(used) & (j == 0))
    def _():
        o_ref[...] = jnp.zeros_like(o_ref)


def expert_ffn(xg, tile_expert, n_used, wg, wu, wd, layer):
    r, d = xg.shape
    f = wg.shape[-1]
    tm, tf = MOE_ROW_TILE, _tile(f, 512)
    nf = f // tf

    def hidden_block(i, j, nu):
        return jnp.where(i < nu[0], j, nf - 1)

    def up_index(i, j, te, nu):
        return (layer, te[i], 0, hidden_block(i, j, nu))

    return pl.pallas_call(
        _expert_ffn_kernel,
        grid_spec=pltpu.PrefetchScalarGridSpec(
            num_scalar_prefetch=2,
            grid=(r // tm, nf),
            in_specs=[pl.BlockSpec((tm, d), lambda i, j, te, nu: (i, 0)),
                      pl.BlockSpec((None, 1, d, tf), up_index),
                      pl.BlockSpec((None, 1, d, tf), up_index),
                      pl.BlockSpec((None, 1, tf, d),
                                   lambda i, j, te, nu: (layer, te[i], hidden_block(i, j, nu), 0))],
            out_specs=pl.BlockSpec((tm, d), lambda i, j, te, nu: (i, 0))),
        out_shape=jax.ShapeDtypeStruct((r, d), F32),
        compiler_params=_params("parallel", "arbitrary"),
        name="expert_ffn",
    )(tile_expert, n_used, xg, wg, wu, wd)


def _combine_kernel(d1_ref, d2_ref, h_ref, y_ref, w1_ref, w2_ref, g_ref, mod_ref, o_ref,
                    y1_ref, y2_ref, sem, *, gate_idx):
    tt = h_ref.shape[1]
    base = (pl.program_id(0) * pl.num_programs(1) + pl.program_id(1)) * tt

    def fetch(r, carry):
        _row_copies(y_ref, y1_ref, d1_ref[base + r], r, sem).start()
        _row_copies(y_ref, y2_ref, d2_ref[base + r], r, sem).start()
        return carry

    lax.fori_loop(0, tt, fetch, 0, unroll=8)
    _wait_rows(y_ref, y1_ref, tt, sem)
    _wait_rows(y_ref, y2_ref, tt, sem)
    y = w1_ref[...] * y1_ref[...] + w2_ref[...] * y2_ref[...]
    gate = mod_ref[0, gate_idx:gate_idx + 1, :]
    o_ref[0] = h_ref[0] + gate * (_rms(y) * g_ref[...])


def combine_residual(h, yg, dest1, dest2, w1, w2, g, mods, gate_idx, n_ctx):
    b, t, d = h.shape
    tt = _tile(n_ctx, 256)
    nt = t // tt
    col = pl.BlockSpec((tt, 1), lambda i, j, d1, d2: (i * nt + j, 0))
    n_ctx_tiles = n_ctx // tt
    return pl.pallas_call(
        functools.partial(_combine_kernel, gate_idx=gate_idx),
        grid_spec=pltpu.PrefetchScalarGridSpec(
            num_scalar_prefetch=2,
            grid=(b, nt),
            in_specs=[pl.BlockSpec((1, tt, d), lambda i, j, d1, d2: (i, j, 0)),
                      pl.BlockSpec(memory_space=pl.ANY),
                      col, col,
                      pl.BlockSpec((1, d), lambda i, j, d1, d2: (0, 0)),
                      pl.BlockSpec((1, 6, d), lambda i, j, d1, d2: (jnp.where(j < n_ctx_tiles, b, i), 0, 0))],
            out_specs=pl.BlockSpec((1, tt, d), lambda i, j, d1, d2: (i, j, 0)),
            scratch_shapes=[pltpu.VMEM((tt, d), F32), pltpu.VMEM((tt, d), F32),
                            pltpu.SemaphoreType.DMA(())]),
        out_shape=jax.ShapeDtypeStruct((b, t, d), F32),
        input_output_aliases={2: 0},
        compiler_params=_params("arbitrary", "arbitrary"),
        name="moe_combine",
    )(dest1, dest2, h, yg, w1.reshape(-1, 1), w2.reshape(-1, 1), g.reshape(1, d), mods)


def moe_ffn_residual(h, a, w_router, b_router, wg, wu, wd, layer, g, mods, gate_idx, n_ctx):
    m, d = a.shape
    n_exp = w_router.shape[1]
    tm = MOE_ROW_TILE
    info, counts = route(a, w_router, b_router)
    i1, i2 = info[0].astype(jnp.int32), info[1].astype(jnp.int32)
    w1, w2 = info[2], info[3]
    r1, r2 = info[4].astype(jnp.int32), info[5].astype(jnp.int32)

    counts = counts[:, 0].astype(jnp.int32)
    padded = (counts + tm - 1) // tm * tm
    ends = jnp.cumsum(padded)
    starts = ends - padded
    n_tiles = (TOP_K * m) // tm + n_exp
    tile_expert = jnp.sum(jnp.arange(n_tiles)[:, None] >= (ends // tm)[None, :], axis=1)
    tile_expert = jnp.minimum(tile_expert, n_exp - 1).astype(jnp.int32)
    n_used = (ends[-1:] // tm).astype(jnp.int32)
    dest1 = starts[i1] + r1
    dest2 = starts[i2] + r2

    xg = dispatch(a, dest1, dest2, n_tiles * tm)
    yg = expert_ffn(xg, tile_expert, n_used, wg, wu, wd, layer)
    return combine_residual(h, yg, dest1, dest2, w1, w2, g, mods, gate_idx, n_ctx)


def kernel(x, c, ctx, c_ctx, w_mod, b_mod, g_pre_mix, g_post_mix, g_pre_ffn, g_post_ffn, w_in,
           hgrn_lb, hgrn_norm_g, diff_lambda, diff_norm_g, qk_norm_q, qk_norm_k, w_branch, w_out,
           w_ff_gate, w_ff_up, w_ff_down, w_router, b_router, w_moe_gate, w_moe_up, w_moe_down):
    n_batch, seq, d = x.shape
    n_ctx = ctx.shape[1]
    depth = w_in.shape[0]
    t = n_ctx + seq
    m = n_batch * t

    hw = HG_HEADS * HEAD
    kv_sizes = (hw, hw, hw, DF_HEADS * 2 * DF_DIM, DF_HEADS * 2 * DF_DIM,
                GQ_KV_HEADS * HEAD, GQ_KV_HEADS * HEAD)
    q_sizes = (hw, hw, DF_HEADS * 2 * DF_DIM, GQ_HEADS * HEAD, FN_GROUPS * HEAD, N_BRANCH * d)
    offs = np.concatenate([[0], np.cumsum(kv_sizes + q_sizes)]).astype(int)
    (c_ff, c_fb, c_hv, c_dfk, c_dfv, c_gqk, c_gqv,
     c_hq, c_hg, c_dfq, c_gqq, c_fu, c_gl) = [int(o) for o in offs[:-1]]

    rope_df = rope_tables(n_ctx, seq, DF_DIM)
    rope_gq = rope_tables(n_ctx, seq, HEAD)
    dft_chan = dft_matrices(HEAD)
    dft_ctx = dft_matrices(n_ctx)
    dft_lat = dft_matrices(seq)

    lb_all = jnp.cumsum(jax.nn.softmax(hgrn_lb.astype(F32), axis=0), axis=0)
    lb_all = lb_all - lb_all[:1]

    n_cond = n_batch + 1
    pad = (-n_cond) % 16
    cond = jnp.concatenate([c, c_ctx[None, :], jnp.zeros((pad, d), F32)], axis=0)
    mods_all = modulation(cond, w_mod, b_mod)[:, :n_cond].reshape(depth, n_cond, 6, d)

    h = jnp.concatenate([ctx, x], axis=1)

    for l in range(depth):
        mods = mods_all[l]
        lam_init = 0.8 - 0.6 * math.exp(-0.3 * l)
        lv = diff_lambda[l].astype(F32)
        lam = jnp.exp(jnp.sum(lv[0] * lv[1])) - jnp.exp(jnp.sum(lv[2] * lv[3])) + lam_init

        a = norm_mod(h, g_pre_mix[l], mods, 0, 1, n_ctx)
        p2 = matmul(a.reshape(m, d), w_in, l, BF16, tm=1024, tn=768)
        p = p2.reshape(n_batch, t, -1)

        y_hg = hgrn_scan(p, (c_hq, c_ff, c_fb, c_hv, c_hg), lb_all[l], hgrn_norm_g[l], n_ctx)

        df_q1, df_q2, df_k, gq_q, gq_k = qk_prep(p, (c_dfq, c_dfk, c_gqq, c_gqk), rope_df, rope_gq,
                                                 qk_norm_q[l], qk_norm_k[l])
        y_df = attention(df_q1, df_k, p, c_dfv, DF_HEADS, n_ctx, q2=df_q2, lam=lam,
                         g=diff_norm_g[l], post_scale=1.0 - lam_init)

        y_fn = fourier_mix(p, c_fu, n_ctx, dft_chan, dft_ctx, dft_lat)

        assert GQ_HEADS == 2 * GQ_KV_HEADS
        y_gq = attention(gq_q, gq_k, p, c_gqv, GQ_KV_HEADS, n_ctx)

        merged = merge((y_hg, y_df, y_fn, y_gq), p2, c_gl, w_branch, l)
        y = matmul(merged, w_out, l, F32)
        h = residual(h, y, g_post_mix[l], mods, 2, n_ctx)

        j = l // 2
        if l % 2 == 0:
            f_in = norm_mod(h, g_pre_ffn[l], mods, 3, 4, n_ctx).reshape(m, d)
            hidden = swiglu_up(f_in, w_ff_gate, w_ff_up, j)
            y = matmul(hidden, w_ff_down, j, F32, tm=1024, tn=512, tk=w_ff_down.shape[1] // 2)
            h = residual(h, y, g_post_ffn[l], mods, 5, n_ctx)
        else:
            f_in = norm_mod(h, g_pre_ffn[l], mods, 3, 4, n_ctx, out_dtype=F32).reshape(m, d)
            h = moe_ffn_residual(h, f_in, w_router[j], b_router[j], w_moe_gate, w_moe_up,
                                 w_moe_down, j, g_post_ffn[l], mods, 5, n_ctx)

    return h[:, n_ctx:]
```

```python
import functools
import math

import jax
import jax.numpy as jnp
import numpy as np
from jax import lax
from jax.experimental import pallas as pl
from jax.experimental.pallas import tpu as pltpu

F32 = jnp.float32
BF16 = jnp.bfloat16

EPS = 1e-6
GRID_W = 64
ROPE_THETA = 10000.0
HEAD = 128
HG_HEADS = 4
GLA_CHUNK = 32
DF_HEADS = 4
DF_DIM = 64
FN_GROUPS = 4
GQ_HEADS = 4
GQ_KV_HEADS = 2
N_BRANCH = 4
BRANCH_W = 512
TOP_K = 2

V7X_VMEM_LIMIT_BYTES = 56 * 1024 * 1024
SCAN_BLOCK = 256
SCAN_HEADS_PER_STEP = 2


def _params(*sem):
    return pltpu.CompilerParams(dimension_semantics=sem, vmem_limit_bytes=V7X_VMEM_LIMIT_BYTES)


def _tile(n, pref, align=128):
    if n <= pref:
        return n
    t = pref - pref % align
    while n % t:
        t -= align
    return t


def _mm_kernel(a_ref, w_ref, o_ref):
    o_ref[...] = jnp.dot(a_ref[...], w_ref[...].astype(BF16),
                         preferred_element_type=F32).astype(o_ref.dtype)


def _mm_acc_kernel(a_ref, w_ref, o_ref, acc_ref):
    k = pl.program_id(2)

    @pl.when(k == 0)
    def _():
        acc_ref[...] = jnp.zeros_like(acc_ref)

    acc_ref[...] += jnp.dot(a_ref[...], w_ref[...].astype(BF16), preferred_element_type=F32)

    @pl.when(k == pl.num_programs(2) - 1)
    def _():
        o_ref[...] = acc_ref[...].astype(o_ref.dtype)


def matmul(a, w, layer, out_dtype, tm=1024, tn=1024, tk=None):
    m, k = a.shape
    n = w.shape[2]
    tm, tn = _tile(m, tm), _tile(n, tn)
    if tk is None or tk >= k:
        return pl.pallas_call(
            _mm_kernel,
            grid=(m // tm, n // tn),
            in_specs=[pl.BlockSpec((tm, k), lambda i, j: (i, 0)),
                      pl.BlockSpec((None, k, tn), lambda i, j: (layer, 0, j))],
            out_specs=pl.BlockSpec((tm, tn), lambda i, j: (i, j)),
            out_shape=jax.ShapeDtypeStruct((m, n), out_dtype),
            compiler_params=_params("parallel", "parallel"),
            name="matmul",
        )(a, w)
    tk = _tile(k, tk)
    return pl.pallas_call(
        _mm_acc_kernel,
        grid=(m // tm, n // tn, k // tk),
        in_specs=[pl.BlockSpec((tm, tk), lambda i, j, q: (i, q)),
                  pl.BlockSpec((None, tk, tn), lambda i, j, q: (layer, q, j))],
        out_specs=pl.BlockSpec((tm, tn), lambda i, j, q: (i, j)),
        out_shape=jax.ShapeDtypeStruct((m, n), out_dtype),
        scratch_shapes=[pltpu.VMEM((tm, tn), F32)],
        compiler_params=_params("parallel", "parallel", "arbitrary"),
        name="matmul_acc",
    )(a, w)


def _mod_kernel(s_ref, w_ref, b_ref, o_ref):
    s = s_ref[...]
    s = (s * jax.nn.sigmoid(s)).astype(BF16)
    o_ref[0] = jnp.dot(s, w_ref[0].astype(BF16), preferred_element_type=F32) + b_ref[0]


def modulation(cond, w_mod, b_mod):
    depth, d, width = w_mod.shape
    r = cond.shape[0]
    tn = _tile(width, 1024)
    return pl.pallas_call(
        _mod_kernel,
        grid=(depth, width // tn),
        in_specs=[pl.BlockSpec((r, d), lambda l, j: (0, 0)),
                  pl.BlockSpec((1, d, tn), lambda l, j: (l, 0, j)),
                  pl.BlockSpec((1, 1, tn), lambda l, j: (l, 0, j))],
        out_specs=pl.BlockSpec((1, r, tn), lambda l, j: (l, 0, j)),
        out_shape=jax.ShapeDtypeStruct((depth, r, width), F32),
        compiler_params=_params("parallel", "parallel"),
        name="modulation",
    )(cond, w_mod, b_mod.reshape(depth, 1, width))


def _rms(x):
    return x * lax.rsqrt(jnp.mean(x * x, axis=-1, keepdims=True) + EPS)


def _norm_mod_kernel(h_ref, g_ref, mod_ref, o_ref, *, shift_idx, scale_idx):
    y = _rms(h_ref[0]) * g_ref[...]
    shift = mod_ref[0, shift_idx:shift_idx + 1, :]
    scale = mod_ref[0, scale_idx:scale_idx + 1, :]
    o_ref[0] = (y * (1.0 + scale) + shift).astype(o_ref.dtype)


def _mod_index(n_ctx_tiles, n_batch):
    return lambda b, t: (jnp.where(t < n_ctx_tiles, n_batch, b), 0, 0)


def norm_mod(h, g, mods, shift_idx, scale_idx, n_ctx, out_dtype=BF16):
    b, t, d = h.shape
    tt = _tile(n_ctx, 256)
    return pl.pallas_call(
        functools.partial(_norm_mod_kernel, shift_idx=shift_idx, scale_idx=scale_idx),
        grid=(b, t // tt),
        in_specs=[pl.BlockSpec((1, tt, d), lambda i, j: (i, j, 0)),
                  pl.BlockSpec((1, d), lambda i, j: (0, 0)),
                  pl.BlockSpec((1, 6, d), _mod_index(n_ctx // tt, b))],
        out_specs=pl.BlockSpec((1, tt, d), lambda i, j: (i, j, 0)),
        out_shape=jax.ShapeDtypeStruct((b, t, d), out_dtype),
        compiler_params=_params("parallel", "parallel"),
        name="norm_mod",
    )(h, g.reshape(1, d), mods)


def _resid_kernel(h_ref, y_ref, g_ref, mod_ref, o_ref, *, gate_idx):
    gate = mod_ref[0, gate_idx:gate_idx + 1, :]
    o_ref[0] = h_ref[0] + gate * (_rms(y_ref[0]) * g_ref[...])


def residual(h, y, g, mods, gate_idx, n_ctx, latents_only=False):
    b, t, d = h.shape
    tt = _tile(n_ctx, 256)
    out_index, out_shape, in_place = _stream_out(b, t, d, tt, n_ctx, latents_only)
    return pl.pallas_call(
        functools.partial(_resid_kernel, gate_idx=gate_idx),
        grid=(b, t // tt),
        in_specs=[pl.BlockSpec((1, tt, d), lambda i, j: (i, j, 0)),
                  pl.BlockSpec((1, tt, d), lambda i, j: (i, j, 0)),
                  pl.BlockSpec((1, d), lambda i, j: (0, 0)),
                  pl.BlockSpec((1, 6, d), _mod_index(n_ctx // tt, b))],
        out_specs=pl.BlockSpec((1, tt, d), out_index),
        out_shape=out_shape,
        input_output_aliases={0: 0} if in_place else {},
        compiler_params=_params("parallel", "arbitrary"),
        name="residual",
    )(h, y.reshape(b, t, d), g.reshape(1, d), mods)


def _mm_resid_kernel(a_ref, w_ref, h_ref, g_ref, mod_ref, modc_ref, o_ref, *, gate_idx, n_ctx):
    tt = a_ref.shape[1]
    y = jnp.dot(a_ref[0], w_ref[...].astype(BF16), preferred_element_type=F32)
    row = pl.program_id(1) * tt + lax.broadcasted_iota(jnp.int32, (tt, 1), 0)
    gate = jnp.where(row < n_ctx, modc_ref[0, gate_idx:gate_idx + 1, :],
                     mod_ref[0, gate_idx:gate_idx + 1, :])
    o_ref[0] = h_ref[0] + gate * (_rms(y) * g_ref[...])


def matmul_residual(a, w, layer, h, g, mods, gate_idx, n_ctx):
    b, t, d = h.shape
    k = a.shape[-1]
    tt = _tile(t, 544, 16)
    row = pl.BlockSpec((1, tt, d), lambda i, j: (i, j, 0))
    return pl.pallas_call(
        functools.partial(_mm_resid_kernel, gate_idx=gate_idx, n_ctx=n_ctx),
        grid=(b, t // tt),
        in_specs=[pl.BlockSpec((1, tt, k), lambda i, j: (i, j, 0)),
                  pl.BlockSpec((None, k, d), lambda i, j: (layer, 0, 0)),
                  row,
                  pl.BlockSpec((1, d), lambda i, j: (0, 0)),
                  pl.BlockSpec((1, 6, d), lambda i, j: (i, 0, 0)),
                  pl.BlockSpec((1, 6, d), lambda i, j: (b, 0, 0))],
        out_specs=row,
        out_shape=jax.ShapeDtypeStruct((b, t, d), F32),
        input_output_aliases={2: 0},
        compiler_params=_params("parallel", "parallel"),
        name="matmul_residual",
    )(a, w, h, g.reshape(1, d), mods, mods)


def rope_tables(n_ctx, seq, dim):
    quarter = dim // 4
    inv = ROPE_THETA ** (-jnp.arange(quarter, dtype=F32) / quarter)
    pos = jnp.arange(seq)
    ar = (pos // GRID_W).astype(F32)[:, None] * inv
    ac = (pos % GRID_W).astype(F32)[:, None] * inv
    ang = jnp.concatenate([ar, ar, ac, ac], axis=-1)
    ang = jnp.tile(ang, (1, HEAD // dim))
    ang = jnp.concatenate([jnp.zeros((n_ctx, HEAD), F32), ang], axis=0)
    first = ((jnp.arange(HEAD) % dim) % (dim // 2)) < quarter
    cos, sin = jnp.cos(ang), jnp.sin(ang)
    return cos, jnp.where(first, -sin, 0.0), jnp.where(first, 0.0, sin)


def _rope(x, cos_ref, sa_ref, sb_ref, quarter):
    return (x * cos_ref[...] + pltpu.roll(x, HEAD - quarter, 1) * sa_ref[...]
            + pltpu.roll(x, quarter, 1) * sb_ref[...])


def _qk_prep_kernel(dfq_ref, dfk_ref, gqq_ref, gqk_ref, dcos, dsa, dsb, gcos, gsa, gsb, gq_ref, gk_ref,
                    q1_ref, q2_ref, dk_ref, gqo_ref, gko_ref):
    lane = lax.broadcasted_iota(jnp.int32, dcos.shape, 1)
    for h in range(DF_HEADS):
        sl = slice(h * HEAD, (h + 1) * HEAD)
        y = _rope(dfq_ref[0, :, sl].astype(F32), dcos, dsa, dsb, DF_DIM // 4) * DF_DIM ** -0.5
        q1_ref[0, :, sl] = jnp.where(lane < DF_DIM, y, 0.0).astype(BF16)
        q2_ref[0, :, sl] = jnp.where(lane >= DF_DIM, y, 0.0).astype(BF16)
        dk_ref[0, :, sl] = _rope(dfk_ref[0, :, sl].astype(F32), dcos, dsa, dsb,
                                 DF_DIM // 4).astype(BF16)
    for h in range(GQ_HEADS):
        sl = slice(h * HEAD, (h + 1) * HEAD)
        x = _rms(gqq_ref[0, :, sl].astype(F32)) * gq_ref[...]
        gqo_ref[0, :, sl] = (_rope(x, gcos, gsa, gsb, HEAD // 4) * HEAD ** -0.5).astype(BF16)
    for h in range(GQ_KV_HEADS):
        sl = slice(h * HEAD, (h + 1) * HEAD)
        x = _rms(gqk_ref[0, :, sl].astype(F32)) * gk_ref[...]
        gko_ref[0, :, sl] = _rope(x, gcos, gsa, gsb, HEAD // 4).astype(BF16)


def qk_prep(p, cols, rope_df, rope_gq, g_q, g_k):
    b, t, _ = p.shape
    tt = _tile(t, 1088, 16)
    widths = (DF_HEADS * HEAD, DF_HEADS * HEAD, GQ_HEADS * HEAD, GQ_KV_HEADS * HEAD)
    assert all(c % w == 0 for c, w in zip(cols, widths))

    def col_spec(c0, w):
        return pl.BlockSpec((1, tt, w), lambda i, j: (i, j, c0 // w))

    def out_spec(w):
        return pl.BlockSpec((1, tt, w), lambda i, j: (i, j, 0))

    tab = pl.BlockSpec((tt, HEAD), lambda i, j: (j, 0))
    vec = pl.BlockSpec((1, HEAD), lambda i, j: (0, 0))
    out_w = (widths[0], widths[0], widths[1], widths[2], widths[3])
    return pl.pallas_call(
        _qk_prep_kernel,
        grid=(b, t // tt),
        in_specs=[col_spec(c, w) for c, w in zip(cols, widths)] + [tab] * 6 + [vec, vec],
        out_specs=[out_spec(w) for w in out_w],
        out_shape=[jax.ShapeDtypeStruct((b, t, w), BF16) for w in out_w],
        compiler_params=_params("parallel", "parallel"),
        name="qk_prep",
    )(p, p, p, p, *rope_df, *rope_gq, g_q.reshape(1, HEAD), g_k.reshape(1, HEAD))


ATTN_Q_TILE = 256
ATTN_K_CHUNK = 512


def _attend(q, k_ref, vx_ref, nk):
    kc = ATTN_K_CHUNK
    m = acc = None
    for c0 in range(0, nk, kc):
        c1 = min(c0 + kc, nk)
        s = lax.dot_general(q, k_ref[0, c0:c1, :], (((1,), (1,)), ((), ())),
                            preferred_element_type=F32)
        m_c = jnp.max(s, axis=-1, keepdims=True)
        m_new = m_c if m is None else jnp.maximum(m, m_c)
        pv = jnp.dot(jnp.exp((s - m_new).astype(BF16)), vx_ref[c0:c1, :],
                     preferred_element_type=F32)
        acc = pv if acc is None else acc * jnp.exp(m - m_new) + pv
        m = m_new
    return acc[:, :HEAD] / acc[:, HEAD:]


def _attn_kernel(*refs, diff, n_ctx, tq, post_scale):
    if diff:
        lam_ref, qa_ref, qb_ref, k_ref, v_ref, g_ref, o_ref, vx_ref = refs
    else:
        qa_ref, k_ref, v_ref, o_ref, vx_ref = refs
    n_keys = k_ref.shape[1]

    def rows(r0, n, nk):
        if diff:
            qa, qb = qa_ref[0, pl.ds(r0, n), :], qb_ref[0, pl.ds(r0, n), :]
        else:
            pair = qa_ref[0, pl.ds(r0, n), :]
            qa, qb = pair[:, :HEAD], pair[:, HEAD:]
        oa = _attend(qa, k_ref, vx_ref, nk)
        ob = _attend(qb, k_ref, vx_ref, nk)
        if diff:
            o = oa - lam_ref[...] * ob
            o = _rms(o) * g_ref[...] * post_scale
        else:
            o = jnp.concatenate([oa, ob], axis=1)
        o_ref[0, pl.ds(r0, n), :] = o.astype(o_ref.dtype)

    j = pl.program_id(2)

    @pl.when(j == 0)
    def _():
        vx_ref[:, :HEAD] = v_ref[0]
        vx_ref[:, HEAD:] = jnp.ones((n_keys, HEAD), BF16)
        rows(0, n_ctx, n_ctx)

    rows(pl.multiple_of(n_ctx + j * tq, math.gcd(n_ctx, tq)), tq, n_keys)


def attention(q, k, v, v_col0, n_groups, n_ctx, q2=None, lam=None, g=None, post_scale=1.0):
    b, t, _ = k.shape
    tq = _tile(t - n_ctx, ATTN_Q_TILE, 16)
    vc = v_col0 // HEAD
    diff = q2 is not None
    qw = HEAD if diff else 2 * HEAD
    qspec = pl.BlockSpec((1, t, qw), lambda i, h, j: (i, 0, h))
    kspec = pl.BlockSpec((1, t, HEAD), lambda i, h, j: (i, 0, h))
    vspec = pl.BlockSpec((1, t, HEAD), lambda i, h, j: (i, 0, vc + h))
    if diff:
        one = pl.BlockSpec((1, 1), lambda i, h, j: (0, 0))
        in_specs = [one, qspec, qspec, kspec, vspec, pl.BlockSpec((1, HEAD), lambda i, h, j: (0, 0))]
        args = (lam.reshape(1, 1).astype(F32), q, q2, k, v, g.reshape(1, HEAD))
    else:
        in_specs = [qspec, kspec, vspec]
        args = (q, k, v)
    return pl.pallas_call(
        functools.partial(_attn_kernel, diff=diff, n_ctx=n_ctx, tq=tq, post_scale=post_scale),
        grid=(b, n_groups, (t - n_ctx) // tq),
        in_specs=in_specs,
        out_specs=qspec,
        out_shape=jax.ShapeDtypeStruct((b, t, n_groups * qw), BF16),
        scratch_shapes=[pltpu.VMEM((t, 2 * HEAD), BF16)],
        compiler_params=_params("parallel", "parallel", "arbitrary"),
        name="diff_attention" if diff else "gq_attention",
    )(*args)


def _chunk_prefix(x, row_in_chunk, c):
    s = 1
    while s < c:
        x = x + jnp.where(row_in_chunk >= s, pltpu.roll(x, s, 0), 0.0)
        s *= 2
    return x


def _scan_kernel(q_ref, zf_ref, zb_ref, v_ref, hg_ref, lb_ref, g_ref, o_ref,
                 of_ref, ob_ref, sf_ref, sb_ref, *, n_ctx_blocks):
    rows = SCAN_BLOCK
    c = GLA_CHUNK
    n_chunks = rows // c
    n_blocks = q_ref.shape[1] // rows
    half = c // 2

    r_i = lax.broadcasted_iota(jnp.int32, (rows, rows), 0)
    s_i = lax.broadcasted_iota(jnp.int32, (rows, rows), 1)
    shift = c.bit_length() - 1
    same = (r_i >> shift) == (s_i >> shift)
    mask_f = same & (s_i <= r_i)
    mask_b = same & (s_i >= r_i)
    row_in_chunk = lax.broadcasted_iota(jnp.int32, (rows, HEAD), 0) & (c - 1)

    def chunk_row(x, row):
        return x.reshape(n_chunks, c, HEAD)[:, row:row + 1, :]

    def spread(x3):
        return jnp.broadcast_to(x3, (n_chunks, c, HEAD)).reshape(rows, HEAD)

    def block(blk, hd, z_ref, lb, s_ref, out_ref, mask, reverse):
        r0 = pl.multiple_of(blk * rows, rows)
        z = z_ref[0, pl.ds(r0, rows), hd].astype(F32)
        q = q_ref[0, pl.ds(r0, rows), hd].astype(F32)
        v = v_ref[0, pl.ds(r0, rows), hd]
        kk = (1.0 - lb) * jax.nn.sigmoid(-z)
        lf = jnp.log(1.0 - kk)
        pre = _chunk_prefix(lf, row_in_chunk, c)
        end3 = chunk_row(pre, c - 1)
        if reverse:
            cum = spread(end3) - pre + lf
            mid3 = chunk_row(cum, half)
        else:
            cum = pre
            mid3 = chunk_row(cum, half - 1)
        mid, end = spread(mid3), spread(end3)
        qa = (q * jnp.exp(cum - mid)).astype(BF16)
        ka = (kk * jnp.exp(mid - cum)).astype(BF16)
        att = lax.dot_general(qa, ka, (((1,), (1,)), ((), ())), preferred_element_type=F32)
        att = jnp.where(mask, att, 0.0).astype(BF16)
        o_intra = jnp.dot(att, v, preferred_element_type=F32)
        q_dec = (q * jnp.exp(cum)).astype(BF16)
        k_dec = (kk * jnp.exp(end - cum)).astype(BF16)
        dec = jnp.exp(end3)
        st = s_ref[...]
        outs = [None] * n_chunks
        order = range(n_chunks - 1, -1, -1) if reverse else range(n_chunks)
        for n in order:
            sl = slice(n * c, (n + 1) * c)
            outs[n] = lax.dot_general(q_dec[sl], st.astype(BF16), (((1,), (1,)), ((), ())),
                                      preferred_element_type=F32)
            upd = lax.dot_general(v[sl], k_dec[sl], (((0,), (0,)), ((), ())),
                                  preferred_element_type=F32)
            st = st * dec[n] + upd
        s_ref[...] = st
        out_ref[pl.ds(r0, rows), hd] = o_intra + jnp.concatenate(outs, axis=0)

    sf_ref[...] = jnp.zeros_like(sf_ref)
    sb_ref[...] = jnp.zeros_like(sb_ref)
    heads = [slice(k * HEAD, (k + 1) * HEAD) for k in range(q_ref.shape[2] // HEAD)]

    def step(i, carry):
        j = jnp.where(i < n_ctx_blocks, n_ctx_blocks - 1 - i, n_blocks - 1 - i + n_ctx_blocks)
        for k, hd in enumerate(heads):
            block(i, hd, zf_ref, lb_ref[0:1, hd], sf_ref.at[k], of_ref, mask_f, False)
            block(j, hd, zb_ref, lb_ref[1:2, hd], sb_ref.at[k], ob_ref, mask_b, True)
        return carry

    lax.fori_loop(0, n_blocks, step, 0)

    def finish(i, carry):
        r0 = pl.multiple_of(i * rows, rows)
        for hd in heads:
            o = of_ref[pl.ds(r0, rows), hd] + ob_ref[pl.ds(r0, rows), hd]
            gate = hg_ref[0, pl.ds(r0, rows), hd].astype(F32)
            y = _rms(o) * g_ref[...] * (gate * jax.nn.sigmoid(gate))
            o_ref[0, pl.ds(r0, rows), hd] = y.astype(o_ref.dtype)
        return carry

    lax.fori_loop(0, n_blocks, finish, 0)


def hgrn_scan(p, cols, lb, g, n_ctx):
    b, t, _ = p.shape
    assert t % SCAN_BLOCK == 0 and n_ctx % SCAN_BLOCK == 0
    hp = SCAN_HEADS_PER_STEP
    w = hp * HEAD
    assert HG_HEADS % hp == 0 and all(c0 % w == 0 for c0 in cols)

    def col(c0):
        return pl.BlockSpec((1, t, w), lambda i, h: (i, 0, c0 // w + h))

    return pl.pallas_call(
        functools.partial(_scan_kernel, n_ctx_blocks=n_ctx // SCAN_BLOCK),
        grid=(b, HG_HEADS // hp),
        in_specs=[col(c0) for c0 in cols] + [pl.BlockSpec((2, w), lambda i, h: (0, h)),
                                             pl.BlockSpec((1, HEAD), lambda i, h: (0, 0))],
        out_specs=pl.BlockSpec((1, t, w), lambda i, h: (i, 0, h)),
        out_shape=jax.ShapeDtypeStruct((b, t, HG_HEADS * HEAD), BF16),
        scratch_shapes=[pltpu.VMEM((t, w), F32), pltpu.VMEM((t, w), F32),
                        pltpu.VMEM((hp, HEAD, HEAD), F32), pltpu.VMEM((hp, HEAD, HEAD), F32)],
        compiler_params=_params("parallel", "parallel"),
        name="hgrn_scan",
    )(p, p, p, p, p, lb, g.reshape(1, HEAD))


def dft_matrices(n):
    idx = jnp.arange(n, dtype=jnp.int32)
    ang = ((idx[:, None] * idx[None, :]) % n).astype(F32) * (2.0 * math.pi / n)
    s = 1.0 / math.sqrt(n)
    return (jnp.cos(ang) * s).astype(BF16), (jnp.sin(ang) * s).astype(BF16)


def _fn_channel_kernel(u_ref, cc_ref, sc_ref, a_ref, b_ref):
    for grp in range(FN_GROUPS):
        sl = slice(grp * HEAD, (grp + 1) * HEAD)
        u = u_ref[0, :, sl]
        a_ref[0, :, sl] = jnp.dot(u, cc_ref[...], preferred_element_type=F32).astype(BF16)
        b_ref[0, :, sl] = jnp.dot(u, sc_ref[...], preferred_element_type=F32).astype(BF16)


def _fn_position_kernel(a_ref, b_ref, cl_ref, sl_ref, cx_ref, sx_ref, o_ref, *, n_ctx):
    i = pl.program_id(1)

    @pl.when(i == 0)
    def _():
        a = a_ref[0, :n_ctx, :]
        b = b_ref[0, :n_ctx, :]
        o_ref[0] = (jnp.dot(cx_ref[...], a, preferred_element_type=F32)
                    - jnp.dot(sx_ref[...], b, preferred_element_type=F32)).astype(o_ref.dtype)

    @pl.when(i > 0)
    def _():
        a = a_ref[0, n_ctx:, :]
        b = b_ref[0, n_ctx:, :]
        o_ref[0] = (jnp.dot(cl_ref[...], a, preferred_element_type=F32)
                    - jnp.dot(sl_ref[...], b, preferred_element_type=F32)).astype(o_ref.dtype)


def fourier_mix(p, col0, n_ctx, dft_chan, dft_ctx, dft_lat):
    b, t, _ = p.shape
    w = FN_GROUPS * HEAD
    tt = _tile(t, 1088, 16)
    assert col0 % w == 0
    blk = pl.BlockSpec((1, tt, w), lambda i, j: (i, j, 0))
    mat = pl.BlockSpec((HEAD, HEAD), lambda i, j: (0, 0))
    ua, ub = pl.pallas_call(
        _fn_channel_kernel,
        grid=(b, t // tt),
        in_specs=[pl.BlockSpec((1, tt, w), lambda i, j: (i, j, col0 // w)), mat, mat],
        out_specs=[blk, blk],
        out_shape=[jax.ShapeDtypeStruct((b, t, w), BF16)] * 2,
        compiler_params=_params("parallel", "parallel"),
        name="fourier_channels",
    )(p, *dft_chan)
    seq = t - n_ctx
    tm = n_ctx
    assert seq % tm == 0 and tm % 8 == 0
    full = pl.BlockSpec((1, t, w), lambda i, j: (i, 0, 0))
    lat = pl.BlockSpec((tm, seq), lambda i, j: (jnp.maximum(j - 1, 0), 0))
    ctx = pl.BlockSpec((n_ctx, n_ctx), lambda i, j: (0, 0))
    return pl.pallas_call(
        functools.partial(_fn_position_kernel, n_ctx=n_ctx),
        grid=(b, 1 + seq // tm),
        in_specs=[full, full, lat, lat, ctx, ctx],
        out_specs=pl.BlockSpec((1, tm, w), lambda i, j: (i, j, 0)),
        out_shape=jax.ShapeDtypeStruct((b, t, w), BF16),
        compiler_params=_params("parallel", "parallel"),
        name="fourier_positions",
    )(ua, ub, *dft_lat, *dft_ctx)


def _merge_kernel(*refs):
    ys, gs, ws, o_ref = refs[:N_BRANCH], refs[N_BRANCH:2 * N_BRANCH], refs[2 * N_BRANCH:-1], refs[-1]
    acc = None
    for y_ref, g_ref, w_ref in zip(ys, gs, ws):
        term = jax.nn.sigmoid(g_ref[...].astype(F32)) * jnp.dot(
            y_ref[...], w_ref[0].astype(BF16), preferred_element_type=F32)
        acc = term if acc is None else acc + term
    o_ref[...] = acc.astype(o_ref.dtype)


def merge(branches, p2, gate_col0, w_branch, layer):
    m = p2.shape[0]
    d = w_branch.shape[-1]
    tm, tn = _tile(m, 1024), _tile(d, 512)
    y_specs = [pl.BlockSpec((tm, BRANCH_W), lambda i, j: (i, 0))] * N_BRANCH
    g_specs = [pl.BlockSpec((tm, tn), functools.partial(
        lambda i, j, off: (i, off + j), off=(gate_col0 + k * d) // tn)) for k in range(N_BRANCH)]
    w_specs = [pl.BlockSpec((None, 1, BRANCH_W, tn),
                            functools.partial(lambda i, j, k: (layer, k, 0, j), k=k))
               for k in range(N_BRANCH)]
    return pl.pallas_call(
        _merge_kernel,
        grid=(m // tm, d // tn),
        in_specs=y_specs + g_specs + w_specs,
        out_specs=pl.BlockSpec((tm, tn), lambda i, j: (i, j)),
        out_shape=jax.ShapeDtypeStruct((m, d), BF16),
        compiler_params=_params("parallel", "parallel"),
        name="merge",
    )(*[y.reshape(m, BRANCH_W) for y in branches], *([p2] * N_BRANCH), *([w_branch] * N_BRANCH))


def _swiglu_kernel(a_ref, wg_ref, wu_ref, o_ref):
    a = a_ref[...]
    gate = jnp.dot(a, wg_ref[...].astype(BF16), preferred_element_type=F32)
    up = jnp.dot(a, wu_ref[...].astype(BF16), preferred_element_type=F32)
    o_ref[...] = (gate * jax.nn.sigmoid(gate) * up).astype(BF16)


def swiglu_up(a, wg, wu, layer):
    m, d = a.shape
    f = wg.shape[2]
    tm, tn = _tile(m, 1024), _tile(f, 512)
    return pl.pallas_call(
        _swiglu_kernel,
        grid=(m // tm, f // tn),
        in_specs=[pl.BlockSpec((tm, d), lambda i, j: (i, 0)),
                  pl.BlockSpec((None, d, tn), lambda i, j: (layer, 0, j)),
                  pl.BlockSpec((None, d, tn), lambda i, j: (layer, 0, j))],
        out_specs=pl.BlockSpec((tm, tn), lambda i, j: (i, j)),
        out_shape=jax.ShapeDtypeStruct((m, f), BF16),
        compiler_params=_params("parallel", "parallel"),
        name="swiglu_up",
    )(a, wg, wu)


MOE_ROW_TILE = 512
MOE_INFO_ROWS = 8


def _router_kernel(a_ref, w_ref, b_ref, info_ref, cnt_ref, carry_ref):
    @pl.when(pl.program_id(0) == 0)
    def _():
        carry_ref[...] = jnp.zeros_like(carry_ref)

    logits = lax.dot_general(w_ref[...], a_ref[...].astype(BF16), (((1,), (1,)), ((), ())),
                             preferred_element_type=F32) + b_ref[...]
    n_exp, tm = logits.shape
    e = lax.broadcasted_iota(jnp.int32, logits.shape, 0).astype(F32)
    m1 = jnp.max(logits, axis=0, keepdims=True)
    i1 = jnp.min(jnp.where(logits == m1, e, float(n_exp)), axis=0, keepdims=True)
    rest = jnp.where(e == i1, -jnp.inf, logits)
    m2 = jnp.max(rest, axis=0, keepdims=True)
    i2 = jnp.min(jnp.where(rest == m2, e, float(n_exp)), axis=0, keepdims=True)
    x = jnp.exp(m2 - m1)
    w1 = 1.0 / (1.0 + x)
    w2 = x * w1

    chosen = jnp.where((e == i1) | (e == i2), 1.0, 0.0)
    s_i = lax.broadcasted_iota(jnp.int32, (tm, tm), 0)
    t_i = lax.broadcasted_iota(jnp.int32, (tm, tm), 1)
    earlier = (s_i < t_i).astype(BF16)
    rank = jnp.dot(chosen.astype(BF16), earlier, preferred_element_type=F32) + carry_ref[:, 0:1]
    r1 = jnp.sum(jnp.where(e == i1, rank, 0.0), axis=0, keepdims=True)
    r2 = jnp.sum(jnp.where(e == i2, rank, 0.0), axis=0, keepdims=True)
    carry_ref[...] = carry_ref[...] + jnp.sum(chosen, axis=1, keepdims=True)
    cnt_ref[...] = carry_ref[...]

    row = lax.broadcasted_iota(jnp.int32, (MOE_INFO_ROWS, tm), 0)
    info = jnp.zeros((MOE_INFO_ROWS, tm), F32)
    for k, val in enumerate((i1, i2, w1, w2, r1, r2)):
        info = jnp.where(row == k, val, info)
    info_ref[...] = info


def route(a, w_router, b_router):
    m, d = a.shape
    n_exp = w_router.shape[1]
    tm = _tile(m, 1024)
    return pl.pallas_call(
        _router_kernel,
        grid=(m // tm,),
        in_specs=[pl.BlockSpec((tm, d), lambda i: (i, 0)),
                  pl.BlockSpec((n_exp, d), lambda i: (0, 0)),
                  pl.BlockSpec((n_exp, 1), lambda i: (0, 0))],
        out_specs=[pl.BlockSpec((MOE_INFO_ROWS, tm), lambda i: (0, i)),
                   pl.BlockSpec((n_exp, HEAD), lambda i: (0, 0))],
        out_shape=[jax.ShapeDtypeStruct((MOE_INFO_ROWS, m), F32),
                   jax.ShapeDtypeStruct((n_exp, HEAD), F32)],
        scratch_shapes=[pltpu.VMEM((n_exp, HEAD), F32)],
        compiler_params=_params("arbitrary"),
        name="router",
    )(a, w_router.T.astype(BF16), b_router.reshape(n_exp, 1).astype(F32))


def _row_copies(src_ref, dst_ref, src_row, dst_row, sem):
    return pltpu.make_async_copy(src_ref.at[pl.ds(src_row, 1), :], dst_ref.at[pl.ds(dst_row, 1), :], sem)


def _wait_rows(src_ref, dst_ref, n_rows, sem):
    pltpu.make_async_copy(src_ref.at[pl.ds(0, n_rows), :], dst_ref.at[pl.ds(0, n_rows), :], sem).wait()


def _dispatch_kernel(d1_ref, d2_ref, x_ref, init_ref, xg_ref, sem):
    del init_ref
    tt = x_ref.shape[0]
    base = pl.program_id(0) * tt

    def send(r, carry):
        _row_copies(x_ref, xg_ref, r, d1_ref[base + r], sem).start()
        _row_copies(x_ref, xg_ref, r, d2_ref[base + r], sem).start()
        return carry

    lax.fori_loop(0, tt, send, 0, unroll=8)
    _wait_rows(x_ref, xg_ref, tt, sem)
    _wait_rows(x_ref, xg_ref, tt, sem)


def dispatch(x, dest1, dest2, n_rows):
    m, d = x.shape
    tt = _tile(m, 256)
    any_spec = pl.BlockSpec(memory_space=pl.ANY)
    return pl.pallas_call(
        _dispatch_kernel,
        grid_spec=pltpu.PrefetchScalarGridSpec(
            num_scalar_prefetch=2,
            grid=(m // tt,),
            in_specs=[pl.BlockSpec((tt, d), lambda i, d1, d2: (i, 0)), any_spec],
            out_specs=any_spec,
            scratch_shapes=[pltpu.SemaphoreType.DMA(())]),
        out_shape=jax.ShapeDtypeStruct((n_rows, d), F32),
        input_output_aliases={3: 0},
        compiler_params=_params("arbitrary"),
        name="moe_dispatch",
    )(dest1, dest2, x, jnp.zeros((n_rows, d), F32))


def _expert_ffn_kernel(te_ref, nu_ref, x_ref, wg_ref, wu_ref, wd_ref, o_ref):
    i, j = pl.program_id(0), pl.program_id(1)
    used = i < nu_ref[0]

    @pl.when(used)
    def _():
        a = x_ref[...].astype(BF16)
        gate = jnp.dot(a, wg_ref[0].astype(BF16), preferred_element_type=F32)
        up = jnp.dot(a, wu_ref[0].astype(BF16), preferred_element_type=F32)
        hidden = (gate * jax.nn.sigmoid(gate) * up).astype(BF16)
        part = jnp.dot(hidden, wd_ref[0].astype(BF16), preferred_element_type=F32)

        @pl.when(j == 0)
        def _():
            o_ref[...] = part

        @pl.when(j > 0)
        def _():
            o_ref[...] += part

    @pl.when(jnp.logical_not(used) & (j == 0))
    def _():
        o_ref[...] = jnp.zeros_like(o_ref)


def expert_ffn(xg, tile_expert, n_used, wg, wu, wd, layer):
    r, d = xg.shape
    f = wg.shape[-1]
    tm, tf = MOE_ROW_TILE, _tile(f, 1024)
    nf = f // tf

    def hidden_block(i, j, nu):
        return jnp.where(i < nu[0], j, nf - 1)

    def up_index(i, j, te, nu):
        return (layer, te[i], 0, hidden_block(i, j, nu))

    return pl.pallas_call(
        _expert_ffn_kernel,
        grid_spec=pltpu.PrefetchScalarGridSpec(
            num_scalar_prefetch=2,
            grid=(r // tm, nf),
            in_specs=[pl.BlockSpec((tm, d), lambda i, j, te, nu: (i, 0)),
                      pl.BlockSpec((None, 1, d, tf), up_index),
                      pl.BlockSpec((None, 1, d, tf), up_index),
                      pl.BlockSpec((None, 1, tf, d),
                                   lambda i, j, te, nu: (layer, te[i], hidden_block(i, j, nu), 0))],
            out_specs=pl.BlockSpec((tm, d), lambda i, j, te, nu: (i, 0))),
        out_shape=jax.ShapeDtypeStruct((r, d), F32),
        compiler_params=_params("parallel", "arbitrary"),
        name="expert_ffn",
    )(tile_expert, n_used, xg, wg, wu, wd)


def _combine_kernel(d1_ref, d2_ref, h_ref, y_ref, w1_ref, w2_ref, g_ref, mod_ref, o_ref,
                    y1_ref, y2_ref, sem, *, gate_idx):
    tt = h_ref.shape[1]
    base = (pl.program_id(0) * pl.num_programs(1) + pl.program_id(1)) * tt

    def fetch(r, carry):
        _row_copies(y_ref, y1_ref, d1_ref[base + r], r, sem).start()
        _row_copies(y_ref, y2_ref, d2_ref[base + r], r, sem).start()
        return carry

    lax.fori_loop(0, tt, fetch, 0, unroll=8)
    _wait_rows(y_ref, y1_ref, tt, sem)
    _wait_rows(y_ref, y2_ref, tt, sem)
    y = w1_ref[...] * y1_ref[...] + w2_ref[...] * y2_ref[...]
    gate = mod_ref[0, gate_idx:gate_idx + 1, :]
    o_ref[0] = h_ref[0] + gate * (_rms(y) * g_ref[...])


def _stream_out(b, t, d, tt, n_ctx, latents_only):
    if not latents_only:
        return (lambda i, j, *_: (i, j, 0)), jax.ShapeDtypeStruct((b, t, d), F32), True
    skip = n_ctx // tt
    return ((lambda i, j, *_: (i, jnp.maximum(j - skip, 0), 0)),
            jax.ShapeDtypeStruct((b, t - n_ctx, d), F32), False)


def combine_residual(h, yg, dest1, dest2, w1, w2, g, mods, gate_idx, n_ctx, latents_only=False):
    b, t, d = h.shape
    tt = _tile(n_ctx, 256)
    nt = t // tt
    col = pl.BlockSpec((tt, 1), lambda i, j, d1, d2: (i * nt + j, 0))
    n_ctx_tiles = n_ctx // tt
    out_index, out_shape, in_place = _stream_out(b, t, d, tt, n_ctx, latents_only)
    return pl.pallas_call(
        functools.partial(_combine_kernel, gate_idx=gate_idx),
        grid_spec=pltpu.PrefetchScalarGridSpec(
            num_scalar_prefetch=2,
            grid=(b, nt),
            in_specs=[pl.BlockSpec((1, tt, d), lambda i, j, d1, d2: (i, j, 0)),
                      pl.BlockSpec(memory_space=pl.ANY),
                      col, col,
                      pl.BlockSpec((1, d), lambda i, j, d1, d2: (0, 0)),
                      pl.BlockSpec((1, 6, d), lambda i, j, d1, d2: (jnp.where(j < n_ctx_tiles, b, i), 0, 0))],
            out_specs=pl.BlockSpec((1, tt, d), out_index),
            scratch_shapes=[pltpu.VMEM((tt, d), F32), pltpu.VMEM((tt, d), F32),
                            pltpu.SemaphoreType.DMA(())]),
        out_shape=out_shape,
        input_output_aliases={2: 0} if in_place else {},
        compiler_params=_params("arbitrary", "arbitrary"),
        name="moe_combine",
    )(dest1, dest2, h, yg, w1.reshape(-1, 1), w2.reshape(-1, 1), g.reshape(1, d), mods)


def moe_ffn_residual(h, a, w_router, b_router, wg, wu, wd, layer, g, mods, gate_idx, n_ctx,
                     latents_only=False):
    m, d = a.shape
    n_exp = w_router.shape[1]
    tm = MOE_ROW_TILE
    info, counts = route(a, w_router, b_router)
    i1, i2 = info[0].astype(jnp.int32), info[1].astype(jnp.int32)
    w1, w2 = info[2], info[3]
    r1, r2 = info[4].astype(jnp.int32), info[5].astype(jnp.int32)

    counts = counts[:, 0].astype(jnp.int32)
    padded = (counts + tm - 1) // tm * tm
    ends = jnp.cumsum(padded)
    starts = ends - padded
    n_tiles = (TOP_K * m) // tm + n_exp
    tile_expert = jnp.sum(jnp.arange(n_tiles)[:, None] >= (ends // tm)[None, :], axis=1)
    tile_expert = jnp.minimum(tile_expert, n_exp - 1).astype(jnp.int32)
    n_used = (ends[-1:] // tm).astype(jnp.int32)
    dest1 = starts[i1] + r1
    dest2 = starts[i2] + r2

    xg = dispatch(a, dest1, dest2, n_tiles * tm)
    yg = expert_ffn(xg, tile_expert, n_used, wg, wu, wd, layer)
    return combine_residual(h, yg, dest1, dest2, w1, w2, g, mods, gate_idx, n_ctx, latents_only)


def kernel(x, c, ctx, c_ctx, w_mod, b_mod, g_pre_mix, g_post_mix, g_pre_ffn, g_post_ffn, w_in,
           hgrn_lb, hgrn_norm_g, diff_lambda, diff_norm_g, qk_norm_q, qk_norm_k, w_branch, w_out,
           w_ff_gate, w_ff_up, w_ff_down, w_router, b_router, w_moe_gate, w_moe_up, w_moe_down):
    n_batch, seq, d = x.shape
    n_ctx = ctx.shape[1]
    depth = w_in.shape[0]
    t = n_ctx + seq
    m = n_batch * t

    hw = HG_HEADS * HEAD
    kv_sizes = (hw, hw, hw, DF_HEADS * 2 * DF_DIM, DF_HEADS * 2 * DF_DIM,
                GQ_KV_HEADS * HEAD, GQ_KV_HEADS * HEAD)
    q_sizes = (hw, hw, DF_HEADS * 2 * DF_DIM, GQ_HEADS * HEAD, FN_GROUPS * HEAD, N_BRANCH * d)
    offs = np.concatenate([[0], np.cumsum(kv_sizes + q_sizes)]).astype(int)
    (c_ff, c_fb, c_hv, c_dfk, c_dfv, c_gqk, c_gqv,
     c_hq, c_hg, c_dfq, c_gqq, c_fu, c_gl) = [int(o) for o in offs[:-1]]

    rope_df = rope_tables(n_ctx, seq, DF_DIM)
    rope_gq = rope_tables(n_ctx, seq, HEAD)
    dft_chan = dft_matrices(HEAD)
    dft_ctx = dft_matrices(n_ctx)
    dft_lat = dft_matrices(seq)

    lb_all = jnp.cumsum(jax.nn.softmax(hgrn_lb.astype(F32), axis=0), axis=0)
    lb_all = lb_all - lb_all[:1]

    n_cond = n_batch + 1
    pad = (-n_cond) % 16
    cond = jnp.concatenate([c, c_ctx[None, :], jnp.zeros((pad, d), F32)], axis=0)
    mods_all = modulation(cond, w_mod, b_mod)[:, :n_cond].reshape(depth, n_cond, 6, d)

    h = jnp.concatenate([ctx, x], axis=1)

    (w_in, w_branch, w_out, w_ff_gate, w_ff_up, w_ff_down, w_moe_gate, w_moe_up, w_moe_down) = [
        w.astype(BF16) for w in (w_in, w_branch, w_out, w_ff_gate, w_ff_up, w_ff_down,
                                 w_moe_gate, w_moe_up, w_moe_down)]

    for l in range(depth):
        mods = mods_all[l]
        lam_init = 0.8 - 0.6 * math.exp(-0.3 * l)
        lv = diff_lambda[l].astype(F32)
        lam = jnp.exp(jnp.sum(lv[0] * lv[1])) - jnp.exp(jnp.sum(lv[2] * lv[3])) + lam_init

        a = norm_mod(h, g_pre_mix[l], mods, 0, 1, n_ctx)
        p2 = matmul(a.reshape(m, d), w_in, l, BF16, tm=1024, tn=1536)
        p = p2.reshape(n_batch, t, -1)

        y_hg = hgrn_scan(p, (c_hq, c_ff, c_fb, c_hv, c_hg), lb_all[l], hgrn_norm_g[l], n_ctx)

        df_q1, df_q2, df_k, gq_q, gq_k = qk_prep(p, (c_dfq, c_dfk, c_gqq, c_gqk), rope_df, rope_gq,
                                                 qk_norm_q[l], qk_norm_k[l])
        y_df = attention(df_q1, df_k, p, c_dfv, DF_HEADS, n_ctx, q2=df_q2, lam=lam,
                         g=diff_norm_g[l], post_scale=1.0 - lam_init)

        y_fn = fourier_mix(p, c_fu, n_ctx, dft_chan, dft_ctx, dft_lat)

        assert GQ_HEADS == 2 * GQ_KV_HEADS
        y_gq = attention(gq_q, gq_k, p, c_gqv, GQ_KV_HEADS, n_ctx)

        merged = merge((y_hg, y_df, y_fn, y_gq), p2, c_gl, w_branch, l)
        h = matmul_residual(merged.reshape(n_batch, t, d), w_out, l, h, g_post_mix[l], mods, 2, n_ctx)

        j = l // 2
        last = l == depth - 1
        if l % 2 == 0:
            f_in = norm_mod(h, g_pre_ffn[l], mods, 3, 4, n_ctx).reshape(m, d)
            hidden = swiglu_up(f_in, w_ff_gate, w_ff_up, j)
            y = matmul(hidden, w_ff_down, j, F32, tm=1024, tn=512)
            h = residual(h, y, g_post_ffn[l], mods, 5, n_ctx, latents_only=last)
        else:
            f_in = norm_mod(h, g_pre_ffn[l], mods, 3, 4, n_ctx, out_dtype=F32).reshape(m, d)
            h = moe_ffn_residual(h, f_in, w_router[j], b_router[j], w_moe_gate, w_moe_up,
                                 w_moe_down, j, g_post_ffn[l], mods, 5, n_ctx, latents_only=last)

    return h
```

```python
import functools
import math

import jax
import jax.numpy as jnp
import numpy as np
from jax import lax
from jax.experimental import pallas as pl
from jax.experimental.pallas import tpu as pltpu

F32 = jnp.float32
BF16 = jnp.bfloat16

EPS = 1e-6
GRID_W = 64
ROPE_THETA = 10000.0
HEAD = 128
HG_HEADS = 4
GLA_CHUNK = 32
DF_HEADS = 4
DF_DIM = 64
FN_GROUPS = 4
GQ_HEADS = 4
GQ_KV_HEADS = 2
N_BRANCH = 4
BRANCH_W = 512
TOP_K = 2

V7X_VMEM_LIMIT_BYTES = 56 * 1024 * 1024
SCAN_BLOCK = 256
SCAN_HEADS_PER_STEP = 2


def _params(*sem):
    return pltpu.CompilerParams(dimension_semantics=sem, vmem_limit_bytes=V7X_VMEM_LIMIT_BYTES)


def _tile(n, pref, align=128):
    if n <= pref:
        return n
    t = pref - pref % align
    while n % t:
        t -= align
    return t


def _mm_kernel(a_ref, w_ref, o_ref):
    o_ref[...] = jnp.dot(a_ref[...], w_ref[...].astype(BF16),
                         preferred_element_type=F32).astype(o_ref.dtype)


def _mm_acc_kernel(a_ref, w_ref, o_ref, acc_ref):
    k = pl.program_id(2)

    @pl.when(k == 0)
    def _():
        acc_ref[...] = jnp.zeros_like(acc_ref)

    acc_ref[...] += jnp.dot(a_ref[...], w_ref[...].astype(BF16), preferred_element_type=F32)

    @pl.when(k == pl.num_programs(2) - 1)
    def _():
        o_ref[...] = acc_ref[...].astype(o_ref.dtype)


def matmul(a, w, layer, out_dtype, tm=1024, tn=1024, tk=None):
    m, k = a.shape
    n = w.shape[2]
    tm, tn = _tile(m, tm), _tile(n, tn)
    if tk is None or tk >= k:
        return pl.pallas_call(
            _mm_kernel,
            grid=(m // tm, n // tn),
            in_specs=[pl.BlockSpec((tm, k), lambda i, j: (i, 0)),
                      pl.BlockSpec((None, k, tn), lambda i, j: (layer, 0, j))],
            out_specs=pl.BlockSpec((tm, tn), lambda i, j: (i, j)),
            out_shape=jax.ShapeDtypeStruct((m, n), out_dtype),
            compiler_params=_params("parallel", "parallel"),
            name="matmul",
        )(a, w)
    tk = _tile(k, tk)
    return pl.pallas_call(
        _mm_acc_kernel,
        grid=(m // tm, n // tn, k // tk),
        in_specs=[pl.BlockSpec((tm, tk), lambda i, j, q: (i, q)),
                  pl.BlockSpec((None, tk, tn), lambda i, j, q: (layer, q, j))],
        out_specs=pl.BlockSpec((tm, tn), lambda i, j, q: (i, j)),
        out_shape=jax.ShapeDtypeStruct((m, n), out_dtype),
        scratch_shapes=[pltpu.VMEM((tm, tn), F32)],
        compiler_params=_params("parallel", "parallel", "arbitrary"),
        name="matmul_acc",
    )(a, w)


def _mod_kernel(s_ref, w_ref, b_ref, o_ref):
    s = s_ref[...]
    s = (s * jax.nn.sigmoid(s)).astype(BF16)
    o_ref[0] = jnp.dot(s, w_ref[0].astype(BF16), preferred_element_type=F32) + b_ref[0]


def modulation(cond, w_mod, b_mod):
    depth, d, width = w_mod.shape
    r = cond.shape[0]
    tn = _tile(width, 1024)
    return pl.pallas_call(
        _mod_kernel,
        grid=(depth, width // tn),
        in_specs=[pl.BlockSpec((r, d), lambda l, j: (0, 0)),
                  pl.BlockSpec((1, d, tn), lambda l, j: (l, 0, j)),
                  pl.BlockSpec((1, 1, tn), lambda l, j: (l, 0, j))],
        out_specs=pl.BlockSpec((1, r, tn), lambda l, j: (l, 0, j)),
        out_shape=jax.ShapeDtypeStruct((depth, r, width), F32),
        compiler_params=_params("parallel", "parallel"),
        name="modulation",
    )(cond, w_mod, b_mod.reshape(depth, 1, width))


def _rms(x):
    return x * lax.rsqrt(jnp.mean(x * x, axis=-1, keepdims=True) + EPS)


def _norm_mod_kernel(h_ref, g_ref, mod_ref, o_ref, *, shift_idx, scale_idx):
    y = _rms(h_ref[0]) * g_ref[...]
    shift = mod_ref[0, shift_idx:shift_idx + 1, :]
    scale = mod_ref[0, scale_idx:scale_idx + 1, :]
    o_ref[0] = (y * (1.0 + scale) + shift).astype(o_ref.dtype)


def _mod_index(n_ctx_tiles, n_batch):
    return lambda b, t: (jnp.where(t < n_ctx_tiles, n_batch, b), 0, 0)


def norm_mod(h, g, mods, shift_idx, scale_idx, n_ctx, out_dtype=BF16):
    b, t, d = h.shape
    tt = _tile(n_ctx, 256)
    return pl.pallas_call(
        functools.partial(_norm_mod_kernel, shift_idx=shift_idx, scale_idx=scale_idx),
        grid=(b, t // tt),
        in_specs=[pl.BlockSpec((1, tt, d), lambda i, j: (i, j, 0)),
                  pl.BlockSpec((1, d), lambda i, j: (0, 0)),
                  pl.BlockSpec((1, 6, d), _mod_index(n_ctx // tt, b))],
        out_specs=pl.BlockSpec((1, tt, d), lambda i, j: (i, j, 0)),
        out_shape=jax.ShapeDtypeStruct((b, t, d), out_dtype),
        compiler_params=_params("parallel", "parallel"),
        name="norm_mod",
    )(h, g.reshape(1, d), mods)


def _modulated_norm(x, g_ref, mod_ref, shift_idx, scale_idx):
    scale = mod_ref[0, scale_idx:scale_idx + 1, :]
    shift = mod_ref[0, shift_idx:shift_idx + 1, :]
    return _rms(x) * g_ref[...] * (1.0 + scale) + shift


def _resid_kernel(h_ref, y_ref, g_ref, mod_ref, *rest, gate_idx, then_norm):
    gate = mod_ref[0, gate_idx:gate_idx + 1, :]
    new = h_ref[0] + gate * (_rms(y_ref[0]) * g_ref[...])
    if then_norm:
        g2_ref, mod2_ref, o_ref, a_ref = rest
        a_ref[0] = _modulated_norm(new, g2_ref, mod2_ref, *then_norm).astype(a_ref.dtype)
    else:
        (o_ref,) = rest
    o_ref[0] = new


def residual(h, y, g, mods, gate_idx, n_ctx, latents_only=False, next_norm=None):
    b, t, d = h.shape
    tt = _tile(n_ctx, 256)
    out_index, out_shape, in_place = _stream_out(b, t, d, tt, n_ctx, latents_only)
    row = pl.BlockSpec((1, tt, d), lambda i, j: (i, j, 0))
    vec = pl.BlockSpec((1, d), lambda i, j: (0, 0))
    mod = pl.BlockSpec((1, 6, d), _mod_index(n_ctx // tt, b))
    in_specs, args = [row, row, vec, mod], [h, y.reshape(b, t, d), g.reshape(1, d), mods]
    out_specs, out_shapes = pl.BlockSpec((1, tt, d), out_index), out_shape
    if next_norm is not None:
        g2, mods2 = next_norm[:2]
        in_specs, args = in_specs + [vec, mod], args + [g2.reshape(1, d), mods2]
        out_specs, out_shapes = [out_specs, row], [out_shape, jax.ShapeDtypeStruct((b, t, d), BF16)]
    return pl.pallas_call(
        functools.partial(_resid_kernel, gate_idx=gate_idx,
                          then_norm=None if next_norm is None else tuple(next_norm[2:])),
        grid=(b, t // tt),
        in_specs=in_specs,
        out_specs=out_specs,
        out_shape=out_shapes,
        input_output_aliases={0: 0} if in_place else {},
        compiler_params=_params("parallel", "arbitrary"),
        name="residual",
    )(*args)


def _mm_resid_kernel(a_ref, w_ref, h_ref, g_ref, g2_ref, mod_ref, modc_ref, o_ref, f_ref, *,
                     gate_idx, shift_idx, scale_idx, n_ctx):
    tt = a_ref.shape[1]
    y = jnp.dot(a_ref[0], w_ref[...].astype(BF16), preferred_element_type=F32)
    is_ctx = pl.program_id(1) * tt + lax.broadcasted_iota(jnp.int32, (tt, 1), 0) < n_ctx

    def pick(idx):
        return jnp.where(is_ctx, modc_ref[0, idx:idx + 1, :], mod_ref[0, idx:idx + 1, :])

    new = h_ref[0] + pick(gate_idx) * (_rms(y) * g_ref[...])
    o_ref[0] = new
    f_ref[0] = (_rms(new) * g2_ref[...] * (1.0 + pick(scale_idx)) + pick(shift_idx)).astype(f_ref.dtype)


def matmul_residual(a, w, layer, h, g, mods, gate_idx, n_ctx, g2, shift_idx, scale_idx, norm_dtype):
    b, t, d = h.shape
    k = a.shape[-1]
    tt = _tile(t, 272, 16)
    row = pl.BlockSpec((1, tt, d), lambda i, j: (i, j, 0))
    vec = pl.BlockSpec((1, d), lambda i, j: (0, 0))
    return pl.pallas_call(
        functools.partial(_mm_resid_kernel, gate_idx=gate_idx, shift_idx=shift_idx,
                          scale_idx=scale_idx, n_ctx=n_ctx),
        grid=(b, t // tt),
        in_specs=[pl.BlockSpec((1, tt, k), lambda i, j: (i, j, 0)),
                  pl.BlockSpec((None, k, d), lambda i, j: (layer, 0, 0)),
                  row, vec, vec,
                  pl.BlockSpec((1, 6, d), lambda i, j: (i, 0, 0)),
                  pl.BlockSpec((1, 6, d), lambda i, j: (b, 0, 0))],
        out_specs=[row, row],
        out_shape=[jax.ShapeDtypeStruct((b, t, d), F32), jax.ShapeDtypeStruct((b, t, d), norm_dtype)],
        input_output_aliases={2: 0},
        compiler_params=_params("parallel", "parallel"),
        name="matmul_residual",
    )(a, w, h, g.reshape(1, d), g2.reshape(1, d), mods, mods)


def rope_tables(n_ctx, seq, dim):
    quarter = dim // 4
    inv = ROPE_THETA ** (-jnp.arange(quarter, dtype=F32) / quarter)
    pos = jnp.arange(seq)
    ar = (pos // GRID_W).astype(F32)[:, None] * inv
    ac = (pos % GRID_W).astype(F32)[:, None] * inv
    ang = jnp.concatenate([ar, ar, ac, ac], axis=-1)
    ang = jnp.tile(ang, (1, HEAD // dim))
    ang = jnp.concatenate([jnp.zeros((n_ctx, HEAD), F32), ang], axis=0)
    first = ((jnp.arange(HEAD) % dim) % (dim // 2)) < quarter
    cos, sin = jnp.cos(ang), jnp.sin(ang)
    return cos, jnp.where(first, -sin, 0.0), jnp.where(first, 0.0, sin)


def _rope(x, cos_ref, sa_ref, sb_ref, quarter):
    return (x * cos_ref[...] + pltpu.roll(x, HEAD - quarter, 1) * sa_ref[...]
            + pltpu.roll(x, quarter, 1) * sb_ref[...])


def _qk_prep_kernel(dfq_ref, dfk_ref, gqq_ref, gqk_ref, dcos, dsa, dsb, gcos, gsa, gsb, gq_ref, gk_ref,
                    q1_ref, q2_ref, dk_ref, gqo_ref, gko_ref):
    lane = lax.broadcasted_iota(jnp.int32, dcos.shape, 1)
    for h in range(DF_HEADS):
        sl = slice(h * HEAD, (h + 1) * HEAD)
        y = _rope(dfq_ref[0, :, sl].astype(F32), dcos, dsa, dsb, DF_DIM // 4) * DF_DIM ** -0.5
        q1_ref[0, :, sl] = jnp.where(lane < DF_DIM, y, 0.0).astype(BF16)
        q2_ref[0, :, sl] = jnp.where(lane >= DF_DIM, y, 0.0).astype(BF16)
        dk_ref[0, :, sl] = _rope(dfk_ref[0, :, sl].astype(F32), dcos, dsa, dsb,
                                 DF_DIM // 4).astype(BF16)
    for h in range(GQ_HEADS):
        sl = slice(h * HEAD, (h + 1) * HEAD)
        x = _rms(gqq_ref[0, :, sl].astype(F32)) * gq_ref[...]
        gqo_ref[0, :, sl] = (_rope(x, gcos, gsa, gsb, HEAD // 4) * HEAD ** -0.5).astype(BF16)
    for h in range(GQ_KV_HEADS):
        sl = slice(h * HEAD, (h + 1) * HEAD)
        x = _rms(gqk_ref[0, :, sl].astype(F32)) * gk_ref[...]
        gko_ref[0, :, sl] = _rope(x, gcos, gsa, gsb, HEAD // 4).astype(BF16)


def qk_prep(p, cols, rope_df, rope_gq, g_q, g_k):
    b, t, _ = p.shape
    tt = _tile(t, 1088, 16)
    widths = (DF_HEADS * HEAD, DF_HEADS * HEAD, GQ_HEADS * HEAD, GQ_KV_HEADS * HEAD)
    assert all(c % w == 0 for c, w in zip(cols, widths))

    def col_spec(c0, w):
        return pl.BlockSpec((1, tt, w), lambda i, j: (i, j, c0 // w))

    def out_spec(w):
        return pl.BlockSpec((1, tt, w), lambda i, j: (i, j, 0))

    tab = pl.BlockSpec((tt, HEAD), lambda i, j: (j, 0))
    vec = pl.BlockSpec((1, HEAD), lambda i, j: (0, 0))
    out_w = (widths[0], widths[0], widths[1], widths[2], widths[3])
    return pl.pallas_call(
        _qk_prep_kernel,
        grid=(b, t // tt),
        in_specs=[col_spec(c, w) for c, w in zip(cols, widths)] + [tab] * 6 + [vec, vec],
        out_specs=[out_spec(w) for w in out_w],
        out_shape=[jax.ShapeDtypeStruct((b, t, w), BF16) for w in out_w],
        compiler_params=_params("parallel", "parallel"),
        name="qk_prep",
    )(p, p, p, p, *rope_df, *rope_gq, g_q.reshape(1, HEAD), g_k.reshape(1, HEAD))


ATTN_Q_TILE = 256
ATTN_K_CHUNK = 512


def _attend(q, k_ref, vx_ref, nk):
    kc = ATTN_K_CHUNK
    m = acc = None
    for c0 in range(0, nk, kc):
        c1 = min(c0 + kc, nk)
        s = lax.dot_general(q, k_ref[0, c0:c1, :], (((1,), (1,)), ((), ())),
                            preferred_element_type=F32)
        m_c = jnp.max(s, axis=-1, keepdims=True)
        m_new = m_c if m is None else jnp.maximum(m, m_c)
        pv = jnp.dot(jnp.exp((s - m_new).astype(BF16)), vx_ref[c0:c1, :],
                     preferred_element_type=F32)
        acc = pv if acc is None else acc * jnp.exp(m - m_new) + pv
        m = m_new
    return acc[:, :HEAD] / acc[:, HEAD:]


def _attn_kernel(*refs, diff, n_ctx, tq, post_scale):
    if diff:
        lam_ref, qa_ref, qb_ref, k_ref, v_ref, g_ref, o_ref, vx_ref = refs
    else:
        qa_ref, k_ref, v_ref, o_ref, vx_ref = refs
    n_keys = k_ref.shape[1]

    def rows(r0, n, nk):
        if diff:
            qa, qb = qa_ref[0, pl.ds(r0, n), :], qb_ref[0, pl.ds(r0, n), :]
        else:
            pair = qa_ref[0, pl.ds(r0, n), :]
            qa, qb = pair[:, :HEAD], pair[:, HEAD:]
        oa = _attend(qa, k_ref, vx_ref, nk)
        ob = _attend(qb, k_ref, vx_ref, nk)
        if diff:
            o = oa - lam_ref[...] * ob
            o = _rms(o) * g_ref[...] * post_scale
        else:
            o = jnp.concatenate([oa, ob], axis=1)
        o_ref[0, pl.ds(r0, n), :] = o.astype(o_ref.dtype)

    j = pl.program_id(2)

    @pl.when(j == 0)
    def _():
        vx_ref[:, :HEAD] = v_ref[0]
        vx_ref[:, HEAD:] = jnp.ones((n_keys, HEAD), BF16)
        rows(0, n_ctx, n_ctx)

    rows(pl.multiple_of(n_ctx + j * tq, math.gcd(n_ctx, tq)), tq, n_keys)


def attention(q, k, v, v_col0, n_groups, n_ctx, q2=None, lam=None, g=None, post_scale=1.0):
    b, t, _ = k.shape
    tq = _tile(t - n_ctx, ATTN_Q_TILE, 16)
    vc = v_col0 // HEAD
    diff = q2 is not None
    qw = HEAD if diff else 2 * HEAD
    qspec = pl.BlockSpec((1, t, qw), lambda i, h, j: (i, 0, h))
    kspec = pl.BlockSpec((1, t, HEAD), lambda i, h, j: (i, 0, h))
    vspec = pl.BlockSpec((1, t, HEAD), lambda i, h, j: (i, 0, vc + h))
    if diff:
        one = pl.BlockSpec((1, 1), lambda i, h, j: (0, 0))
        in_specs = [one, qspec, qspec, kspec, vspec, pl.BlockSpec((1, HEAD), lambda i, h, j: (0, 0))]
        args = (lam.reshape(1, 1).astype(F32), q, q2, k, v, g.reshape(1, HEAD))
    else:
        in_specs = [qspec, kspec, vspec]
        args = (q, k, v)
    return pl.pallas_call(
        functools.partial(_attn_kernel, diff=diff, n_ctx=n_ctx, tq=tq, post_scale=post_scale),
        grid=(b, n_groups, (t - n_ctx) // tq),
        in_specs=in_specs,
        out_specs=qspec,
        out_shape=jax.ShapeDtypeStruct((b, t, n_groups * qw), BF16),
        scratch_shapes=[pltpu.VMEM((t, 2 * HEAD), BF16)],
        compiler_params=_params("parallel", "parallel", "arbitrary"),
        name="diff_attention" if diff else "gq_attention",
    )(*args)


def _chunk_prefix(x, row_in_chunk, c):
    s = 1
    while s < c:
        x = x + jnp.where(row_in_chunk >= s, pltpu.roll(x, s, 0), 0.0)
        s *= 2
    return x


def _scan_kernel(q_ref, zf_ref, zb_ref, v_ref, hg_ref, lb_ref, g_ref, o_ref,
                 of_ref, ob_ref, sf_ref, sb_ref, *, n_ctx_blocks):
    rows = SCAN_BLOCK
    c = GLA_CHUNK
    n_chunks = rows // c
    n_blocks = q_ref.shape[1] // rows
    half = c // 2

    r_i = lax.broadcasted_iota(jnp.int32, (rows, rows), 0)
    s_i = lax.broadcasted_iota(jnp.int32, (rows, rows), 1)
    shift = c.bit_length() - 1
    same = (r_i >> shift) == (s_i >> shift)
    mask_f = same & (s_i <= r_i)
    mask_b = same & (s_i >= r_i)
    row_in_chunk = lax.broadcasted_iota(jnp.int32, (rows, HEAD), 0) & (c - 1)

    def chunk_row(x, row):
        return x.reshape(n_chunks, c, HEAD)[:, row:row + 1, :]

    def spread(x3):
        return jnp.broadcast_to(x3, (n_chunks, c, HEAD)).reshape(rows, HEAD)

    def block(blk, hd, z_ref, lb, s_ref, out_ref, mask, reverse):
        r0 = pl.multiple_of(blk * rows, rows)
        z = z_ref[0, pl.ds(r0, rows), hd].astype(F32)
        q = q_ref[0, pl.ds(r0, rows), hd].astype(F32)
        v = v_ref[0, pl.ds(r0, rows), hd]
        kk = (1.0 - lb) * jax.nn.sigmoid(-z)
        lf = jnp.log(1.0 - kk)
        pre = _chunk_prefix(lf, row_in_chunk, c)
        end3 = chunk_row(pre, c - 1)
        if reverse:
            cum = spread(end3) - pre + lf
            mid3 = chunk_row(cum, half)
        else:
            cum = pre
            mid3 = chunk_row(cum, half - 1)
        mid, end = spread(mid3), spread(end3)
        qa = (q * jnp.exp(cum - mid)).astype(BF16)
        ka = (kk * jnp.exp(mid - cum)).astype(BF16)
        att = lax.dot_general(qa, ka, (((1,), (1,)), ((), ())), preferred_element_type=F32)
        att = jnp.where(mask, att, 0.0).astype(BF16)
        o_intra = jnp.dot(att, v, preferred_element_type=F32)
        q_dec = (q * jnp.exp(cum)).astype(BF16)
        k_dec = (kk * jnp.exp(end - cum)).astype(BF16)
        dec = jnp.exp(end3)
        st = s_ref[...]
        outs = [None] * n_chunks
        order = range(n_chunks - 1, -1, -1) if reverse else range(n_chunks)
        for n in order:
            sl = slice(n * c, (n + 1) * c)
            outs[n] = lax.dot_general(q_dec[sl], st.astype(BF16), (((1,), (1,)), ((), ())),
                                      preferred_element_type=F32)
            upd = lax.dot_general(v[sl], k_dec[sl], (((0,), (0,)), ((), ())),
                                  preferred_element_type=F32)
            st = st * dec[n] + upd
        s_ref[...] = st
        out_ref[pl.ds(r0, rows), hd] = o_intra + jnp.concatenate(outs, axis=0)

    sf_ref[...] = jnp.zeros_like(sf_ref)
    sb_ref[...] = jnp.zeros_like(sb_ref)
    heads = [slice(k * HEAD, (k + 1) * HEAD) for k in range(q_ref.shape[2] // HEAD)]

    def step(i, carry):
        j = jnp.where(i < n_ctx_blocks, n_ctx_blocks - 1 - i, n_blocks - 1 - i + n_ctx_blocks)
        for k, hd in enumerate(heads):
            block(i, hd, zf_ref, lb_ref[0:1, hd], sf_ref.at[k], of_ref, mask_f, False)
            block(j, hd, zb_ref, lb_ref[1:2, hd], sb_ref.at[k], ob_ref, mask_b, True)
        return carry

    lax.fori_loop(0, n_blocks, step, 0)

    def finish(i, carry):
        r0 = pl.multiple_of(i * rows, rows)
        for hd in heads:
            o = of_ref[pl.ds(r0, rows), hd] + ob_ref[pl.ds(r0, rows), hd]
            gate = hg_ref[0, pl.ds(r0, rows), hd].astype(F32)
            y = _rms(o) * g_ref[...] * (gate * jax.nn.sigmoid(gate))
            o_ref[0, pl.ds(r0, rows), hd] = y.astype(o_ref.dtype)
        return carry

    lax.fori_loop(0, n_blocks, finish, 0)


def hgrn_scan(p, cols, lb, g, n_ctx):
    b, t, _ = p.shape
    assert t % SCAN_BLOCK == 0 and n_ctx % SCAN_BLOCK == 0
    hp = SCAN_HEADS_PER_STEP
    w = hp * HEAD
    assert HG_HEADS % hp == 0 and all(c0 % w == 0 for c0 in cols)

    def col(c0):
        return pl.BlockSpec((1, t, w), lambda i, h: (i, 0, c0 // w + h))

    return pl.pallas_call(
        functools.partial(_scan_kernel, n_ctx_blocks=n_ctx // SCAN_BLOCK),
        grid=(b, HG_HEADS // hp),
        in_specs=[col(c0) for c0 in cols] + [pl.BlockSpec((2, w), lambda i, h: (0, h)),
                                             pl.BlockSpec((1, HEAD), lambda i, h: (0, 0))],
        out_specs=pl.BlockSpec((1, t, w), lambda i, h: (i, 0, h)),
        out_shape=jax.ShapeDtypeStruct((b, t, HG_HEADS * HEAD), BF16),
        scratch_shapes=[pltpu.VMEM((t, w), F32), pltpu.VMEM((t, w), F32),
                        pltpu.VMEM((hp, HEAD, HEAD), F32), pltpu.VMEM((hp, HEAD, HEAD), F32)],
        compiler_params=_params("parallel", "parallel"),
        name="hgrn_scan",
    )(p, p, p, p, p, lb, g.reshape(1, HEAD))


def dft_matrices(n):
    idx = jnp.arange(n, dtype=jnp.int32)
    ang = ((idx[:, None] * idx[None, :]) % n).astype(F32) * (2.0 * math.pi / n)
    s = 1.0 / math.sqrt(n)
    return (jnp.cos(ang) * s).astype(BF16), (jnp.sin(ang) * s).astype(BF16)


def _fn_channel_kernel(u_ref, cc_ref, sc_ref, a_ref, b_ref):
    for grp in range(FN_GROUPS):
        sl = slice(grp * HEAD, (grp + 1) * HEAD)
        u = u_ref[0, :, sl]
        a_ref[0, :, sl] = jnp.dot(u, cc_ref[...], preferred_element_type=F32).astype(BF16)
        b_ref[0, :, sl] = jnp.dot(u, sc_ref[...], preferred_element_type=F32).astype(BF16)


def _fn_position_kernel(a_ref, b_ref, cl_ref, sl_ref, cx_ref, sx_ref, o_ref, *, n_ctx):
    i = pl.program_id(1)

    @pl.when(i == 0)
    def _():
        a = a_ref[0, :n_ctx, :]
        b = b_ref[0, :n_ctx, :]
        o_ref[0] = (jnp.dot(cx_ref[...], a, preferred_element_type=F32)
                    - jnp.dot(sx_ref[...], b, preferred_element_type=F32)).astype(o_ref.dtype)

    @pl.when(i > 0)
    def _():
        a = a_ref[0, n_ctx:, :]
        b = b_ref[0, n_ctx:, :]
        o_ref[0] = (jnp.dot(cl_ref[...], a, preferred_element_type=F32)
                    - jnp.dot(sl_ref[...], b, preferred_element_type=F32)).astype(o_ref.dtype)


def fourier_mix(p, col0, n_ctx, dft_chan, dft_ctx, dft_lat):
    b, t, _ = p.shape
    w = FN_GROUPS * HEAD
    tt = _tile(t, 1088, 16)
    assert col0 % w == 0
    blk = pl.BlockSpec((1, tt, w), lambda i, j: (i, j, 0))
    mat = pl.BlockSpec((HEAD, HEAD), lambda i, j: (0, 0))
    ua, ub = pl.pallas_call(
        _fn_channel_kernel,
        grid=(b, t // tt),
        in_specs=[pl.BlockSpec((1, tt, w), lambda i, j: (i, j, col0 // w)), mat, mat],
        out_specs=[blk, blk],
        out_shape=[jax.ShapeDtypeStruct((b, t, w), BF16)] * 2,
        compiler_params=_params("parallel", "parallel"),
        name="fourier_channels",
    )(p, *dft_chan)
    seq = t - n_ctx
    tm = n_ctx
    assert seq % tm == 0 and tm % 8 == 0
    full = pl.BlockSpec((1, t, w), lambda i, j: (i, 0, 0))
    lat = pl.BlockSpec((tm, seq), lambda i, j: (jnp.maximum(j - 1, 0), 0))
    ctx = pl.BlockSpec((n_ctx, n_ctx), lambda i, j: (0, 0))
    return pl.pallas_call(
        functools.partial(_fn_position_kernel, n_ctx=n_ctx),
        grid=(b, 1 + seq // tm),
        in_specs=[full, full, lat, lat, ctx, ctx],
        out_specs=pl.BlockSpec((1, tm, w), lambda i, j: (i, j, 0)),
        out_shape=jax.ShapeDtypeStruct((b, t, w), BF16),
        compiler_params=_params("parallel", "parallel"),
        name="fourier_positions",
    )(ua, ub, *dft_lat, *dft_ctx)


def _merge_kernel(*refs):
    ys, gs, ws, o_ref = refs[:N_BRANCH], refs[N_BRANCH:2 * N_BRANCH], refs[2 * N_BRANCH:-1], refs[-1]
    acc = None
    for y_ref, g_ref, w_ref in zip(ys, gs, ws):
        term = jax.nn.sigmoid(g_ref[...]).astype(F32) * jnp.dot(
            y_ref[...], w_ref[0].astype(BF16), preferred_element_type=F32)
        acc = term if acc is None else acc + term
    o_ref[...] = acc.astype(o_ref.dtype)


def merge(branches, p2, gate_col0, w_branch, layer):
    m = p2.shape[0]
    d = w_branch.shape[-1]
    tm, tn = _tile(m, 1024), _tile(d, 512)
    y_specs = [pl.BlockSpec((tm, BRANCH_W), lambda i, j: (i, 0))] * N_BRANCH
    g_specs = [pl.BlockSpec((tm, tn), functools.partial(
        lambda i, j, off: (i, off + j), off=(gate_col0 + k * d) // tn)) for k in range(N_BRANCH)]
    w_specs = [pl.BlockSpec((None, 1, BRANCH_W, tn),
                            functools.partial(lambda i, j, k: (layer, k, 0, j), k=k))
               for k in range(N_BRANCH)]
    return pl.pallas_call(
        _merge_kernel,
        grid=(m // tm, d // tn),
        in_specs=y_specs + g_specs + w_specs,
        out_specs=pl.BlockSpec((tm, tn), lambda i, j: (i, j)),
        out_shape=jax.ShapeDtypeStruct((m, d), BF16),
        compiler_params=_params("parallel", "parallel"),
        name="merge",
    )(*[y.reshape(m, BRANCH_W) for y in branches], *([p2] * N_BRANCH), *([w_branch] * N_BRANCH))


def _swiglu_kernel(a_ref, wg_ref, wu_ref, o_ref):
    a = a_ref[...]
    gate = jnp.dot(a, wg_ref[...].astype(BF16), preferred_element_type=F32)
    up = jnp.dot(a, wu_ref[...].astype(BF16), preferred_element_type=F32)
    o_ref[...] = (gate * jax.nn.sigmoid(gate) * up).astype(BF16)


def swiglu_up(a, wg, wu, layer):
    m, d = a.shape
    f = wg.shape[2]
    tm, tn = _tile(m, 1024), _tile(f, 512)
    return pl.pallas_call(
        _swiglu_kernel,
        grid=(m // tm, f // tn),
        in_specs=[pl.BlockSpec((tm, d), lambda i, j: (i, 0)),
                  pl.BlockSpec((None, d, tn), lambda i, j: (layer, 0, j)),
                  pl.BlockSpec((None, d, tn), lambda i, j: (layer, 0, j))],
        out_specs=pl.BlockSpec((tm, tn), lambda i, j: (i, j)),
        out_shape=jax.ShapeDtypeStruct((m, f), BF16),
        compiler_params=_params("parallel", "parallel"),
        name="swiglu_up",
    )(a, wg, wu)


MOE_ROW_TILE = 512
MOE_INFO_ROWS = 8


def _router_kernel(a_ref, w_ref, b_ref, info_ref, cnt_ref, carry_ref):
    @pl.when(pl.program_id(0) == 0)
    def _():
        carry_ref[...] = jnp.zeros_like(carry_ref)

    logits = lax.dot_general(w_ref[...], a_ref[...].astype(BF16), (((1,), (1,)), ((), ())),
                             preferred_element_type=F32) + b_ref[...]
    n_exp, tm = logits.shape
    e = lax.broadcasted_iota(jnp.int32, logits.shape, 0).astype(F32)
    m1 = jnp.max(logits, axis=0, keepdims=True)
    i1 = jnp.min(jnp.where(logits == m1, e, float(n_exp)), axis=0, keepdims=True)
    rest = jnp.where(e == i1, -jnp.inf, logits)
    m2 = jnp.max(rest, axis=0, keepdims=True)
    i2 = jnp.min(jnp.where(rest == m2, e, float(n_exp)), axis=0, keepdims=True)
    x = jnp.exp(m2 - m1)
    w1 = 1.0 / (1.0 + x)
    w2 = x * w1

    chosen = jnp.where((e == i1) | (e == i2), 1.0, 0.0)
    s_i = lax.broadcasted_iota(jnp.int32, (tm, tm), 0)
    t_i = lax.broadcasted_iota(jnp.int32, (tm, tm), 1)
    earlier = (s_i < t_i).astype(BF16)
    rank = jnp.dot(chosen.astype(BF16), earlier, preferred_element_type=F32) + carry_ref[:, 0:1]
    r1 = jnp.sum(jnp.where(e == i1, rank, 0.0), axis=0, keepdims=True)
    r2 = jnp.sum(jnp.where(e == i2, rank, 0.0), axis=0, keepdims=True)
    carry_ref[...] = carry_ref[...] + jnp.sum(chosen, axis=1, keepdims=True)
    cnt_ref[...] = carry_ref[...]

    row = lax.broadcasted_iota(jnp.int32, (MOE_INFO_ROWS, tm), 0)
    info = jnp.zeros((MOE_INFO_ROWS, tm), F32)
    for k, val in enumerate((i1, i2, w1, w2, r1, r2)):
        info = jnp.where(row == k, val, info)
    info_ref[...] = info


def route(a, w_router, b_router):
    m, d = a.shape
    n_exp = w_router.shape[1]
    tm = _tile(m, 1024)
    return pl.pallas_call(
        _router_kernel,
        grid=(m // tm,),
        in_specs=[pl.BlockSpec((tm, d), lambda i: (i, 0)),
                  pl.BlockSpec((n_exp, d), lambda i: (0, 0)),
                  pl.BlockSpec((n_exp, 1), lambda i: (0, 0))],
        out_specs=[pl.BlockSpec((MOE_INFO_ROWS, tm), lambda i: (0, i)),
                   pl.BlockSpec((n_exp, HEAD), lambda i: (0, 0))],
        out_shape=[jax.ShapeDtypeStruct((MOE_INFO_ROWS, m), F32),
                   jax.ShapeDtypeStruct((n_exp, HEAD), F32)],
        scratch_shapes=[pltpu.VMEM((n_exp, HEAD), F32)],
        compiler_params=_params("arbitrary"),
        name="router",
    )(a, w_router.T.astype(BF16), b_router.reshape(n_exp, 1).astype(F32))


def _row_copies(src_ref, dst_ref, src_row, dst_row, sem):
    return pltpu.make_async_copy(src_ref.at[pl.ds(src_row, 1), :], dst_ref.at[pl.ds(dst_row, 1), :], sem)


def _wait_rows(src_ref, dst_ref, n_rows, sem):
    pltpu.make_async_copy(src_ref.at[pl.ds(0, n_rows), :], dst_ref.at[pl.ds(0, n_rows), :], sem).wait()


def _dispatch_kernel(d1_ref, d2_ref, x_ref, init_ref, xg_ref, sem):
    del init_ref
    tt = x_ref.shape[0]
    base = pl.program_id(0) * tt

    def send(r, carry):
        _row_copies(x_ref, xg_ref, r, d1_ref[base + r], sem).start()
        _row_copies(x_ref, xg_ref, r, d2_ref[base + r], sem).start()
        return carry

    lax.fori_loop(0, tt, send, 0, unroll=8)
    _wait_rows(x_ref, xg_ref, tt, sem)
    _wait_rows(x_ref, xg_ref, tt, sem)


def dispatch(x, dest1, dest2, n_rows):
    m, d = x.shape
    tt = _tile(m, 256)
    any_spec = pl.BlockSpec(memory_space=pl.ANY)
    return pl.pallas_call(
        _dispatch_kernel,
        grid_spec=pltpu.PrefetchScalarGridSpec(
            num_scalar_prefetch=2,
            grid=(m // tt,),
            in_specs=[pl.BlockSpec((tt, d), lambda i, d1, d2: (i, 0)), any_spec],
            out_specs=any_spec,
            scratch_shapes=[pltpu.SemaphoreType.DMA(())]),
        out_shape=jax.ShapeDtypeStruct((n_rows, d), F32),
        input_output_aliases={3: 0},
        compiler_params=_params("arbitrary"),
        name="moe_dispatch",
    )(dest1, dest2, x, jnp.zeros((n_rows, d), F32))


def _expert_ffn_kernel(te_ref, nu_ref, x_ref, wg_ref, wu_ref, wd_ref, o_ref):
    i, j = pl.program_id(0), pl.program_id(1)
    used = i < nu_ref[0]

    @pl.when(used)
    def _():
        a = x_ref[...].astype(BF16)
        gate = jnp.dot(a, wg_ref[0].astype(BF16), preferred_element_type=F32)
        up = jnp.dot(a, wu_ref[0].astype(BF16), preferred_element_type=F32)
        hidden = (gate * jax.nn.sigmoid(gate) * up).astype(BF16)
        part = jnp.dot(hidden, wd_ref[0].astype(BF16), preferred_element_type=F32)

        @pl.when(j == 0)
        def _():
            o_ref[...] = part

        @pl.when(j > 0)
        def _():
            o_ref[...] += part

    @pl.when(jnp.logical_not(used) & (j == 0))
    def _():
        o_ref[...] = jnp.zeros_like(o_ref)


def expert_ffn(xg, tile_expert, n_used, wg, wu, wd, layer):
    r, d = xg.shape
    f = wg.shape[-1]
    tm, tf = MOE_ROW_TILE, _tile(f, 1024)
    nf = f // tf

    def hidden_block(i, j, nu):
        return jnp.where(i < nu[0], j, nf - 1)

    def up_index(i, j, te, nu):
        return (layer, te[i], 0, hidden_block(i, j, nu))

    return pl.pallas_call(
        _expert_ffn_kernel,
        grid_spec=pltpu.PrefetchScalarGridSpec(
            num_scalar_prefetch=2,
            grid=(r // tm, nf),
            in_specs=[pl.BlockSpec((tm, d), lambda i, j, te, nu: (i, 0)),
                      pl.BlockSpec((None, 1, d, tf), up_index),
                      pl.BlockSpec((None, 1, d, tf), up_index),
                      pl.BlockSpec((None, 1, tf, d),
                                   lambda i, j, te, nu: (layer, te[i], hidden_block(i, j, nu), 0))],
            out_specs=pl.BlockSpec((tm, d), lambda i, j, te, nu: (i, 0))),
        out_shape=jax.ShapeDtypeStruct((r, d), F32),
        compiler_params=_params("parallel", "arbitrary"),
        name="expert_ffn",
    )(tile_expert, n_used, xg, wg, wu, wd)


def _combine_kernel(d1_ref, d2_ref, h_ref, y_ref, w1_ref, w2_ref, g_ref, mod_ref, *rest,
                    gate_idx, then_norm):
    if then_norm:
        g2_ref, mod2_ref, o_ref, a_ref, y1_ref, y2_ref, sem = rest
    else:
        o_ref, y1_ref, y2_ref, sem = rest
    tt = h_ref.shape[1]
    base = (pl.program_id(0) * pl.num_programs(1) + pl.program_id(1)) * tt

    def fetch(r, carry):
        _row_copies(y_ref, y1_ref, d1_ref[base + r], r, sem).start()
        _row_copies(y_ref, y2_ref, d2_ref[base + r], r, sem).start()
        return carry

    lax.fori_loop(0, tt, fetch, 0, unroll=8)
    _wait_rows(y_ref, y1_ref, tt, sem)
    _wait_rows(y_ref, y2_ref, tt, sem)
    y = w1_ref[...] * y1_ref[...] + w2_ref[...] * y2_ref[...]
    gate = mod_ref[0, gate_idx:gate_idx + 1, :]
    new = h_ref[0] + gate * (_rms(y) * g_ref[...])
    o_ref[0] = new
    if then_norm:
        a_ref[0] = _modulated_norm(new, g2_ref, mod2_ref, *then_norm).astype(a_ref.dtype)


def _stream_out(b, t, d, tt, n_ctx, latents_only):
    if not latents_only:
        return (lambda i, j, *_: (i, j, 0)), jax.ShapeDtypeStruct((b, t, d), F32), True
    skip = n_ctx // tt
    return ((lambda i, j, *_: (i, jnp.maximum(j - skip, 0), 0)),
            jax.ShapeDtypeStruct((b, t - n_ctx, d), F32), False)


def combine_residual(h, yg, dest1, dest2, w1, w2, g, mods, gate_idx, n_ctx, latents_only=False,
                     next_norm=None):
    b, t, d = h.shape
    tt = _tile(n_ctx, 256)
    nt = t // tt
    col = pl.BlockSpec((tt, 1), lambda i, j, d1, d2: (i * nt + j, 0))
    n_ctx_tiles = n_ctx // tt
    out_index, out_shape, in_place = _stream_out(b, t, d, tt, n_ctx, latents_only)
    row = pl.BlockSpec((1, tt, d), lambda i, j, d1, d2: (i, j, 0))
    vec = pl.BlockSpec((1, d), lambda i, j, d1, d2: (0, 0))
    mod = pl.BlockSpec((1, 6, d), lambda i, j, d1, d2: (jnp.where(j < n_ctx_tiles, b, i), 0, 0))
    in_specs = [row, pl.BlockSpec(memory_space=pl.ANY), col, col, vec, mod]
    args = [dest1, dest2, h, yg, w1.reshape(-1, 1), w2.reshape(-1, 1), g.reshape(1, d), mods]
    out_specs, out_shapes = pl.BlockSpec((1, tt, d), out_index), out_shape
    if next_norm is not None:
        g2, mods2 = next_norm[:2]
        in_specs, args = in_specs + [vec, mod], args + [g2.reshape(1, d), mods2]
        out_specs, out_shapes = [out_specs, row], [out_shape, jax.ShapeDtypeStruct((b, t, d), BF16)]
    return pl.pallas_call(
        functools.partial(_combine_kernel, gate_idx=gate_idx,
                          then_norm=None if next_norm is None else tuple(next_norm[2:])),
        grid_spec=pltpu.PrefetchScalarGridSpec(
            num_scalar_prefetch=2,
            grid=(b, nt),
            in_specs=in_specs,
            out_specs=out_specs,
            scratch_shapes=[pltpu.VMEM((tt, d), F32), pltpu.VMEM((tt, d), F32),
                            pltpu.SemaphoreType.DMA(())]),
        out_shape=out_shapes,
        input_output_aliases={2: 0} if in_place else {},
        compiler_params=_params("arbitrary", "arbitrary"),
        name="moe_combine",
    )(*args)


def moe_ffn_residual(h, a, w_router, b_router, wg, wu, wd, layer, g, mods, gate_idx, n_ctx,
                     latents_only=False, next_norm=None):
    m, d = a.shape
    n_exp = w_router.shape[1]
    tm = MOE_ROW_TILE
    info, counts = route(a, w_router, b_router)
    i1, i2 = info[0].astype(jnp.int32), info[1].astype(jnp.int32)
    w1, w2 = info[2], info[3]
    r1, r2 = info[4].astype(jnp.int32), info[5].astype(jnp.int32)

    counts = counts[:, 0].astype(jnp.int32)
    padded = (counts + tm - 1) // tm * tm
    ends = jnp.cumsum(padded)
    starts = ends - padded
    n_tiles = (TOP_K * m) // tm + n_exp
    tile_expert = jnp.sum(jnp.arange(n_tiles)[:, None] >= (ends // tm)[None, :], axis=1)
    tile_expert = jnp.minimum(tile_expert, n_exp - 1).astype(jnp.int32)
    n_used = (ends[-1:] // tm).astype(jnp.int32)
    dest1 = starts[i1] + r1
    dest2 = starts[i2] + r2

    xg = dispatch(a, dest1, dest2, n_tiles * tm)
    yg = expert_ffn(xg, tile_expert, n_used, wg, wu, wd, layer)
    return combine_residual(h, yg, dest1, dest2, w1, w2, g, mods, gate_idx, n_ctx, latents_only,
                            next_norm)


def kernel(x, c, ctx, c_ctx, w_mod, b_mod, g_pre_mix, g_post_mix, g_pre_ffn, g_post_ffn, w_in,
           hgrn_lb, hgrn_norm_g, diff_lambda, diff_norm_g, qk_norm_q, qk_norm_k, w_branch, w_out,
           w_ff_gate, w_ff_up, w_ff_down, w_router, b_router, w_moe_gate, w_moe_up, w_moe_down):
    n_batch, seq, d = x.shape
    n_ctx = ctx.shape[1]
    depth = w_in.shape[0]
    t = n_ctx + seq
    m = n_batch * t

    hw = HG_HEADS * HEAD
    kv_sizes = (hw, hw, hw, DF_HEADS * 2 * DF_DIM, DF_HEADS * 2 * DF_DIM,
                GQ_KV_HEADS * HEAD, GQ_KV_HEADS * HEAD)
    q_sizes = (hw, hw, DF_HEADS * 2 * DF_DIM, GQ_HEADS * HEAD, FN_GROUPS * HEAD, N_BRANCH * d)
    offs = np.concatenate([[0], np.cumsum(kv_sizes + q_sizes)]).astype(int)
    (c_ff, c_fb, c_hv, c_dfk, c_dfv, c_gqk, c_gqv,
     c_hq, c_hg, c_dfq, c_gqq, c_fu, c_gl) = [int(o) for o in offs[:-1]]

    rope_df = rope_tables(n_ctx, seq, DF_DIM)
    rope_gq = rope_tables(n_ctx, seq, HEAD)
    dft_chan = dft_matrices(HEAD)
    dft_ctx = dft_matrices(n_ctx)
    dft_lat = dft_matrices(seq)

    lb_all = jnp.cumsum(jax.nn.softmax(hgrn_lb.astype(F32), axis=0), axis=0)
    lb_all = lb_all - lb_all[:1]

    n_cond = n_batch + 1
    pad = (-n_cond) % 16
    cond = jnp.concatenate([c, c_ctx[None, :], jnp.zeros((pad, d), F32)], axis=0)
    mods_all = modulation(cond, w_mod, b_mod)[:, :n_cond].reshape(depth, n_cond, 6, d)

    h = jnp.concatenate([ctx, x], axis=1)

    (w_in, w_branch, w_out, w_ff_gate, w_ff_up, w_ff_down, w_moe_gate, w_moe_up, w_moe_down) = [
        w.astype(BF16) for w in (w_in, w_branch, w_out, w_ff_gate, w_ff_up, w_ff_down,
                                 w_moe_gate, w_moe_up, w_moe_down)]

    for l in range(depth):
        mods = mods_all[l]
        lam_init = 0.8 - 0.6 * math.exp(-0.3 * l)
        lv = diff_lambda[l].astype(F32)
        lam = jnp.exp(jnp.sum(lv[0] * lv[1])) - jnp.exp(jnp.sum(lv[2] * lv[3])) + lam_init

        if l == 0:
            a = norm_mod(h, g_pre_mix[l], mods, 0, 1, n_ctx)
        p2 = matmul(a.reshape(m, d), w_in, l, BF16, tm=1024, tn=1536)
        p = p2.reshape(n_batch, t, -1)

        y_hg = hgrn_scan(p, (c_hq, c_ff, c_fb, c_hv, c_hg), lb_all[l], hgrn_norm_g[l], n_ctx)

        df_q1, df_q2, df_k, gq_q, gq_k = qk_prep(p, (c_dfq, c_dfk, c_gqq, c_gqk), rope_df, rope_gq,
                                                 qk_norm_q[l], qk_norm_k[l])
        y_df = attention(df_q1, df_k, p, c_dfv, DF_HEADS, n_ctx, q2=df_q2, lam=lam,
                         g=diff_norm_g[l], post_scale=1.0 - lam_init)

        y_fn = fourier_mix(p, c_fu, n_ctx, dft_chan, dft_ctx, dft_lat)

        assert GQ_HEADS == 2 * GQ_KV_HEADS
        y_gq = attention(gq_q, gq_k, p, c_gqv, GQ_KV_HEADS, n_ctx)

        merged = merge((y_hg, y_df, y_fn, y_gq), p2, c_gl, w_branch, l)
        dense = l % 2 == 0
        h, f_in = matmul_residual(merged.reshape(n_batch, t, d), w_out, l, h, g_post_mix[l], mods, 2,
                                  n_ctx, g_pre_ffn[l], 3, 4, BF16 if dense else F32)
        f_in = f_in.reshape(m, d)

        j = l // 2
        last = l == depth - 1
        next_norm = None if last else (g_pre_mix[l + 1], mods_all[l + 1], 0, 1)
        if dense:
            hidden = swiglu_up(f_in, w_ff_gate, w_ff_up, j)
            y = matmul(hidden, w_ff_down, j, F32, tm=1024, tn=512)
            out = residual(h, y, g_post_ffn[l], mods, 5, n_ctx, last, next_norm)
        else:
            out = moe_ffn_residual(h, f_in, w_router[j], b_router[j], w_moe_gate, w_moe_up,
                                   w_moe_down, j, g_post_ffn[l], mods, 5, n_ctx, last, next_norm)
        h, a = (out, None) if last else out

    return h
```

```python
import functools
import math

import jax
import jax.numpy as jnp
import numpy as np
from jax import lax
from jax.experimental import pallas as pl
from jax.experimental.pallas import tpu as pltpu

F32 = jnp.float32
BF16 = jnp.bfloat16

EPS = 1e-6
GRID_W = 64
ROPE_THETA = 10000.0
HEAD = 128
HG_HEADS = 4
GLA_CHUNK = 32
DF_HEADS = 4
DF_DIM = 64
FN_GROUPS = 4
GQ_HEADS = 4
GQ_KV_HEADS = 2
N_BRANCH = 4
BRANCH_W = 512
TOP_K = 2

V7X_VMEM_LIMIT_BYTES = 56 * 1024 * 1024
SCAN_BLOCK = 256
SCAN_HEADS_PER_STEP = 2


def _params(*sem):
    return pltpu.CompilerParams(dimension_semantics=sem, vmem_limit_bytes=V7X_VMEM_LIMIT_BYTES)


def _tile(n, pref, align=128):
    if n <= pref:
        return n
    t = pref - pref % align
    while n % t:
        t -= align
    return t


def _mm_kernel(a_ref, w_ref, o_ref):
    o_ref[...] = jnp.dot(a_ref[...], w_ref[...], preferred_element_type=F32).astype(o_ref.dtype)


def _mm_f32w_kernel(a_ref, w_ref, o_ref, wb_ref):
    @pl.when(pl.program_id(1) == 0)
    def _():
        wb_ref[...] = w_ref[...].astype(BF16)

    o_ref[...] = jnp.dot(a_ref[...], wb_ref[...], preferred_element_type=F32).astype(o_ref.dtype)


def matmul(a, w, layer, out_dtype, tm=1024, tn=1024):
    m, k = a.shape
    n = w.shape[2]
    tm, tn = _tile(m, tm), _tile(n, tn)
    if w.dtype == BF16:
        return pl.pallas_call(
            _mm_kernel,
            grid=(m // tm, n // tn),
            in_specs=[pl.BlockSpec((tm, k), lambda i, j: (i, 0)),
                      pl.BlockSpec((None, k, tn), lambda i, j: (layer, 0, j))],
            out_specs=pl.BlockSpec((tm, tn), lambda i, j: (i, j)),
            out_shape=jax.ShapeDtypeStruct((m, n), out_dtype),
            compiler_params=_params("parallel", "parallel"),
            name="matmul",
        )(a, w)
    return pl.pallas_call(
        _mm_f32w_kernel,
        grid=(n // tn, m // tm),
        in_specs=[pl.BlockSpec((tm, k), lambda j, i: (i, 0)),
                  pl.BlockSpec((None, k, tn), lambda j, i: (layer, 0, j))],
        out_specs=pl.BlockSpec((tm, tn), lambda j, i: (i, j)),
        out_shape=jax.ShapeDtypeStruct((m, n), out_dtype),
        scratch_shapes=[pltpu.VMEM((k, tn), BF16)],
        compiler_params=_params("parallel", "arbitrary"),
        name="matmul_f32w",
    )(a, w)


def _mod_kernel(s_ref, w_ref, b_ref, o_ref):
    s = s_ref[...]
    s = (s * jax.nn.sigmoid(s)).astype(BF16)
    o_ref[0] = jnp.dot(s, w_ref[0].astype(BF16), preferred_element_type=F32) + b_ref[0]


def modulation(cond, w_mod, b_mod):
    depth, d, width = w_mod.shape
    r = cond.shape[0]
    tn = _tile(width, 1024)
    return pl.pallas_call(
        _mod_kernel,
        grid=(depth, width // tn),
        in_specs=[pl.BlockSpec((r, d), lambda l, j: (0, 0)),
                  pl.BlockSpec((1, d, tn), lambda l, j: (l, 0, j)),
                  pl.BlockSpec((1, 1, tn), lambda l, j: (l, 0, j))],
        out_specs=pl.BlockSpec((1, r, tn), lambda l, j: (l, 0, j)),
        out_shape=jax.ShapeDtypeStruct((depth, r, width), F32),
        compiler_params=_params("parallel", "parallel"),
        name="modulation",
    )(cond, w_mod, b_mod.reshape(depth, 1, width))


def _rms(x):
    return x * lax.rsqrt(jnp.mean(x * x, axis=-1, keepdims=True) + EPS)


def _norm_mod_kernel(h_ref, g_ref, mod_ref, o_ref, *, shift_idx, scale_idx):
    y = _rms(h_ref[0]) * g_ref[...]
    shift = mod_ref[0, shift_idx:shift_idx + 1, :]
    scale = mod_ref[0, scale_idx:scale_idx + 1, :]
    o_ref[0] = (y * (1.0 + scale) + shift).astype(o_ref.dtype)


def _mod_index(n_ctx_tiles, n_batch):
    return lambda b, t: (jnp.where(t < n_ctx_tiles, n_batch, b), 0, 0)


def norm_mod(h, g, mods, shift_idx, scale_idx, n_ctx, out_dtype=BF16):
    b, t, d = h.shape
    tt = _tile(n_ctx, 256)
    return pl.pallas_call(
        functools.partial(_norm_mod_kernel, shift_idx=shift_idx, scale_idx=scale_idx),
        grid=(b, t // tt),
        in_specs=[pl.BlockSpec((1, tt, d), lambda i, j: (i, j, 0)),
                  pl.BlockSpec((1, d), lambda i, j: (0, 0)),
                  pl.BlockSpec((1, 6, d), _mod_index(n_ctx // tt, b))],
        out_specs=pl.BlockSpec((1, tt, d), lambda i, j: (i, j, 0)),
        out_shape=jax.ShapeDtypeStruct((b, t, d), out_dtype),
        compiler_params=_params("parallel", "parallel"),
        name="norm_mod",
    )(h, g.reshape(1, d), mods)


def _modulated_norm(x, g_ref, mod_ref, shift_idx, scale_idx):
    scale = mod_ref[0, scale_idx:scale_idx + 1, :]
    shift = mod_ref[0, shift_idx:shift_idx + 1, :]
    return _rms(x) * g_ref[...] * (1.0 + scale) + shift


def _resid_kernel(h_ref, y_ref, g_ref, mod_ref, *rest, gate_idx, then_norm):
    gate = mod_ref[0, gate_idx:gate_idx + 1, :]
    new = h_ref[0] + gate * (_rms(y_ref[0]) * g_ref[...])
    if then_norm:
        g2_ref, mod2_ref, o_ref, a_ref = rest
        a_ref[0] = _modulated_norm(new, g2_ref, mod2_ref, *then_norm).astype(a_ref.dtype)
    else:
        (o_ref,) = rest
    o_ref[0] = new


def residual(h, y, g, mods, gate_idx, n_ctx, latents_only=False, next_norm=None):
    b, t, d = h.shape
    tt = _tile(n_ctx, 256)
    out_index, out_shape, in_place = _stream_out(b, t, d, tt, n_ctx, latents_only)
    row = pl.BlockSpec((1, tt, d), lambda i, j: (i, j, 0))
    vec = pl.BlockSpec((1, d), lambda i, j: (0, 0))
    mod = pl.BlockSpec((1, 6, d), _mod_index(n_ctx // tt, b))
    in_specs, args = [row, row, vec, mod], [h, y.reshape(b, t, d), g.reshape(1, d), mods]
    out_specs, out_shapes = pl.BlockSpec((1, tt, d), out_index), out_shape
    if next_norm is not None:
        g2, mods2 = next_norm[:2]
        in_specs, args = in_specs + [vec, mod], args + [g2.reshape(1, d), mods2]
        out_specs, out_shapes = [out_specs, row], [out_shape, jax.ShapeDtypeStruct((b, t, d), BF16)]
    return pl.pallas_call(
        functools.partial(_resid_kernel, gate_idx=gate_idx,
                          then_norm=None if next_norm is None else tuple(next_norm[2:])),
        grid=(b, t // tt),
        in_specs=in_specs,
        out_specs=out_specs,
        out_shape=out_shapes,
        input_output_aliases={0: 0} if in_place else {},
        compiler_params=_params("parallel", "arbitrary"),
        name="residual",
    )(*args)


def _mm_resid_kernel(a_ref, w_ref, h_ref, g_ref, g2_ref, mod_ref, modc_ref, o_ref, f_ref, *,
                     gate_idx, shift_idx, scale_idx, n_ctx):
    tt = a_ref.shape[1]
    y = jnp.dot(a_ref[0], w_ref[...].astype(BF16), preferred_element_type=F32)
    is_ctx = pl.program_id(1) * tt + lax.broadcasted_iota(jnp.int32, (tt, 1), 0) < n_ctx

    def pick(idx):
        return jnp.where(is_ctx, modc_ref[0, idx:idx + 1, :], mod_ref[0, idx:idx + 1, :])

    new = h_ref[0] + pick(gate_idx) * (_rms(y) * g_ref[...])
    o_ref[0] = new
    f_ref[0] = (_rms(new) * g2_ref[...] * (1.0 + pick(scale_idx)) + pick(shift_idx)).astype(f_ref.dtype)


def matmul_residual(a, w, layer, h, g, mods, gate_idx, n_ctx, g2, shift_idx, scale_idx, norm_dtype):
    b, t, d = h.shape
    k = a.shape[-1]
    tt = _tile(t, 272, 16)
    row = pl.BlockSpec((1, tt, d), lambda i, j: (i, j, 0))
    vec = pl.BlockSpec((1, d), lambda i, j: (0, 0))
    return pl.pallas_call(
        functools.partial(_mm_resid_kernel, gate_idx=gate_idx, shift_idx=shift_idx,
                          scale_idx=scale_idx, n_ctx=n_ctx),
        grid=(b, t // tt),
        in_specs=[pl.BlockSpec((1, tt, k), lambda i, j: (i, j, 0)),
                  pl.BlockSpec((None, k, d), lambda i, j: (layer, 0, 0)),
                  row, vec, vec,
                  pl.BlockSpec((1, 6, d), lambda i, j: (i, 0, 0)),
                  pl.BlockSpec((1, 6, d), lambda i, j: (b, 0, 0))],
        out_specs=[row, row],
        out_shape=[jax.ShapeDtypeStruct((b, t, d), F32), jax.ShapeDtypeStruct((b, t, d), norm_dtype)],
        input_output_aliases={2: 0},
        compiler_params=_params("parallel", "parallel"),
        name="matmul_residual",
    )(a, w, h, g.reshape(1, d), g2.reshape(1, d), mods, mods)


def rope_tables(n_ctx, seq, dim):
    quarter = dim // 4
    inv = ROPE_THETA ** (-jnp.arange(quarter, dtype=F32) / quarter)
    pos = jnp.arange(seq)
    ar = (pos // GRID_W).astype(F32)[:, None] * inv
    ac = (pos % GRID_W).astype(F32)[:, None] * inv
    ang = jnp.concatenate([ar, ar, ac, ac], axis=-1)
    ang = jnp.tile(ang, (1, HEAD // dim))
    ang = jnp.concatenate([jnp.zeros((n_ctx, HEAD), F32), ang], axis=0)
    first = ((jnp.arange(HEAD) % dim) % (dim // 2)) < quarter
    cos, sin = jnp.cos(ang), jnp.sin(ang)
    return cos, jnp.where(first, -sin, 0.0), jnp.where(first, 0.0, sin)


def _rope(x, cos_ref, sa_ref, sb_ref, quarter):
    return (x * cos_ref[...] + pltpu.roll(x, HEAD - quarter, 1) * sa_ref[...]
            + pltpu.roll(x, quarter, 1) * sb_ref[...])


def _qk_prep_kernel(dfq_ref, dfk_ref, gqq_ref, gqk_ref, dcos, dsa, dsb, gcos, gsa, gsb, gq_ref, gk_ref,
                    q1_ref, q2_ref, dk_ref, gqo_ref, gko_ref):
    lane = lax.broadcasted_iota(jnp.int32, dcos.shape, 1)
    for h in range(DF_HEADS):
        sl = slice(h * HEAD, (h + 1) * HEAD)
        y = _rope(dfq_ref[0, :, sl].astype(F32), dcos, dsa, dsb, DF_DIM // 4) * DF_DIM ** -0.5
        q1_ref[0, :, sl] = jnp.where(lane < DF_DIM, y, 0.0).astype(BF16)
        q2_ref[0, :, sl] = jnp.where(lane >= DF_DIM, y, 0.0).astype(BF16)
        dk_ref[0, :, sl] = _rope(dfk_ref[0, :, sl].astype(F32), dcos, dsa, dsb,
                                 DF_DIM // 4).astype(BF16)
    for h in range(GQ_HEADS):
        sl = slice(h * HEAD, (h + 1) * HEAD)
        x = _rms(gqq_ref[0, :, sl].astype(F32)) * gq_ref[...]
        gqo_ref[0, :, sl] = (_rope(x, gcos, gsa, gsb, HEAD // 4) * HEAD ** -0.5).astype(BF16)
    for h in range(GQ_KV_HEADS):
        sl = slice(h * HEAD, (h + 1) * HEAD)
        x = _rms(gqk_ref[0, :, sl].astype(F32)) * gk_ref[...]
        gko_ref[0, :, sl] = _rope(x, gcos, gsa, gsb, HEAD // 4).astype(BF16)


def qk_prep(p, cols, rope_df, rope_gq, g_q, g_k):
    b, t, _ = p.shape
    tt = _tile(t, 1088, 16)
    widths = (DF_HEADS * HEAD, DF_HEADS * HEAD, GQ_HEADS * HEAD, GQ_KV_HEADS * HEAD)
    assert all(c % w == 0 for c, w in zip(cols, widths))

    def col_spec(c0, w):
        return pl.BlockSpec((1, tt, w), lambda i, j: (i, j, c0 // w))

    def out_spec(w):
        return pl.BlockSpec((1, tt, w), lambda i, j: (i, j, 0))

    tab = pl.BlockSpec((tt, HEAD), lambda i, j: (j, 0))
    vec = pl.BlockSpec((1, HEAD), lambda i, j: (0, 0))
    out_w = (widths[0], widths[0], widths[1], widths[2], widths[3])
    return pl.pallas_call(
        _qk_prep_kernel,
        grid=(b, t // tt),
        in_specs=[col_spec(c, w) for c, w in zip(cols, widths)] + [tab] * 6 + [vec, vec],
        out_specs=[out_spec(w) for w in out_w],
        out_shape=[jax.ShapeDtypeStruct((b, t, w), BF16) for w in out_w],
        compiler_params=_params("parallel", "parallel"),
        name="qk_prep",
    )(p, p, p, p, *rope_df, *rope_gq, g_q.reshape(1, HEAD), g_k.reshape(1, HEAD))


ATTN_Q_TILE = 256
ATTN_K_CHUNK = 512


def _attend(q, k_ref, vx_ref, nk):
    kc = ATTN_K_CHUNK
    m = acc = None
    for c0 in range(0, nk, kc):
        c1 = min(c0 + kc, nk)
        s = lax.dot_general(q, k_ref[0, c0:c1, :], (((1,), (1,)), ((), ())),
                            preferred_element_type=F32)
        m_c = jnp.max(s, axis=-1, keepdims=True)
        m_new = m_c if m is None else jnp.maximum(m, m_c)
        pv = jnp.dot(jnp.exp((s - m_new).astype(BF16)), vx_ref[c0:c1, :],
                     preferred_element_type=F32)
        acc = pv if acc is None else acc * jnp.exp(m - m_new) + pv
        m = m_new
    return acc[:, :HEAD] / acc[:, HEAD:]


def _attn_kernel(*refs, diff, n_ctx, tq, post_scale):
    if diff:
        lam_ref, qa_ref, qb_ref, k_ref, v_ref, g_ref, o_ref, vx_ref = refs
    else:
        qa_ref, k_ref, v_ref, o_ref, vx_ref = refs
    n_keys = k_ref.shape[1]

    def rows(r0, n, nk):
        if diff:
            qa, qb = qa_ref[0, pl.ds(r0, n), :], qb_ref[0, pl.ds(r0, n), :]
        else:
            pair = qa_ref[0, pl.ds(r0, n), :]
            qa, qb = pair[:, :HEAD], pair[:, HEAD:]
        oa = _attend(qa, k_ref, vx_ref, nk)
        ob = _attend(qb, k_ref, vx_ref, nk)
        if diff:
            o = oa - lam_ref[...] * ob
            o = _rms(o) * g_ref[...] * post_scale
        else:
            o = jnp.concatenate([oa, ob], axis=1)
        o_ref[0, pl.ds(r0, n), :] = o.astype(o_ref.dtype)

    j = pl.program_id(2)

    @pl.when(j == 0)
    def _():
        vx_ref[:, :HEAD] = v_ref[0]
        vx_ref[:, HEAD:] = jnp.ones((n_keys, HEAD), BF16)
        rows(0, n_ctx, n_ctx)

    rows(pl.multiple_of(n_ctx + j * tq, math.gcd(n_ctx, tq)), tq, n_keys)


def attention(q, k, v, v_col0, n_groups, n_ctx, q2=None, lam=None, g=None, post_scale=1.0):
    b, t, _ = k.shape
    tq = _tile(t - n_ctx, ATTN_Q_TILE, 16)
    vc = v_col0 // HEAD
    diff = q2 is not None
    qw = HEAD if diff else 2 * HEAD
    qspec = pl.BlockSpec((1, t, qw), lambda i, h, j: (i, 0, h))
    kspec = pl.BlockSpec((1, t, HEAD), lambda i, h, j: (i, 0, h))
    vspec = pl.BlockSpec((1, t, HEAD), lambda i, h, j: (i, 0, vc + h))
    if diff:
        one = pl.BlockSpec((1, 1), lambda i, h, j: (0, 0))
        in_specs = [one, qspec, qspec, kspec, vspec, pl.BlockSpec((1, HEAD), lambda i, h, j: (0, 0))]
        args = (lam.reshape(1, 1).astype(F32), q, q2, k, v, g.reshape(1, HEAD))
    else:
        in_specs = [qspec, kspec, vspec]
        args = (q, k, v)
    return pl.pallas_call(
        functools.partial(_attn_kernel, diff=diff, n_ctx=n_ctx, tq=tq, post_scale=post_scale),
        grid=(b, n_groups, (t - n_ctx) // tq),
        in_specs=in_specs,
        out_specs=qspec,
        out_shape=jax.ShapeDtypeStruct((b, t, n_groups * qw), BF16),
        scratch_shapes=[pltpu.VMEM((t, 2 * HEAD), BF16)],
        compiler_params=_params("parallel", "parallel", "arbitrary"),
        name="diff_attention" if diff else "gq_attention",
    )(*args)


def _chunk_prefix(x, row_in_chunk, c):
    s = 1
    while s < c:
        x = x + jnp.where(row_in_chunk >= s, pltpu.roll(x, s, 0), 0.0)
        s *= 2
    return x


def _scan_kernel(q_ref, zf_ref, zb_ref, v_ref, hg_ref, lb_ref, g_ref, o_ref,
                 of_ref, ob_ref, sf_ref, sb_ref, *, n_ctx_blocks):
    rows = SCAN_BLOCK
    c = GLA_CHUNK
    n_chunks = rows // c
    n_blocks = q_ref.shape[1] // rows
    half = c // 2

    r_i = lax.broadcasted_iota(jnp.int32, (rows, rows), 0)
    s_i = lax.broadcasted_iota(jnp.int32, (rows, rows), 1)
    shift = c.bit_length() - 1
    same = (r_i >> shift) == (s_i >> shift)
    mask_f = same & (s_i <= r_i)
    mask_b = same & (s_i >= r_i)
    row_in_chunk = lax.broadcasted_iota(jnp.int32, (rows, HEAD), 0) & (c - 1)

    def chunk_row(x, row):
        return x.reshape(n_chunks, c, HEAD)[:, row:row + 1, :]

    def spread(x3):
        return jnp.broadcast_to(x3, (n_chunks, c, HEAD)).reshape(rows, HEAD)

    def block(blk, hd, z_ref, lb, s_ref, out_ref, mask, reverse):
        r0 = pl.multiple_of(blk * rows, rows)
        z = z_ref[0, pl.ds(r0, rows), hd].astype(F32)
        q = q_ref[0, pl.ds(r0, rows), hd].astype(F32)
        v = v_ref[0, pl.ds(r0, rows), hd]
        kk = (1.0 - lb) * jax.nn.sigmoid(-z)
        lf = jnp.log(1.0 - kk)
        pre = _chunk_prefix(lf, row_in_chunk, c)
        end3 = chunk_row(pre, c - 1)
        if reverse:
            cum = spread(end3) - pre + lf
            mid3 = chunk_row(cum, half)
        else:
            cum = pre
            mid3 = chunk_row(cum, half - 1)
        mid, end = spread(mid3), spread(end3)
        qa = (q * jnp.exp(cum - mid)).astype(BF16)
        ka = (kk * jnp.exp(mid - cum)).astype(BF16)
        att = lax.dot_general(qa, ka, (((1,), (1,)), ((), ())), preferred_element_type=F32)
        att = jnp.where(mask, att, 0.0).astype(BF16)
        o_intra = jnp.dot(att, v, preferred_element_type=F32)
        q_dec = (q * jnp.exp(cum)).astype(BF16)
        k_dec = (kk * jnp.exp(end - cum)).astype(BF16)
        dec = jnp.exp(end3)
        st = s_ref[...]
        outs = [None] * n_chunks
        order = range(n_chunks - 1, -1, -1) if reverse else range(n_chunks)
        for n in order:
            sl = slice(n * c, (n + 1) * c)
            outs[n] = lax.dot_general(q_dec[sl], st.astype(BF16), (((1,), (1,)), ((), ())),
                                      preferred_element_type=F32)
            upd = lax.dot_general(v[sl], k_dec[sl], (((0,), (0,)), ((), ())),
                                  preferred_element_type=F32)
            st = st * dec[n] + upd
        s_ref[...] = st
        out_ref[pl.ds(r0, rows), hd] = o_intra + jnp.concatenate(outs, axis=0)

    sf_ref[...] = jnp.zeros_like(sf_ref)
    sb_ref[...] = jnp.zeros_like(sb_ref)
    heads = [slice(k * HEAD, (k + 1) * HEAD) for k in range(q_ref.shape[2] // HEAD)]

    def step(i, carry):
        j = jnp.where(i < n_ctx_blocks, n_ctx_blocks - 1 - i, n_blocks - 1 - i + n_ctx_blocks)
        for k, hd in enumerate(heads):
            block(i, hd, zf_ref, lb_ref[0:1, hd], sf_ref.at[k], of_ref, mask_f, False)
            block(j, hd, zb_ref, lb_ref[1:2, hd], sb_ref.at[k], ob_ref, mask_b, True)
        return carry

    lax.fori_loop(0, n_blocks, step, 0)

    def finish(i, carry):
        r0 = pl.multiple_of(i * rows, rows)
        for hd in heads:
            o = of_ref[pl.ds(r0, rows), hd] + ob_ref[pl.ds(r0, rows), hd]
            gate = hg_ref[0, pl.ds(r0, rows), hd].astype(F32)
            y = _rms(o) * g_ref[...] * (gate * jax.nn.sigmoid(gate))
            o_ref[0, pl.ds(r0, rows), hd] = y.astype(o_ref.dtype)
        return carry

    lax.fori_loop(0, n_blocks, finish, 0)


def hgrn_scan(p, cols, lb, g, n_ctx):
    b, t, _ = p.shape
    assert t % SCAN_BLOCK == 0 and n_ctx % SCAN_BLOCK == 0
    hp = SCAN_HEADS_PER_STEP
    w = hp * HEAD
    assert HG_HEADS % hp == 0 and all(c0 % w == 0 for c0 in cols)

    def col(c0):
        return pl.BlockSpec((1, t, w), lambda i, h: (i, 0, c0 // w + h))

    return pl.pallas_call(
        functools.partial(_scan_kernel, n_ctx_blocks=n_ctx // SCAN_BLOCK),
        grid=(b, HG_HEADS // hp),
        in_specs=[col(c0) for c0 in cols] + [pl.BlockSpec((2, w), lambda i, h: (0, h)),
                                             pl.BlockSpec((1, HEAD), lambda i, h: (0, 0))],
        out_specs=pl.BlockSpec((1, t, w), lambda i, h: (i, 0, h)),
        out_shape=jax.ShapeDtypeStruct((b, t, HG_HEADS * HEAD), BF16),
        scratch_shapes=[pltpu.VMEM((t, w), F32), pltpu.VMEM((t, w), F32),
                        pltpu.VMEM((hp, HEAD, HEAD), F32), pltpu.VMEM((hp, HEAD, HEAD), F32)],
        compiler_params=_params("parallel", "parallel"),
        name="hgrn_scan",
    )(p, p, p, p, p, lb, g.reshape(1, HEAD))


def dft_matrices(n):
    idx = jnp.arange(n, dtype=jnp.int32)
    ang = ((idx[:, None] * idx[None, :]) % n).astype(F32) * (2.0 * math.pi / n)
    s = 1.0 / math.sqrt(n)
    return (jnp.cos(ang) * s).astype(BF16), (jnp.sin(ang) * s).astype(BF16)


def _fn_channel_kernel(u_ref, cc_ref, sc_ref, a_ref, b_ref):
    for grp in range(FN_GROUPS):
        sl = slice(grp * HEAD, (grp + 1) * HEAD)
        u = u_ref[0, :, sl]
        a_ref[0, :, sl] = jnp.dot(u, cc_ref[...], preferred_element_type=F32).astype(BF16)
        b_ref[0, :, sl] = jnp.dot(u, sc_ref[...], preferred_element_type=F32).astype(BF16)


def _fn_position_kernel(a_ref, b_ref, cl_ref, sl_ref, cx_ref, sx_ref, o_ref, *, n_ctx):
    i = pl.program_id(1)

    @pl.when(i == 0)
    def _():
        a = a_ref[0, :n_ctx, :]
        b = b_ref[0, :n_ctx, :]
        o_ref[0] = (jnp.dot(cx_ref[...], a, preferred_element_type=F32)
                    - jnp.dot(sx_ref[...], b, preferred_element_type=F32)).astype(o_ref.dtype)

    @pl.when(i > 0)
    def _():
        a = a_ref[0, n_ctx:, :]
        b = b_ref[0, n_ctx:, :]
        o_ref[0] = (jnp.dot(cl_ref[...], a, preferred_element_type=F32)
                    - jnp.dot(sl_ref[...], b, preferred_element_type=F32)).astype(o_ref.dtype)


def fourier_mix(p, col0, n_ctx, dft_chan, dft_ctx, dft_lat):
    b, t, _ = p.shape
    w = FN_GROUPS * HEAD
    tt = _tile(t, 1088, 16)
    assert col0 % w == 0
    blk = pl.BlockSpec((1, tt, w), lambda i, j: (i, j, 0))
    mat = pl.BlockSpec((HEAD, HEAD), lambda i, j: (0, 0))
    ua, ub = pl.pallas_call(
        _fn_channel_kernel,
        grid=(b, t // tt),
        in_specs=[pl.BlockSpec((1, tt, w), lambda i, j: (i, j, col0 // w)), mat, mat],
        out_specs=[blk, blk],
        out_shape=[jax.ShapeDtypeStruct((b, t, w), BF16)] * 2,
        compiler_params=_params("parallel", "parallel"),
        name="fourier_channels",
    )(p, *dft_chan)
    seq = t - n_ctx
    tm = n_ctx
    assert seq % tm == 0 and tm % 8 == 0
    full = pl.BlockSpec((1, t, w), lambda i, j: (i, 0, 0))
    lat = pl.BlockSpec((tm, seq), lambda i, j: (jnp.maximum(j - 1, 0), 0))
    ctx = pl.BlockSpec((n_ctx, n_ctx), lambda i, j: (0, 0))
    return pl.pallas_call(
        functools.partial(_fn_position_kernel, n_ctx=n_ctx),
        grid=(b, 1 + seq // tm),
        in_specs=[full, full, lat, lat, ctx, ctx],
        out_specs=pl.BlockSpec((1, tm, w), lambda i, j: (i, j, 0)),
        out_shape=jax.ShapeDtypeStruct((b, t, w), BF16),
        compiler_params=_params("parallel", "parallel"),
        name="fourier_positions",
    )(ua, ub, *dft_lat, *dft_ctx)


def _merge_kernel(*refs):
    ys, gs, ws, o_ref = refs[:N_BRANCH], refs[N_BRANCH:2 * N_BRANCH], refs[2 * N_BRANCH:-1], refs[-1]
    acc = None
    for y_ref, g_ref, w_ref in zip(ys, gs, ws):
        term = jax.nn.sigmoid(g_ref[...].astype(F32)) * jnp.dot(
            y_ref[...], w_ref[0].astype(BF16), preferred_element_type=F32)
        acc = term if acc is None else acc + term
    o_ref[...] = acc.astype(o_ref.dtype)


def merge(branches, p2, gate_col0, w_branch, layer):
    m = p2.shape[0]
    d = w_branch.shape[-1]
    tm, tn = _tile(m, 1024), _tile(d, 512)
    y_specs = [pl.BlockSpec((tm, BRANCH_W), lambda i, j: (i, 0))] * N_BRANCH
    g_specs = [pl.BlockSpec((tm, tn), functools.partial(
        lambda i, j, off: (i, off + j), off=(gate_col0 + k * d) // tn)) for k in range(N_BRANCH)]
    w_specs = [pl.BlockSpec((None, 1, BRANCH_W, tn),
                            functools.partial(lambda i, j, k: (layer, k, 0, j), k=k))
               for k in range(N_BRANCH)]
    return pl.pallas_call(
        _merge_kernel,
        grid=(m // tm, d // tn),
        in_specs=y_specs + g_specs + w_specs,
        out_specs=pl.BlockSpec((tm, tn), lambda i, j: (i, j)),
        out_shape=jax.ShapeDtypeStruct((m, d), BF16),
        compiler_params=_params("parallel", "parallel"),
        name="merge",
    )(*[y.reshape(m, BRANCH_W) for y in branches], *([p2] * N_BRANCH), *([w_branch] * N_BRANCH))


def _swiglu_kernel(a_ref, wg_ref, wu_ref, o_ref, wgb_ref, wub_ref):
    @pl.when(pl.program_id(1) == 0)
    def _():
        wgb_ref[...] = wg_ref[...].astype(BF16)
        wub_ref[...] = wu_ref[...].astype(BF16)

    a = a_ref[...]
    gate = jnp.dot(a, wgb_ref[...], preferred_element_type=F32)
    up = jnp.dot(a, wub_ref[...], preferred_element_type=F32)
    o_ref[...] = (gate * jax.nn.sigmoid(gate) * up).astype(BF16)


def swiglu_up(a, wg, wu, layer):
    m, d = a.shape
    f = wg.shape[2]
    tm, tn = _tile(m, 1024), _tile(f, 512)
    wspec = pl.BlockSpec((None, d, tn), lambda j, i: (layer, 0, j))
    return pl.pallas_call(
        _swiglu_kernel,
        grid=(f // tn, m // tm),
        in_specs=[pl.BlockSpec((tm, d), lambda j, i: (i, 0)), wspec, wspec],
        out_specs=pl.BlockSpec((tm, tn), lambda j, i: (i, j)),
        out_shape=jax.ShapeDtypeStruct((m, f), BF16),
        scratch_shapes=[pltpu.VMEM((d, tn), BF16), pltpu.VMEM((d, tn), BF16)],
        compiler_params=_params("parallel", "arbitrary"),
        name="swiglu_up",
    )(a, wg, wu)


MOE_ROW_TILE = 512
MOE_INFO_ROWS = 8


def _router_kernel(a_ref, w_ref, b_ref, info_ref, cnt_ref, carry_ref):
    @pl.when(pl.program_id(0) == 0)
    def _():
        carry_ref[...] = jnp.zeros_like(carry_ref)

    logits = lax.dot_general(w_ref[...], a_ref[...].astype(BF16), (((1,), (1,)), ((), ())),
                             preferred_element_type=F32) + b_ref[...]
    n_exp, tm = logits.shape
    e = lax.broadcasted_iota(jnp.int32, logits.shape, 0).astype(F32)
    m1 = jnp.max(logits, axis=0, keepdims=True)
    i1 = jnp.min(jnp.where(logits == m1, e, float(n_exp)), axis=0, keepdims=True)
    rest = jnp.where(e == i1, -jnp.inf, logits)
    m2 = jnp.max(rest, axis=0, keepdims=True)
    i2 = jnp.min(jnp.where(rest == m2, e, float(n_exp)), axis=0, keepdims=True)
    x = jnp.exp(m2 - m1)
    w1 = 1.0 / (1.0 + x)
    w2 = x * w1

    chosen = jnp.where((e == i1) | (e == i2), 1.0, 0.0)
    s_i = lax.broadcasted_iota(jnp.int32, (tm, tm), 0)
    t_i = lax.broadcasted_iota(jnp.int32, (tm, tm), 1)
    earlier = (s_i < t_i).astype(BF16)
    rank = jnp.dot(chosen.astype(BF16), earlier, preferred_element_type=F32) + carry_ref[:, 0:1]
    r1 = jnp.sum(jnp.where(e == i1, rank, 0.0), axis=0, keepdims=True)
    r2 = jnp.sum(jnp.where(e == i2, rank, 0.0), axis=0, keepdims=True)
    carry_ref[...] = carry_ref[...] + jnp.sum(chosen, axis=1, keepdims=True)
    cnt_ref[...] = carry_ref[...]

    row = lax.broadcasted_iota(jnp.int32, (MOE_INFO_ROWS, tm), 0)
    info = jnp.zeros((MOE_INFO_ROWS, tm), F32)
    for k, val in enumerate((i1, i2, w1, w2, r1, r2)):
        info = jnp.where(row == k, val, info)
    info_ref[...] = info


def route(a, w_router, b_router):
    m, d = a.shape
    n_exp = w_router.shape[1]
    tm = _tile(m, 1024)
    return pl.pallas_call(
        _router_kernel,
        grid=(m // tm,),
        in_specs=[pl.BlockSpec((tm, d), lambda i: (i, 0)),
                  pl.BlockSpec((n_exp, d), lambda i: (0, 0)),
                  pl.BlockSpec((n_exp, 1), lambda i: (0, 0))],
        out_specs=[pl.BlockSpec((MOE_INFO_ROWS, tm), lambda i: (0, i)),
                   pl.BlockSpec((n_exp, HEAD), lambda i: (0, 0))],
        out_shape=[jax.ShapeDtypeStruct((MOE_INFO_ROWS, m), F32),
                   jax.ShapeDtypeStruct((n_exp, HEAD), F32)],
        scratch_shapes=[pltpu.VMEM((n_exp, HEAD), F32)],
        compiler_params=_params("arbitrary"),
        name="router",
    )(a, w_router.T.astype(BF16), b_router.reshape(n_exp, 1).astype(F32))


def _row_copies(src_ref, dst_ref, src_row, dst_row, sem):
    return pltpu.make_async_copy(src_ref.at[pl.ds(src_row, 1), :], dst_ref.at[pl.ds(dst_row, 1), :], sem)


def _wait_rows(src_ref, dst_ref, n_rows, sem):
    pltpu.make_async_copy(src_ref.at[pl.ds(0, n_rows), :], dst_ref.at[pl.ds(0, n_rows), :], sem).wait()


def _dispatch_kernel(d1_ref, d2_ref, pt_ref, nu_ref, x_ref, xg_ref, zero_ref, sem, zsem):
    tt = x_ref.shape[0]
    tm = zero_ref.shape[0]
    base = pl.program_id(0) * tt

    def zero_tile(tile):
        return pltpu.make_async_copy(zero_ref, xg_ref.at[pl.ds(tile * tm, tm), :], zsem)

    @pl.when(pl.program_id(0) == 0)
    def _():
        zero_ref[...] = jnp.zeros_like(zero_ref)
        for e in range(pt_ref.shape[0]):
            @pl.when(pt_ref[e] >= 0)
            def _():
                zero_tile(pt_ref[e]).start()
        for e in range(pt_ref.shape[0]):
            @pl.when(pt_ref[e] >= 0)
            def _():
                zero_tile(pt_ref[e]).wait()

        def zero_unused(tile, carry):
            zero_tile(tile).start()
            zero_tile(tile).wait()
            return carry

        lax.fori_loop(nu_ref[0], xg_ref.shape[0] // tm, zero_unused, 0)

    def send(r, carry):
        _row_copies(x_ref, xg_ref, r, d1_ref[base + r], sem).start()
        _row_copies(x_ref, xg_ref, r, d2_ref[base + r], sem).start()
        return carry

    lax.fori_loop(0, tt, send, 0, unroll=8)
    _wait_rows(x_ref, xg_ref, tt, sem)
    _wait_rows(x_ref, xg_ref, tt, sem)


def dispatch(x, dest1, dest2, pad_tile, n_used, n_rows):
    m, d = x.shape
    tt = _tile(m, 256)
    return pl.pallas_call(
        _dispatch_kernel,
        grid_spec=pltpu.PrefetchScalarGridSpec(
            num_scalar_prefetch=4,
            grid=(m // tt,),
            in_specs=[pl.BlockSpec((tt, d), lambda i, d1, d2, pt, nu: (i, 0))],
            out_specs=pl.BlockSpec(memory_space=pl.ANY),
            scratch_shapes=[pltpu.VMEM((MOE_ROW_TILE, d), F32), pltpu.SemaphoreType.DMA(()),
                            pltpu.SemaphoreType.DMA(())]),
        out_shape=jax.ShapeDtypeStruct((n_rows, d), F32),
        compiler_params=_params("arbitrary"),
        name="moe_dispatch",
    )(dest1, dest2, pad_tile, n_used, x)


def _expert_ffn_kernel(te_ref, nu_ref, x_ref, wg_ref, wu_ref, wd_ref, o_ref):
    i, j = pl.program_id(0), pl.program_id(1)
    used = i < nu_ref[0]

    @pl.when(used)
    def _():
        a = x_ref[...].astype(BF16)
        gate = jnp.dot(a, wg_ref[0].astype(BF16), preferred_element_type=F32)
        up = jnp.dot(a, wu_ref[0].astype(BF16), preferred_element_type=F32)
        hidden = (gate * jax.nn.sigmoid(gate) * up).astype(BF16)
        part = jnp.dot(hidden, wd_ref[0].astype(BF16), preferred_element_type=F32)

        @pl.when(j == 0)
        def _():
            o_ref[...] = part

        @pl.when(j > 0)
        def _():
            o_ref[...] += part

    @pl.when(jnp.logical_not(used) & (j == 0))
    def _():
        o_ref[...] = jnp.zeros_like(o_ref)


def expert_ffn(xg, tile_expert, n_used, wg, wu, wd, layer):
    r, d = xg.shape
    f = wg.shape[-1]
    tm, tf = MOE_ROW_TILE, _tile(f, 1024)
    nf = f // tf

    def hidden_block(i, j, nu):
        return jnp.where(i < nu[0], j, nf - 1)

    def up_index(i, j, te, nu):
        return (layer, te[i], 0, hidden_block(i, j, nu))

    return pl.pallas_call(
        _expert_ffn_kernel,
        grid_spec=pltpu.PrefetchScalarGridSpec(
            num_scalar_prefetch=2,
            grid=(r // tm, nf),
            in_specs=[pl.BlockSpec((tm, d), lambda i, j, te, nu: (jnp.where(i < nu[0], i, 0), 0)),
                      pl.BlockSpec((None, 1, d, tf), up_index),
                      pl.BlockSpec((None, 1, d, tf), up_index),
                      pl.BlockSpec((None, 1, tf, d),
                                   lambda i, j, te, nu: (layer, te[i], hidden_block(i, j, nu), 0))],
            out_specs=pl.BlockSpec((tm, d), lambda i, j, te, nu: (i, 0))),
        out_shape=jax.ShapeDtypeStruct((r, d), F32),
        compiler_params=_params("parallel", "arbitrary"),
        name="expert_ffn",
    )(tile_expert, n_used, xg, wg, wu, wd)


def _combine_kernel(d1_ref, d2_ref, h_ref, y_ref, w1_ref, w2_ref, g_ref, mod_ref, *rest,
                    gate_idx, then_norm):
    if then_norm:
        g2_ref, mod2_ref, o_ref, a_ref, y1_ref, y2_ref, sem = rest
    else:
        o_ref, y1_ref, y2_ref, sem = rest
    tt = h_ref.shape[1]
    base = (pl.program_id(0) * pl.num_programs(1) + pl.program_id(1)) * tt

    def fetch(r, carry):
        _row_copies(y_ref, y1_ref, d1_ref[base + r], r, sem).start()
        _row_copies(y_ref, y2_ref, d2_ref[base + r], r, sem).start()
        return carry

    lax.fori_loop(0, tt, fetch, 0, unroll=8)
    _wait_rows(y_ref, y1_ref, tt, sem)
    _wait_rows(y_ref, y2_ref, tt, sem)
    y = w1_ref[...] * y1_ref[...] + w2_ref[...] * y2_ref[...]
    gate = mod_ref[0, gate_idx:gate_idx + 1, :]
    new = h_ref[0] + gate * (_rms(y) * g_ref[...])
    o_ref[0] = new
    if then_norm:
        a_ref[0] = _modulated_norm(new, g2_ref, mod2_ref, *then_norm).astype(a_ref.dtype)


def _stream_out(b, t, d, tt, n_ctx, latents_only):
    if not latents_only:
        return (lambda i, j, *_: (i, j, 0)), jax.ShapeDtypeStruct((b, t, d), F32), True
    skip = n_ctx // tt
    return ((lambda i, j, *_: (i, jnp.maximum(j - skip, 0), 0)),
            jax.ShapeDtypeStruct((b, t - n_ctx, d), F32), False)


def combine_residual(h, yg, dest1, dest2, w1, w2, g, mods, gate_idx, n_ctx, latents_only=False,
                     next_norm=None):
    b, t, d = h.shape
    tt = _tile(n_ctx, 256)
    nt = t // tt
    col = pl.BlockSpec((tt, 1), lambda i, j, d1, d2: (i * nt + j, 0))
    n_ctx_tiles = n_ctx // tt
    out_index, out_shape, in_place = _stream_out(b, t, d, tt, n_ctx, latents_only)
    row = pl.BlockSpec((1, tt, d), lambda i, j, d1, d2: (i, j, 0))
    vec = pl.BlockSpec((1, d), lambda i, j, d1, d2: (0, 0))
    mod = pl.BlockSpec((1, 6, d), lambda i, j, d1, d2: (jnp.where(j < n_ctx_tiles, b, i), 0, 0))
    in_specs = [row, pl.BlockSpec(memory_space=pl.ANY), col, col, vec, mod]
    args = [dest1, dest2, h, yg, w1.reshape(-1, 1), w2.reshape(-1, 1), g.reshape(1, d), mods]
    out_specs, out_shapes = pl.BlockSpec((1, tt, d), out_index), out_shape
    if next_norm is not None:
        g2, mods2 = next_norm[:2]
        in_specs, args = in_specs + [vec, mod], args + [g2.reshape(1, d), mods2]
        out_specs, out_shapes = [out_specs, row], [out_shape, jax.ShapeDtypeStruct((b, t, d), BF16)]
    return pl.pallas_call(
        functools.partial(_combine_kernel, gate_idx=gate_idx,
                          then_norm=None if next_norm is None else tuple(next_norm[2:])),
        grid_spec=pltpu.PrefetchScalarGridSpec(
            num_scalar_prefetch=2,
            grid=(b, nt),
            in_specs=in_specs,
            out_specs=out_specs,
            scratch_shapes=[pltpu.VMEM((tt, d), F32), pltpu.VMEM((tt, d), F32),
                            pltpu.SemaphoreType.DMA(())]),
        out_shape=out_shapes,
        input_output_aliases={2: 0} if in_place else {},
        compiler_params=_params("arbitrary", "arbitrary"),
        name="moe_combine",
    )(*args)


def moe_ffn_residual(h, a, w_router, b_router, wg, wu, wd, layer, g, mods, gate_idx, n_ctx,
                     latents_only=False, next_norm=None):
    m, d = a.shape
    n_exp = w_router.shape[1]
    tm = MOE_ROW_TILE
    info, counts = route(a, w_router, b_router)
    i1, i2 = info[0].astype(jnp.int32), info[1].astype(jnp.int32)
    w1, w2 = info[2], info[3]
    r1, r2 = info[4].astype(jnp.int32), info[5].astype(jnp.int32)

    counts = counts[:, 0].astype(jnp.int32)
    padded = (counts + tm - 1) // tm * tm
    ends = jnp.cumsum(padded)
    starts = ends - padded
    n_tiles = (TOP_K * m) // tm + n_exp
    tile_expert = jnp.sum(jnp.arange(n_tiles)[:, None] >= (ends // tm)[None, :], axis=1)
    tile_expert = jnp.minimum(tile_expert, n_exp - 1).astype(jnp.int32)
    n_used = (ends[-1:] // tm).astype(jnp.int32)
    dest1 = starts[i1] + r1
    dest2 = starts[i2] + r2
    pad_tile = jnp.where(counts > 0, ends // tm - 1, -1).astype(jnp.int32)

    xg = dispatch(a, dest1, dest2, pad_tile, n_used, n_tiles * tm)
    yg = expert_ffn(xg, tile_expert, n_used, wg, wu, wd, layer)
    return combine_residual(h, yg, dest1, dest2, w1, w2, g, mods, gate_idx, n_ctx, latents_only,
                            next_norm)


def kernel(x, c, ctx, c_ctx, w_mod, b_mod, g_pre_mix, g_post_mix, g_pre_ffn, g_post_ffn, w_in,
           hgrn_lb, hgrn_norm_g, diff_lambda, diff_norm_g, qk_norm_q, qk_norm_k, w_branch, w_out,
           w_ff_gate, w_ff_up, w_ff_down, w_router, b_router, w_moe_gate, w_moe_up, w_moe_down):
    n_batch, seq, d = x.shape
    n_ctx = ctx.shape[1]
    depth = w_in.shape[0]
    t = n_ctx + seq
    m = n_batch * t

    hw = HG_HEADS * HEAD
    kv_sizes = (hw, hw, hw, DF_HEADS * 2 * DF_DIM, DF_HEADS * 2 * DF_DIM,
                GQ_KV_HEADS * HEAD, GQ_KV_HEADS * HEAD)
    q_sizes = (hw, hw, DF_HEADS * 2 * DF_DIM, GQ_HEADS * HEAD, FN_GROUPS * HEAD, N_BRANCH * d)
    offs = np.concatenate([[0], np.cumsum(kv_sizes + q_sizes)]).astype(int)
    (c_ff, c_fb, c_hv, c_dfk, c_dfv, c_gqk, c_gqv,
     c_hq, c_hg, c_dfq, c_gqq, c_fu, c_gl) = [int(o) for o in offs[:-1]]

    rope_df = rope_tables(n_ctx, seq, DF_DIM)
    rope_gq = rope_tables(n_ctx, seq, HEAD)
    dft_chan = dft_matrices(HEAD)
    dft_ctx = dft_matrices(n_ctx)
    dft_lat = dft_matrices(seq)

    lb_all = jnp.cumsum(jax.nn.softmax(hgrn_lb.astype(F32), axis=0), axis=0)
    lb_all = lb_all - lb_all[:1]

    n_cond = n_batch + 1
    pad = (-n_cond) % 16
    cond = jnp.concatenate([c, c_ctx[None, :], jnp.zeros((pad, d), F32)], axis=0)
    mods_all = modulation(cond, w_mod, b_mod)[:, :n_cond].reshape(depth, n_cond, 6, d)

    h = jnp.concatenate([ctx, x], axis=1)

    (w_branch, w_out, w_ff_down, w_moe_gate, w_moe_up, w_moe_down) = [
        w.astype(BF16) for w in (w_branch, w_out, w_ff_down, w_moe_gate, w_moe_up, w_moe_down)]

    for l in range(depth):
        mods = mods_all[l]
        lam_init = 0.8 - 0.6 * math.exp(-0.3 * l)
        lv = diff_lambda[l].astype(F32)
        lam = jnp.exp(jnp.sum(lv[0] * lv[1])) - jnp.exp(jnp.sum(lv[2] * lv[3])) + lam_init

        if l == 0:
            a = norm_mod(h, g_pre_mix[l], mods, 0, 1, n_ctx)
        p2 = matmul(a.reshape(m, d), w_in, l, BF16, tm=1024, tn=1536)
        p = p2.reshape(n_batch, t, -1)

        y_hg = hgrn_scan(p, (c_hq, c_ff, c_fb, c_hv, c_hg), lb_all[l], hgrn_norm_g[l], n_ctx)

        df_q1, df_q2, df_k, gq_q, gq_k = qk_prep(p, (c_dfq, c_dfk, c_gqq, c_gqk), rope_df, rope_gq,
                                                 qk_norm_q[l], qk_norm_k[l])
        y_df = attention(df_q1, df_k, p, c_dfv, DF_HEADS, n_ctx, q2=df_q2, lam=lam,
                         g=diff_norm_g[l], post_scale=1.0 - lam_init)

        y_fn = fourier_mix(p, c_fu, n_ctx, dft_chan, dft_ctx, dft_lat)

        assert GQ_HEADS == 2 * GQ_KV_HEADS
        y_gq = attention(gq_q, gq_k, p, c_gqv, GQ_KV_HEADS, n_ctx)

        merged = merge((y_hg, y_df, y_fn, y_gq), p2, c_gl, w_branch, l)
        dense = l % 2 == 0
        h, f_in = matmul_residual(merged.reshape(n_batch, t, d), w_out, l, h, g_post_mix[l], mods, 2,
                                  n_ctx, g_pre_ffn[l], 3, 4, BF16 if dense else F32)
        f_in = f_in.reshape(m, d)

        j = l // 2
        last = l == depth - 1
        next_norm = None if last else (g_pre_mix[l + 1], mods_all[l + 1], 0, 1)
        if dense:
            hidden = swiglu_up(f_in, w_ff_gate, w_ff_up, j)
            y = matmul(hidden, w_ff_down, j, F32, tm=1024, tn=512)
            out = residual(h, y, g_post_ffn[l], mods, 5, n_ctx, last, next_norm)
        else:
            out = moe_ffn_residual(h, f_in, w_router[j], b_router[j], w_moe_gate, w_moe_up,
                                   w_moe_down, j, g_post_ffn[l], mods, 5, n_ctx, last, next_norm)
        h, a = (out, None) if last else out

    return h
```

```python
import functools
import math

import jax
import jax.numpy as jnp
import numpy as np
from jax import lax
from jax.experimental import pallas as pl
from jax.experimental.pallas import tpu as pltpu

F32 = jnp.float32
BF16 = jnp.bfloat16

EPS = 1e-6
GRID_W = 64
ROPE_THETA = 10000.0
HEAD = 128
HG_HEADS = 4
GLA_CHUNK = 32
DF_HEADS = 4
DF_DIM = 64
FN_GROUPS = 4
GQ_HEADS = 4
GQ_KV_HEADS = 2
N_BRANCH = 4
BRANCH_W = 512
TOP_K = 2

V7X_VMEM_LIMIT_BYTES = 56 * 1024 * 1024
SCAN_BLOCK = 256
SCAN_HEADS_PER_STEP = 2


def _params(*sem):
    return pltpu.CompilerParams(dimension_semantics=sem, vmem_limit_bytes=V7X_VMEM_LIMIT_BYTES)


def _tile(n, pref, align=128):
    if n <= pref:
        return n
    t = pref - pref % align
    while n % t:
        t -= align
    return t


def _mm_kernel(a_ref, w_ref, o_ref):
    o_ref[...] = jnp.dot(a_ref[...], w_ref[...], preferred_element_type=F32).astype(o_ref.dtype)


def _mm_f32w_kernel(a_ref, w_ref, o_ref, wb_ref):
    @pl.when(pl.program_id(1) == 0)
    def _():
        wb_ref[...] = w_ref[...].astype(BF16)

    o_ref[...] = jnp.dot(a_ref[...], wb_ref[...], preferred_element_type=F32).astype(o_ref.dtype)


def matmul(a, w, layer, out_dtype, tm=1024, tn=1024):
    m, k = a.shape
    n = w.shape[2]
    tm, tn = _tile(m, tm), _tile(n, tn)
    if w.dtype == BF16:
        return pl.pallas_call(
            _mm_kernel,
            grid=(m // tm, n // tn),
            in_specs=[pl.BlockSpec((tm, k), lambda i, j: (i, 0)),
                      pl.BlockSpec((None, k, tn), lambda i, j: (layer, 0, j))],
            out_specs=pl.BlockSpec((tm, tn), lambda i, j: (i, j)),
            out_shape=jax.ShapeDtypeStruct((m, n), out_dtype),
            compiler_params=_params("parallel", "parallel"),
            name="matmul",
        )(a, w)
    return pl.pallas_call(
        _mm_f32w_kernel,
        grid=(n // tn, m // tm),
        in_specs=[pl.BlockSpec((tm, k), lambda j, i: (i, 0)),
                  pl.BlockSpec((None, k, tn), lambda j, i: (layer, 0, j))],
        out_specs=pl.BlockSpec((tm, tn), lambda j, i: (i, j)),
        out_shape=jax.ShapeDtypeStruct((m, n), out_dtype),
        scratch_shapes=[pltpu.VMEM((k, tn), BF16)],
        compiler_params=_params("parallel", "arbitrary"),
        name="matmul_f32w",
    )(a, w)


def _mod_kernel(s_ref, w_ref, b_ref, o_ref):
    s = s_ref[...]
    s = (s * jax.nn.sigmoid(s)).astype(BF16)
    o_ref[0] = jnp.dot(s, w_ref[0].astype(BF16), preferred_element_type=F32) + b_ref[0]


def modulation(cond, w_mod, b_mod):
    depth, d, width = w_mod.shape
    r = cond.shape[0]
    tn = _tile(width, 1024)
    return pl.pallas_call(
        _mod_kernel,
        grid=(depth, width // tn),
        in_specs=[pl.BlockSpec((r, d), lambda l, j: (0, 0)),
                  pl.BlockSpec((1, d, tn), lambda l, j: (l, 0, j)),
                  pl.BlockSpec((1, 1, tn), lambda l, j: (l, 0, j))],
        out_specs=pl.BlockSpec((1, r, tn), lambda l, j: (l, 0, j)),
        out_shape=jax.ShapeDtypeStruct((depth, r, width), F32),
        compiler_params=_params("parallel", "parallel"),
        name="modulation",
    )(cond, w_mod, b_mod.reshape(depth, 1, width))


def _rms(x):
    return x * lax.rsqrt(jnp.mean(x * x, axis=-1, keepdims=True) + EPS)


def _norm_mod_kernel(h_ref, g_ref, mod_ref, o_ref, *, shift_idx, scale_idx):
    y = _rms(h_ref[0]) * g_ref[...]
    shift = mod_ref[0, shift_idx:shift_idx + 1, :]
    scale = mod_ref[0, scale_idx:scale_idx + 1, :]
    o_ref[0] = (y * (1.0 + scale) + shift).astype(o_ref.dtype)


def _mod_index(n_ctx_tiles, n_batch):
    return lambda b, t: (jnp.where(t < n_ctx_tiles, n_batch, b), 0, 0)


def norm_mod(h, g, mods, shift_idx, scale_idx, n_ctx, out_dtype=BF16):
    b, t, d = h.shape
    tt = _tile(n_ctx, 256)
    return pl.pallas_call(
        functools.partial(_norm_mod_kernel, shift_idx=shift_idx, scale_idx=scale_idx),
        grid=(b, t // tt),
        in_specs=[pl.BlockSpec((1, tt, d), lambda i, j: (i, j, 0)),
                  pl.BlockSpec((1, d), lambda i, j: (0, 0)),
                  pl.BlockSpec((1, 6, d), _mod_index(n_ctx // tt, b))],
        out_specs=pl.BlockSpec((1, tt, d), lambda i, j: (i, j, 0)),
        out_shape=jax.ShapeDtypeStruct((b, t, d), out_dtype),
        compiler_params=_params("parallel", "parallel"),
        name="norm_mod",
    )(h, g.reshape(1, d), mods)


def _modulated_norm(x, g_ref, mod_ref, shift_idx, scale_idx):
    scale = mod_ref[0, scale_idx:scale_idx + 1, :]
    shift = mod_ref[0, shift_idx:shift_idx + 1, :]
    return _rms(x) * g_ref[...] * (1.0 + scale) + shift


def _resid_kernel(h_ref, y_ref, g_ref, mod_ref, *rest, gate_idx, then_norm):
    gate = mod_ref[0, gate_idx:gate_idx + 1, :]
    new = h_ref[0] + gate * (_rms(y_ref[0]) * g_ref[...])
    if then_norm:
        g2_ref, mod2_ref, o_ref, a_ref = rest
        a_ref[0] = _modulated_norm(new, g2_ref, mod2_ref, *then_norm).astype(a_ref.dtype)
    else:
        (o_ref,) = rest
    o_ref[0] = new


def residual(h, y, g, mods, gate_idx, n_ctx, latents_only=False, next_norm=None):
    b, t, d = h.shape
    tt = _tile(n_ctx, 256)
    out_index, out_shape, in_place = _stream_out(b, t, d, tt, n_ctx, latents_only)
    row = pl.BlockSpec((1, tt, d), lambda i, j: (i, j, 0))
    vec = pl.BlockSpec((1, d), lambda i, j: (0, 0))
    mod = pl.BlockSpec((1, 6, d), _mod_index(n_ctx // tt, b))
    in_specs, args = [row, row, vec, mod], [h, y.reshape(b, t, d), g.reshape(1, d), mods]
    out_specs, out_shapes = pl.BlockSpec((1, tt, d), out_index), out_shape
    if next_norm is not None:
        g2, mods2 = next_norm[:2]
        in_specs, args = in_specs + [vec, mod], args + [g2.reshape(1, d), mods2]
        out_specs, out_shapes = [out_specs, row], [out_shape, jax.ShapeDtypeStruct((b, t, d), BF16)]
    return pl.pallas_call(
        functools.partial(_resid_kernel, gate_idx=gate_idx,
                          then_norm=None if next_norm is None else tuple(next_norm[2:])),
        grid=(b, t // tt),
        in_specs=in_specs,
        out_specs=out_specs,
        out_shape=out_shapes,
        input_output_aliases={0: 0} if in_place else {},
        compiler_params=_params("parallel", "arbitrary"),
        name="residual",
    )(*args)


def _mm_resid_kernel(a_ref, w_ref, h_ref, g_ref, g2_ref, mod_ref, modc_ref, o_ref, f_ref, *,
                     gate_idx, shift_idx, scale_idx, n_ctx):
    tt = a_ref.shape[1]
    y = jnp.dot(a_ref[0], w_ref[...].astype(BF16), preferred_element_type=F32)
    is_ctx = pl.program_id(1) * tt + lax.broadcasted_iota(jnp.int32, (tt, 1), 0) < n_ctx

    def pick(idx):
        return jnp.where(is_ctx, modc_ref[0, idx:idx + 1, :], mod_ref[0, idx:idx + 1, :])

    new = h_ref[0] + pick(gate_idx) * (_rms(y) * g_ref[...])
    o_ref[0] = new
    f_ref[0] = (_rms(new) * g2_ref[...] * (1.0 + pick(scale_idx)) + pick(shift_idx)).astype(f_ref.dtype)


def matmul_residual(a, w, layer, h, g, mods, gate_idx, n_ctx, g2, shift_idx, scale_idx, norm_dtype):
    b, t, d = h.shape
    k = a.shape[-1]
    tt = _tile(t, 272, 16)
    row = pl.BlockSpec((1, tt, d), lambda i, j: (i, j, 0))
    vec = pl.BlockSpec((1, d), lambda i, j: (0, 0))
    return pl.pallas_call(
        functools.partial(_mm_resid_kernel, gate_idx=gate_idx, shift_idx=shift_idx,
                          scale_idx=scale_idx, n_ctx=n_ctx),
        grid=(b, t // tt),
        in_specs=[pl.BlockSpec((1, tt, k), lambda i, j: (i, j, 0)),
                  pl.BlockSpec((None, k, d), lambda i, j: (layer, 0, 0)),
                  row, vec, vec,
                  pl.BlockSpec((1, 6, d), lambda i, j: (i, 0, 0)),
                  pl.BlockSpec((1, 6, d), lambda i, j: (b, 0, 0))],
        out_specs=[row, row],
        out_shape=[jax.ShapeDtypeStruct((b, t, d), F32), jax.ShapeDtypeStruct((b, t, d), norm_dtype)],
        input_output_aliases={2: 0},
        compiler_params=_params("parallel", "parallel"),
        name="matmul_residual",
    )(a, w, h, g.reshape(1, d), g2.reshape(1, d), mods, mods)


def rope_tables(n_ctx, seq, dim):
    quarter = dim // 4
    inv = ROPE_THETA ** (-jnp.arange(quarter, dtype=F32) / quarter)
    pos = jnp.arange(seq)
    ar = (pos // GRID_W).astype(F32)[:, None] * inv
    ac = (pos % GRID_W).astype(F32)[:, None] * inv
    ang = jnp.concatenate([ar, ar, ac, ac], axis=-1)
    ang = jnp.tile(ang, (1, HEAD // dim))
    ang = jnp.concatenate([jnp.zeros((n_ctx, HEAD), F32), ang], axis=0)
    first = ((jnp.arange(HEAD) % dim) % (dim // 2)) < quarter
    cos, sin = jnp.cos(ang), jnp.sin(ang)
    return cos, jnp.where(first, -sin, 0.0), jnp.where(first, 0.0, sin)


def _rope(x, cos_ref, sa_ref, sb_ref, quarter):
    return (x * cos_ref[...] + pltpu.roll(x, HEAD - quarter, 1) * sa_ref[...]
            + pltpu.roll(x, quarter, 1) * sb_ref[...])


def _qk_prep_kernel(dfq_ref, dfk_ref, gqq_ref, gqk_ref, dcos, dsa, dsb, gcos, gsa, gsb, gq_ref, gk_ref,
                    q1_ref, q2_ref, dk_ref, gqo_ref, gko_ref):
    lane = lax.broadcasted_iota(jnp.int32, dcos.shape, 1)
    for h in range(DF_HEADS):
        sl = slice(h * HEAD, (h + 1) * HEAD)
        y = _rope(dfq_ref[0, :, sl].astype(F32), dcos, dsa, dsb, DF_DIM // 4) * DF_DIM ** -0.5
        q1_ref[0, :, sl] = jnp.where(lane < DF_DIM, y, 0.0).astype(BF16)
        q2_ref[0, :, sl] = jnp.where(lane >= DF_DIM, y, 0.0).astype(BF16)
        dk_ref[0, :, sl] = _rope(dfk_ref[0, :, sl].astype(F32), dcos, dsa, dsb,
                                 DF_DIM // 4).astype(BF16)
    for h in range(GQ_HEADS):
        sl = slice(h * HEAD, (h + 1) * HEAD)
        x = _rms(gqq_ref[0, :, sl].astype(F32)) * gq_ref[...]
        gqo_ref[0, :, sl] = (_rope(x, gcos, gsa, gsb, HEAD // 4) * HEAD ** -0.5).astype(BF16)
    for h in range(GQ_KV_HEADS):
        sl = slice(h * HEAD, (h + 1) * HEAD)
        x = _rms(gqk_ref[0, :, sl].astype(F32)) * gk_ref[...]
        gko_ref[0, :, sl] = _rope(x, gcos, gsa, gsb, HEAD // 4).astype(BF16)


def qk_prep(p, cols, rope_df, rope_gq, g_q, g_k):
    b, t, _ = p.shape
    tt = _tile(t, 1088, 16)
    widths = (DF_HEADS * HEAD, DF_HEADS * HEAD, GQ_HEADS * HEAD, GQ_KV_HEADS * HEAD)
    assert all(c % w == 0 for c, w in zip(cols, widths))

    def col_spec(c0, w):
        return pl.BlockSpec((1, tt, w), lambda i, j: (i, j, c0 // w))

    def out_spec(w):
        return pl.BlockSpec((1, tt, w), lambda i, j: (i, j, 0))

    tab = pl.BlockSpec((tt, HEAD), lambda i, j: (j, 0))
    vec = pl.BlockSpec((1, HEAD), lambda i, j: (0, 0))
    out_w = (widths[0], widths[0], widths[1], widths[2], widths[3])
    return pl.pallas_call(
        _qk_prep_kernel,
        grid=(b, t // tt),
        in_specs=[col_spec(c, w) for c, w in zip(cols, widths)] + [tab] * 6 + [vec, vec],
        out_specs=[out_spec(w) for w in out_w],
        out_shape=[jax.ShapeDtypeStruct((b, t, w), BF16) for w in out_w],
        compiler_params=_params("parallel", "parallel"),
        name="qk_prep",
    )(p, p, p, p, *rope_df, *rope_gq, g_q.reshape(1, HEAD), g_k.reshape(1, HEAD))


ATTN_Q_TILE = 512
ATTN_K_CHUNK = 512


def _attend(q, k_ref, vx_ref, nk):
    kc = ATTN_K_CHUNK
    m = acc = None
    for c0 in range(0, nk, kc):
        c1 = min(c0 + kc, nk)
        s = lax.dot_general(q, k_ref[0, c0:c1, :], (((1,), (1,)), ((), ())),
                            preferred_element_type=F32)
        m_c = jnp.max(s, axis=-1, keepdims=True)
        m_new = m_c if m is None else jnp.maximum(m, m_c)
        pv = jnp.dot(jnp.exp((s - m_new).astype(BF16)), vx_ref[c0:c1, :],
                     preferred_element_type=F32)
        acc = pv if acc is None else acc * jnp.exp(m - m_new) + pv
        m = m_new
    return acc[:, :HEAD] / acc[:, HEAD:]


def _attn_kernel(*refs, diff, n_ctx, tq, post_scale):
    if diff:
        lam_ref, qa_ref, qb_ref, k_ref, v_ref, g_ref, o_ref, vx_ref = refs
    else:
        qa_ref, k_ref, v_ref, o_ref, vx_ref = refs
    n_keys = k_ref.shape[1]

    def rows(r0, n, nk):
        if diff:
            qa, qb = qa_ref[0, pl.ds(r0, n), :], qb_ref[0, pl.ds(r0, n), :]
        else:
            pair = qa_ref[0, pl.ds(r0, n), :]
            qa, qb = pair[:, :HEAD], pair[:, HEAD:]
        oa = _attend(qa, k_ref, vx_ref, nk)
        ob = _attend(qb, k_ref, vx_ref, nk)
        if diff:
            o = oa - lam_ref[...] * ob
            o = _rms(o) * g_ref[...] * post_scale
        else:
            o = jnp.concatenate([oa, ob], axis=1)
        o_ref[0, pl.ds(r0, n), :] = o.astype(o_ref.dtype)

    j = pl.program_id(2)

    @pl.when(j == 0)
    def _():
        vx_ref[:, :HEAD] = v_ref[0]
        vx_ref[:, HEAD:] = jnp.ones((n_keys, HEAD), BF16)
        rows(0, n_ctx, n_ctx)

    rows(pl.multiple_of(n_ctx + j * tq, math.gcd(n_ctx, tq)), tq, n_keys)


def attention(q, k, v, v_col0, n_groups, n_ctx, q2=None, lam=None, g=None, post_scale=1.0):
    b, t, _ = k.shape
    tq = _tile(t - n_ctx, ATTN_Q_TILE, 16)
    vc = v_col0 // HEAD
    diff = q2 is not None
    qw = HEAD if diff else 2 * HEAD
    qspec = pl.BlockSpec((1, t, qw), lambda i, h, j: (i, 0, h))
    kspec = pl.BlockSpec((1, t, HEAD), lambda i, h, j: (i, 0, h))
    vspec = pl.BlockSpec((1, t, HEAD), lambda i, h, j: (i, 0, vc + h))
    if diff:
        one = pl.BlockSpec((1, 1), lambda i, h, j: (0, 0))
        in_specs = [one, qspec, qspec, kspec, vspec, pl.BlockSpec((1, HEAD), lambda i, h, j: (0, 0))]
        args = (lam.reshape(1, 1).astype(F32), q, q2, k, v, g.reshape(1, HEAD))
    else:
        in_specs = [qspec, kspec, vspec]
        args = (q, k, v)
    return pl.pallas_call(
        functools.partial(_attn_kernel, diff=diff, n_ctx=n_ctx, tq=tq, post_scale=post_scale),
        grid=(b, n_groups, (t - n_ctx) // tq),
        in_specs=in_specs,
        out_specs=qspec,
        out_shape=jax.ShapeDtypeStruct((b, t, n_groups * qw), BF16),
        scratch_shapes=[pltpu.VMEM((t, 2 * HEAD), BF16)],
        compiler_params=_params("parallel", "parallel", "arbitrary"),
        name="diff_attention" if diff else "gq_attention",
    )(*args)


def _chunk_prefix(x, row_in_chunk, c):
    s = 1
    while s < c:
        x = x + jnp.where(row_in_chunk >= s, pltpu.roll(x, s, 0), 0.0)
        s *= 2
    return x


def _scan_kernel(q_ref, zf_ref, zb_ref, v_ref, hg_ref, lb_ref, g_ref, o_ref,
                 of_ref, ob_ref, sf_ref, sb_ref, *, n_ctx_blocks):
    rows = SCAN_BLOCK
    c = GLA_CHUNK
    n_chunks = rows // c
    n_blocks = q_ref.shape[1] // rows
    half = c // 2

    r_i = lax.broadcasted_iota(jnp.int32, (rows, rows), 0)
    s_i = lax.broadcasted_iota(jnp.int32, (rows, rows), 1)
    shift = c.bit_length() - 1
    same = (r_i >> shift) == (s_i >> shift)
    mask_f = same & (s_i <= r_i)
    mask_b = same & (s_i >= r_i)
    row_in_chunk = lax.broadcasted_iota(jnp.int32, (rows, HEAD), 0) & (c - 1)

    def chunk_row(x, row):
        return x.reshape(n_chunks, c, HEAD)[:, row:row + 1, :]

    def spread(x3):
        return jnp.broadcast_to(x3, (n_chunks, c, HEAD)).reshape(rows, HEAD)

    def block(blk, hd, z_ref, lb, s_ref, out_ref, mask, reverse):
        r0 = pl.multiple_of(blk * rows, rows)
        z = z_ref[0, pl.ds(r0, rows), hd].astype(F32)
        q = q_ref[0, pl.ds(r0, rows), hd].astype(F32)
        v = v_ref[0, pl.ds(r0, rows), hd]
        kk = (1.0 - lb) * jax.nn.sigmoid(-z)
        lf = jnp.log(1.0 - kk)
        pre = _chunk_prefix(lf, row_in_chunk, c)
        end3 = chunk_row(pre, c - 1)
        if reverse:
            cum = spread(end3) - pre + lf
            mid3 = chunk_row(cum, half)
        else:
            cum = pre
            mid3 = chunk_row(cum, half - 1)
        mid, end = spread(mid3), spread(end3)
        qa = (q * jnp.exp(cum - mid)).astype(BF16)
        ka = (kk * jnp.exp(mid - cum)).astype(BF16)
        att = lax.dot_general(qa, ka, (((1,), (1,)), ((), ())), preferred_element_type=F32)
        att = jnp.where(mask, att, 0.0).astype(BF16)
        o_intra = jnp.dot(att, v, preferred_element_type=F32)
        q_dec = (q * jnp.exp(cum)).astype(BF16)
        k_dec = (kk * jnp.exp(end - cum)).astype(BF16)
        dec = jnp.exp(end3)
        st = s_ref[...]
        outs = [None] * n_chunks
        order = range(n_chunks - 1, -1, -1) if reverse else range(n_chunks)
        for n in order:
            sl = slice(n * c, (n + 1) * c)
            outs[n] = lax.dot_general(q_dec[sl], st.astype(BF16), (((1,), (1,)), ((), ())),
                                      preferred_element_type=F32)
            upd = lax.dot_general(v[sl], k_dec[sl], (((0,), (0,)), ((), ())),
                                  preferred_element_type=F32)
            st = st * dec[n] + upd
        s_ref[...] = st
        out_ref[pl.ds(r0, rows), hd] = o_intra + jnp.concatenate(outs, axis=0)

    sf_ref[...] = jnp.zeros_like(sf_ref)
    sb_ref[...] = jnp.zeros_like(sb_ref)
    heads = [slice(k * HEAD, (k + 1) * HEAD) for k in range(q_ref.shape[2] // HEAD)]

    def step(i, carry):
        j = jnp.where(i < n_ctx_blocks, n_ctx_blocks - 1 - i, n_blocks - 1 - i + n_ctx_blocks)
        for k, hd in enumerate(heads):
            block(i, hd, zf_ref, lb_ref[0:1, hd], sf_ref.at[k], of_ref, mask_f, False)
            block(j, hd, zb_ref, lb_ref[1:2, hd], sb_ref.at[k], ob_ref, mask_b, True)
        return carry

    lax.fori_loop(0, n_blocks, step, 0)

    def finish(i, carry):
        r0 = pl.multiple_of(i * rows, rows)
        for hd in heads:
            o = of_ref[pl.ds(r0, rows), hd] + ob_ref[pl.ds(r0, rows), hd]
            gate = hg_ref[0, pl.ds(r0, rows), hd].astype(F32)
            y = _rms(o) * g_ref[...] * (gate * jax.nn.sigmoid(gate))
            o_ref[0, pl.ds(r0, rows), hd] = y.astype(o_ref.dtype)
        return carry

    lax.fori_loop(0, n_blocks, finish, 0)


def hgrn_scan(p, cols, lb, g, n_ctx):
    b, t, _ = p.shape
    assert t % SCAN_BLOCK == 0 and n_ctx % SCAN_BLOCK == 0
    hp = SCAN_HEADS_PER_STEP
    w = hp * HEAD
    assert HG_HEADS % hp == 0 and all(c0 % w == 0 for c0 in cols)

    def col(c0):
        return pl.BlockSpec((1, t, w), lambda i, h: (i, 0, c0 // w + h))

    return pl.pallas_call(
        functools.partial(_scan_kernel, n_ctx_blocks=n_ctx // SCAN_BLOCK),
        grid=(b, HG_HEADS // hp),
        in_specs=[col(c0) for c0 in cols] + [pl.BlockSpec((2, w), lambda i, h: (0, h)),
                                             pl.BlockSpec((1, HEAD), lambda i, h: (0, 0))],
        out_specs=pl.BlockSpec((1, t, w), lambda i, h: (i, 0, h)),
        out_shape=jax.ShapeDtypeStruct((b, t, HG_HEADS * HEAD), BF16),
        scratch_shapes=[pltpu.VMEM((t, w), F32), pltpu.VMEM((t, w), F32),
                        pltpu.VMEM((hp, HEAD, HEAD), F32), pltpu.VMEM((hp, HEAD, HEAD), F32)],
        compiler_params=_params("parallel", "parallel"),
        name="hgrn_scan",
    )(p, p, p, p, p, lb, g.reshape(1, HEAD))


def dft_matrices(n):
    r = 1 << ((n.bit_length() - 1) // 2) if n & (n - 1) == 0 else 1
    c = n // r
    k = jnp.arange(n, dtype=jnp.int32)[:, None]

    def table(cols):
        ang = ((k * cols[None, :]) % n).astype(F32) * (2.0 * math.pi / n)
        return jnp.cos(ang), jnp.sin(ang)

    (ca, sa), (cb, sb) = table(jnp.arange(r, dtype=jnp.int32) * c), table(jnp.arange(c, dtype=jnp.int32))
    ca, sa, cb, sb = ca[:, :, None], sa[:, :, None], cb[:, None, :], sb[:, None, :]
    s = 1.0 / math.sqrt(n)
    cos = ((ca * cb - sa * sb) * s).reshape(n, n)
    sin = ((sa * cb + ca * sb) * s).reshape(n, n)
    return cos.astype(BF16), sin.astype(BF16)


def _fn_channel_kernel(u_ref, cc_ref, sc_ref, a_ref, b_ref):
    for grp in range(FN_GROUPS):
        sl = slice(grp * HEAD, (grp + 1) * HEAD)
        u = u_ref[0, :, sl]
        a_ref[0, :, sl] = jnp.dot(u, cc_ref[...], preferred_element_type=F32).astype(BF16)
        b_ref[0, :, sl] = jnp.dot(u, sc_ref[...], preferred_element_type=F32).astype(BF16)


def _fn_position_kernel(a_ref, b_ref, cl_ref, sl_ref, cx_ref, sx_ref, o_ref, *, n_ctx):
    i = pl.program_id(1)

    @pl.when(i == 0)
    def _():
        a = a_ref[0, :n_ctx, :]
        b = b_ref[0, :n_ctx, :]
        o_ref[0] = (jnp.dot(cx_ref[...], a, preferred_element_type=F32)
                    - jnp.dot(sx_ref[...], b, preferred_element_type=F32)).astype(o_ref.dtype)

    @pl.when(i > 0)
    def _():
        a = a_ref[0, n_ctx:, :]
        b = b_ref[0, n_ctx:, :]
        o_ref[0] = (jnp.dot(cl_ref[...], a, preferred_element_type=F32)
                    - jnp.dot(sl_ref[...], b, preferred_element_type=F32)).astype(o_ref.dtype)


def fourier_mix(p, col0, n_ctx, dft_chan, dft_ctx, dft_lat):
    b, t, _ = p.shape
    w = FN_GROUPS * HEAD
    tt = _tile(t, 1088, 16)
    assert col0 % w == 0
    blk = pl.BlockSpec((1, tt, w), lambda i, j: (i, j, 0))
    mat = pl.BlockSpec((HEAD, HEAD), lambda i, j: (0, 0))
    ua, ub = pl.pallas_call(
        _fn_channel_kernel,
        grid=(b, t // tt),
        in_specs=[pl.BlockSpec((1, tt, w), lambda i, j: (i, j, col0 // w)), mat, mat],
        out_specs=[blk, blk],
        out_shape=[jax.ShapeDtypeStruct((b, t, w), BF16)] * 2,
        compiler_params=_params("parallel", "parallel"),
        name="fourier_channels",
    )(p, *dft_chan)
    seq = t - n_ctx
    tm = n_ctx
    assert seq % tm == 0 and tm % 8 == 0
    full = pl.BlockSpec((1, t, w), lambda i, j: (i, 0, 0))
    lat = pl.BlockSpec((tm, seq), lambda i, j: (jnp.maximum(j - 1, 0), 0))
    ctx = pl.BlockSpec((n_ctx, n_ctx), lambda i, j: (0, 0))
    return pl.pallas_call(
        functools.partial(_fn_position_kernel, n_ctx=n_ctx),
        grid=(b, 1 + seq // tm),
        in_specs=[full, full, lat, lat, ctx, ctx],
        out_specs=pl.BlockSpec((1, tm, w), lambda i, j: (i, j, 0)),
        out_shape=jax.ShapeDtypeStruct((b, t, w), BF16),
        compiler_params=_params("parallel", "parallel"),
        name="fourier_positions",
    )(ua, ub, *dft_lat, *dft_ctx)


def _merge_kernel(*refs):
    ys, gs, ws, o_ref = refs[:N_BRANCH], refs[N_BRANCH:2 * N_BRANCH], refs[2 * N_BRANCH:-1], refs[-1]
    acc = None
    for y_ref, g_ref, w_ref in zip(ys, gs, ws):
        term = jax.nn.sigmoid(g_ref[...].astype(F32)) * jnp.dot(
            y_ref[...], w_ref[0].astype(BF16), preferred_element_type=F32)
        acc = term if acc is None else acc + term
    o_ref[...] = acc.astype(o_ref.dtype)


def merge(branches, p2, gate_col0, w_branch, layer):
    m = p2.shape[0]
    d = w_branch.shape[-1]
    tm, tn = _tile(m, 1024), _tile(d, 512)
    y_specs = [pl.BlockSpec((tm, BRANCH_W), lambda i, j: (i, 0))] * N_BRANCH
    g_specs = [pl.BlockSpec((tm, tn), functools.partial(
        lambda i, j, off: (i, off + j), off=(gate_col0 + k * d) // tn)) for k in range(N_BRANCH)]
    w_specs = [pl.BlockSpec((None, 1, BRANCH_W, tn),
                            functools.partial(lambda i, j, k: (layer, k, 0, j), k=k))
               for k in range(N_BRANCH)]
    return pl.pallas_call(
        _merge_kernel,
        grid=(m // tm, d // tn),
        in_specs=y_specs + g_specs + w_specs,
        out_specs=pl.BlockSpec((tm, tn), lambda i, j: (i, j)),
        out_shape=jax.ShapeDtypeStruct((m, d), BF16),
        compiler_params=_params("parallel", "parallel"),
        name="merge",
    )(*[y.reshape(m, BRANCH_W) for y in branches], *([p2] * N_BRANCH), *([w_branch] * N_BRANCH))


def _swiglu_kernel(a_ref, wg_ref, wu_ref, o_ref, wgb_ref, wub_ref):
    @pl.when(pl.program_id(1) == 0)
    def _():
        wgb_ref[...] = wg_ref[...].astype(BF16)
        wub_ref[...] = wu_ref[...].astype(BF16)

    a = a_ref[...]
    gate = jnp.dot(a, wgb_ref[...], preferred_element_type=F32)
    up = jnp.dot(a, wub_ref[...], preferred_element_type=F32)
    o_ref[...] = (gate * jax.nn.sigmoid(gate) * up).astype(BF16)


def swiglu_up(a, wg, wu, layer):
    m, d = a.shape
    f = wg.shape[2]
    tm, tn = _tile(m, 1024), _tile(f, 512)
    wspec = pl.BlockSpec((None, d, tn), lambda j, i: (layer, 0, j))
    return pl.pallas_call(
        _swiglu_kernel,
        grid=(f // tn, m // tm),
        in_specs=[pl.BlockSpec((tm, d), lambda j, i: (i, 0)), wspec, wspec],
        out_specs=pl.BlockSpec((tm, tn), lambda j, i: (i, j)),
        out_shape=jax.ShapeDtypeStruct((m, f), BF16),
        scratch_shapes=[pltpu.VMEM((d, tn), BF16), pltpu.VMEM((d, tn), BF16)],
        compiler_params=_params("parallel", "arbitrary"),
        name="swiglu_up",
    )(a, wg, wu)


MOE_ROW_TILE = 512
MOE_INFO_ROWS = 8


def _router_kernel(a_ref, w_ref, b_ref, info_ref, cnt_ref, carry_ref):
    @pl.when(pl.program_id(0) == 0)
    def _():
        carry_ref[...] = jnp.zeros_like(carry_ref)

    logits = lax.dot_general(w_ref[...], a_ref[...].astype(BF16), (((1,), (1,)), ((), ())),
                             preferred_element_type=F32) + b_ref[...]
    n_exp, tm = logits.shape
    e = lax.broadcasted_iota(jnp.int32, logits.shape, 0).astype(F32)
    m1 = jnp.max(logits, axis=0, keepdims=True)
    i1 = jnp.min(jnp.where(logits == m1, e, float(n_exp)), axis=0, keepdims=True)
    rest = jnp.where(e == i1, -jnp.inf, logits)
    m2 = jnp.max(rest, axis=0, keepdims=True)
    i2 = jnp.min(jnp.where(rest == m2, e, float(n_exp)), axis=0, keepdims=True)
    x = jnp.exp(m2 - m1)
    w1 = 1.0 / (1.0 + x)
    w2 = x * w1

    chosen = jnp.where((e == i1) | (e == i2), 1.0, 0.0)
    s_i = lax.broadcasted_iota(jnp.int32, (tm, tm), 0)
    t_i = lax.broadcasted_iota(jnp.int32, (tm, tm), 1)
    earlier = (s_i < t_i).astype(BF16)
    rank = jnp.dot(chosen.astype(BF16), earlier, preferred_element_type=F32) + carry_ref[:, 0:1]
    r1 = jnp.sum(jnp.where(e == i1, rank, 0.0), axis=0, keepdims=True)
    r2 = jnp.sum(jnp.where(e == i2, rank, 0.0), axis=0, keepdims=True)
    carry_ref[...] = carry_ref[...] + jnp.sum(chosen, axis=1, keepdims=True)
    cnt_ref[...] = carry_ref[...]

    row = lax.broadcasted_iota(jnp.int32, (MOE_INFO_ROWS, tm), 0)
    info = jnp.zeros((MOE_INFO_ROWS, tm), F32)
    for k, val in enumerate((i1, i2, w1, w2, r1, r2)):
        info = jnp.where(row == k, val, info)
    info_ref[...] = info


def route(a, w_router, b_router):
    m, d = a.shape
    n_exp = w_router.shape[1]
    tm = _tile(m, 1024)
    return pl.pallas_call(
        _router_kernel,
        grid=(m // tm,),
        in_specs=[pl.BlockSpec((tm, d), lambda i: (i, 0)),
                  pl.BlockSpec((n_exp, d), lambda i: (0, 0)),
                  pl.BlockSpec((n_exp, 1), lambda i: (0, 0))],
        out_specs=[pl.BlockSpec((MOE_INFO_ROWS, tm), lambda i: (0, i)),
                   pl.BlockSpec((n_exp, HEAD), lambda i: (0, 0))],
        out_shape=[jax.ShapeDtypeStruct((MOE_INFO_ROWS, m), F32),
                   jax.ShapeDtypeStruct((n_exp, HEAD), F32)],
        scratch_shapes=[pltpu.VMEM((n_exp, HEAD), F32)],
        compiler_params=_params("arbitrary"),
        name="router",
    )(a, w_router.T.astype(BF16), b_router.reshape(n_exp, 1).astype(F32))


def _row_copies(src_ref, dst_ref, src_row, dst_row, sem):
    return pltpu.make_async_copy(src_ref.at[pl.ds(src_row, 1), :], dst_ref.at[pl.ds(dst_row, 1), :], sem)


def _wait_rows(src_ref, dst_ref, n_rows, sem):
    pltpu.make_async_copy(src_ref.at[pl.ds(0, n_rows), :], dst_ref.at[pl.ds(0, n_rows), :], sem).wait()


def _dispatch_kernel(d1_ref, d2_ref, pt_ref, nu_ref, x_ref, xg_ref, zero_ref, sem, zsem):
    tt = x_ref.shape[0]
    tm = zero_ref.shape[0]
    base = pl.program_id(0) * tt

    def zero_tile(tile):
        return pltpu.make_async_copy(zero_ref, xg_ref.at[pl.ds(tile * tm, tm), :], zsem)

    @pl.when(pl.program_id(0) == 0)
    def _():
        zero_ref[...] = jnp.zeros_like(zero_ref)
        for e in range(pt_ref.shape[0]):
            @pl.when(pt_ref[e] >= 0)
            def _():
                zero_tile(pt_ref[e]).start()
        for e in range(pt_ref.shape[0]):
            @pl.when(pt_ref[e] >= 0)
            def _():
                zero_tile(pt_ref[e]).wait()

        def zero_unused(tile, carry):
            zero_tile(tile).start()
            zero_tile(tile).wait()
            return carry

        lax.fori_loop(nu_ref[0], xg_ref.shape[0] // tm, zero_unused, 0)

    def send(r, carry):
        _row_copies(x_ref, xg_ref, r, d1_ref[base + r], sem).start()
        _row_copies(x_ref, xg_ref, r, d2_ref[base + r], sem).start()
        return carry

    lax.fori_loop(0, tt, send, 0, unroll=8)
    _wait_rows(x_ref, xg_ref, tt, sem)
    _wait_rows(x_ref, xg_ref, tt, sem)


def dispatch(x, dest1, dest2, pad_tile, n_used, n_rows):
    m, d = x.shape
    tt = _tile(m, 256)
    return pl.pallas_call(
        _dispatch_kernel,
        grid_spec=pltpu.PrefetchScalarGridSpec(
            num_scalar_prefetch=4,
            grid=(m // tt,),
            in_specs=[pl.BlockSpec((tt, d), lambda i, d1, d2, pt, nu: (i, 0))],
            out_specs=pl.BlockSpec(memory_space=pl.ANY),
            scratch_shapes=[pltpu.VMEM((MOE_ROW_TILE, d), F32), pltpu.SemaphoreType.DMA(()),
                            pltpu.SemaphoreType.DMA(())]),
        out_shape=jax.ShapeDtypeStruct((n_rows, d), F32),
        compiler_params=_params("arbitrary"),
        name="moe_dispatch",
    )(dest1, dest2, pad_tile, n_used, x)


def _expert_ffn_kernel(te_ref, nu_ref, x_ref, wg_ref, wu_ref, wd_ref, o_ref):
    i, j = pl.program_id(0), pl.program_id(1)
    used = i < nu_ref[0]

    @pl.when(used)
    def _():
        a = x_ref[...].astype(BF16)
        gate = jnp.dot(a, wg_ref[0].astype(BF16), preferred_element_type=F32)
        up = jnp.dot(a, wu_ref[0].astype(BF16), preferred_element_type=F32)
        hidden = (gate * jax.nn.sigmoid(gate) * up).astype(BF16)
        part = jnp.dot(hidden, wd_ref[0].astype(BF16), preferred_element_type=F32)

        @pl.when(j == 0)
        def _():
            o_ref[...] = part

        @pl.when(j > 0)
        def _():
            o_ref[...] += part

    @pl.when(jnp.logical_not(used) & (j == 0))
    def _():
        o_ref[...] = jnp.zeros_like(o_ref)


def expert_ffn(xg, tile_expert, n_used, wg, wu, wd, layer):
    r, d = xg.shape
    f = wg.shape[-1]
    tm, tf = MOE_ROW_TILE, _tile(f, 1024)
    nf = f // tf

    def hidden_block(i, j, nu):
        return jnp.where(i < nu[0], j, nf - 1)

    def up_index(i, j, te, nu):
        return (layer, te[i], 0, hidden_block(i, j, nu))

    return pl.pallas_call(
        _expert_ffn_kernel,
        grid_spec=pltpu.PrefetchScalarGridSpec(
            num_scalar_prefetch=2,
            grid=(r // tm, nf),
            in_specs=[pl.BlockSpec((tm, d), lambda i, j, te, nu: (jnp.where(i < nu[0], i, 0), 0)),
                      pl.BlockSpec((None, 1, d, tf), up_index),
                      pl.BlockSpec((None, 1, d, tf), up_index),
                      pl.BlockSpec((None, 1, tf, d),
                                   lambda i, j, te, nu: (layer, te[i], hidden_block(i, j, nu), 0))],
            out_specs=pl.BlockSpec((tm, d), lambda i, j, te, nu: (i, 0))),
        out_shape=jax.ShapeDtypeStruct((r, d), F32),
        compiler_params=_params("parallel", "arbitrary"),
        name="expert_ffn",
    )(tile_expert, n_used, xg, wg, wu, wd)


def _combine_kernel(d1_ref, d2_ref, h_ref, y_ref, w1_ref, w2_ref, g_ref, mod_ref, *rest,
                    gate_idx, then_norm):
    if then_norm:
        g2_ref, mod2_ref, o_ref, a_ref, y1_ref, y2_ref, sem = rest
    else:
        o_ref, y1_ref, y2_ref, sem = rest
    tt = h_ref.shape[1]
    base = (pl.program_id(0) * pl.num_programs(1) + pl.program_id(1)) * tt

    def fetch(r, carry):
        _row_copies(y_ref, y1_ref, d1_ref[base + r], r, sem).start()
        _row_copies(y_ref, y2_ref, d2_ref[base + r], r, sem).start()
        return carry

    lax.fori_loop(0, tt, fetch, 0, unroll=8)
    _wait_rows(y_ref, y1_ref, tt, sem)
    _wait_rows(y_ref, y2_ref, tt, sem)
    y = w1_ref[...] * y1_ref[...] + w2_ref[...] * y2_ref[...]
    gate = mod_ref[0, gate_idx:gate_idx + 1, :]
    new = h_ref[0] + gate * (_rms(y) * g_ref[...])
    o_ref[0] = new
    if then_norm:
        a_ref[0] = _modulated_norm(new, g2_ref, mod2_ref, *then_norm).astype(a_ref.dtype)


def _stream_out(b, t, d, tt, n_ctx, latents_only):
    if not latents_only:
        return (lambda i, j, *_: (i, j, 0)), jax.ShapeDtypeStruct((b, t, d), F32), True
    skip = n_ctx // tt
    return ((lambda i, j, *_: (i, jnp.maximum(j - skip, 0), 0)),
            jax.ShapeDtypeStruct((b, t - n_ctx, d), F32), False)


def combine_residual(h, yg, dest1, dest2, w1, w2, g, mods, gate_idx, n_ctx, latents_only=False,
                     next_norm=None):
    b, t, d = h.shape
    tt = _tile(n_ctx, 256)
    nt = t // tt
    col = pl.BlockSpec((tt, 1), lambda i, j, d1, d2: (i * nt + j, 0))
    n_ctx_tiles = n_ctx // tt
    out_index, out_shape, in_place = _stream_out(b, t, d, tt, n_ctx, latents_only)
    row = pl.BlockSpec((1, tt, d), lambda i, j, d1, d2: (i, j, 0))
    vec = pl.BlockSpec((1, d), lambda i, j, d1, d2: (0, 0))
    mod = pl.BlockSpec((1, 6, d), lambda i, j, d1, d2: (jnp.where(j < n_ctx_tiles, b, i), 0, 0))
    in_specs = [row, pl.BlockSpec(memory_space=pl.ANY), col, col, vec, mod]
    args = [dest1, dest2, h, yg, w1.reshape(-1, 1), w2.reshape(-1, 1), g.reshape(1, d), mods]
    out_specs, out_shapes = pl.BlockSpec((1, tt, d), out_index), out_shape
    if next_norm is not None:
        g2, mods2 = next_norm[:2]
        in_specs, args = in_specs + [vec, mod], args + [g2.reshape(1, d), mods2]
        out_specs, out_shapes = [out_specs, row], [out_shape, jax.ShapeDtypeStruct((b, t, d), BF16)]
    return pl.pallas_call(
        functools.partial(_combine_kernel, gate_idx=gate_idx,
                          then_norm=None if next_norm is None else tuple(next_norm[2:])),
        grid_spec=pltpu.PrefetchScalarGridSpec(
            num_scalar_prefetch=2,
            grid=(b, nt),
            in_specs=in_specs,
            out_specs=out_specs,
            scratch_shapes=[pltpu.VMEM((tt, d), F32), pltpu.VMEM((tt, d), F32),
                            pltpu.SemaphoreType.DMA(())]),
        out_shape=out_shapes,
        input_output_aliases={2: 0} if in_place else {},
        compiler_params=_params("arbitrary", "arbitrary"),
        name="moe_combine",
    )(*args)


def moe_ffn_residual(h, a, w_router, b_router, wg, wu, wd, layer, g, mods, gate_idx, n_ctx,
                     latents_only=False, next_norm=None):
    m, d = a.shape
    n_exp = w_router.shape[1]
    tm = MOE_ROW_TILE
    info, counts = route(a, w_router, b_router)
    i1, i2 = info[0].astype(jnp.int32), info[1].astype(jnp.int32)
    w1, w2 = info[2], info[3]
    r1, r2 = info[4].astype(jnp.int32), info[5].astype(jnp.int32)

    counts = counts[:, 0].astype(jnp.int32)
    padded = (counts + tm - 1) // tm * tm
    ends = jnp.cumsum(padded)
    starts = ends - padded
    n_tiles = (TOP_K * m) // tm + n_exp
    tile_expert = jnp.sum(jnp.arange(n_tiles)[:, None] >= (ends // tm)[None, :], axis=1)
    tile_expert = jnp.minimum(tile_expert, n_exp - 1).astype(jnp.int32)
    n_used = (ends[-1:] // tm).astype(jnp.int32)
    dest1 = starts[i1] + r1
    dest2 = starts[i2] + r2
    pad_tile = jnp.where(counts > 0, ends // tm - 1, -1).astype(jnp.int32)

    xg = dispatch(a, dest1, dest2, pad_tile, n_used, n_tiles * tm)
    yg = expert_ffn(xg, tile_expert, n_used, wg, wu, wd, layer)
    return combine_residual(h, yg, dest1, dest2, w1, w2, g, mods, gate_idx, n_ctx, latents_only,
                            next_norm)


def kernel(x, c, ctx, c_ctx, w_mod, b_mod, g_pre_mix, g_post_mix, g_pre_ffn, g_post_ffn, w_in,
           hgrn_lb, hgrn_norm_g, diff_lambda, diff_norm_g, qk_norm_q, qk_norm_k, w_branch, w_out,
           w_ff_gate, w_ff_up, w_ff_down, w_router, b_router, w_moe_gate, w_moe_up, w_moe_down):
    n_batch, seq, d = x.shape
    n_ctx = ctx.shape[1]
    depth = w_in.shape[0]
    t = n_ctx + seq
    m = n_batch * t

    hw = HG_HEADS * HEAD
    kv_sizes = (hw, hw, hw, DF_HEADS * 2 * DF_DIM, DF_HEADS * 2 * DF_DIM,
                GQ_KV_HEADS * HEAD, GQ_KV_HEADS * HEAD)
    q_sizes = (hw, hw, DF_HEADS * 2 * DF_DIM, GQ_HEADS * HEAD, FN_GROUPS * HEAD, N_BRANCH * d)
    offs = np.concatenate([[0], np.cumsum(kv_sizes + q_sizes)]).astype(int)
    (c_ff, c_fb, c_hv, c_dfk, c_dfv, c_gqk, c_gqv,
     c_hq, c_hg, c_dfq, c_gqq, c_fu, c_gl) = [int(o) for o in offs[:-1]]

    rope_df = rope_tables(n_ctx, seq, DF_DIM)
    rope_gq = rope_tables(n_ctx, seq, HEAD)
    dft_chan = dft_matrices(HEAD)
    dft_ctx = dft_matrices(n_ctx)
    dft_lat = dft_matrices(seq)

    lb_all = jnp.cumsum(jax.nn.softmax(hgrn_lb.astype(F32), axis=0), axis=0)
    lb_all = lb_all - lb_all[:1]

    n_cond = n_batch + 1
    pad = (-n_cond) % 16
    cond = jnp.concatenate([c, c_ctx[None, :], jnp.zeros((pad, d), F32)], axis=0)
    mods_all = modulation(cond, w_mod, b_mod)[:, :n_cond].reshape(depth, n_cond, 6, d)

    h = jnp.concatenate([ctx, x], axis=1)

    (w_branch, w_out, w_ff_down, w_moe_gate, w_moe_up, w_moe_down) = [
        w.astype(BF16) for w in (w_branch, w_out, w_ff_down, w_moe_gate, w_moe_up, w_moe_down)]

    for l in range(depth):
        mods = mods_all[l]
        lam_init = 0.8 - 0.6 * math.exp(-0.3 * l)
        lv = diff_lambda[l].astype(F32)
        lam = jnp.exp(jnp.sum(lv[0] * lv[1])) - jnp.exp(jnp.sum(lv[2] * lv[3])) + lam_init

        if l == 0:
            a = norm_mod(h, g_pre_mix[l], mods, 0, 1, n_ctx)
        p2 = matmul(a.reshape(m, d), w_in, l, BF16, tm=1024, tn=1536)
        p = p2.reshape(n_batch, t, -1)

        y_hg = hgrn_scan(p, (c_hq, c_ff, c_fb, c_hv, c_hg), lb_all[l], hgrn_norm_g[l], n_ctx)

        df_q1, df_q2, df_k, gq_q, gq_k = qk_prep(p, (c_dfq, c_dfk, c_gqq, c_gqk), rope_df, rope_gq,
                                                 qk_norm_q[l], qk_norm_k[l])
        y_df = attention(df_q1, df_k, p, c_dfv, DF_HEADS, n_ctx, q2=df_q2, lam=lam,
                         g=diff_norm_g[l], post_scale=1.0 - lam_init)

        y_fn = fourier_mix(p, c_fu, n_ctx, dft_chan, dft_ctx, dft_lat)

        assert GQ_HEADS == 2 * GQ_KV_HEADS
        y_gq = attention(gq_q, gq_k, p, c_gqv, GQ_KV_HEADS, n_ctx)

        merged = merge((y_hg, y_df, y_fn, y_gq), p2, c_gl, w_branch, l)
        dense = l % 2 == 0
        h, f_in = matmul_residual(merged.reshape(n_batch, t, d), w_out, l, h, g_post_mix[l], mods, 2,
                                  n_ctx, g_pre_ffn[l], 3, 4, BF16 if dense else F32)
        f_in = f_in.reshape(m, d)

        j = l // 2
        last = l == depth - 1
        next_norm = None if last else (g_pre_mix[l + 1], mods_all[l + 1], 0, 1)
        if dense:
            hidden = swiglu_up(f_in, w_ff_gate, w_ff_up, j)
            y = matmul(hidden, w_ff_down, j, F32, tm=1024, tn=512)
            out = residual(h, y, g_post_ffn[l], mods, 5, n_ctx, last, next_norm)
        else:
            out = moe_ffn_residual(h, f_in, w_router[j], b_router[j], w_moe_gate, w_moe_up,
                                   w_moe_down, j, g_post_ffn[l], mods, 5, n_ctx, last, next_norm)
        h, a = (out, None) if last else out

    return h
```

```python
import functools
import math

import jax
import jax.numpy as jnp
import numpy as np
from jax import lax
from jax.experimental import pallas as pl
from jax.experimental.pallas import tpu as pltpu

F32 = jnp.float32
BF16 = jnp.bfloat16

EPS = 1e-6
GRID_W = 64
ROPE_THETA = 10000.0
HEAD = 128
HG_HEADS = 4
GLA_CHUNK = 32
DF_HEADS = 4
DF_DIM = 64
FN_GROUPS = 4
GQ_HEADS = 4
GQ_KV_HEADS = 2
N_BRANCH = 4
BRANCH_W = 512
TOP_K = 2

V7X_VMEM_LIMIT_BYTES = 56 * 1024 * 1024
SCAN_BLOCK = 256
SCAN_HEADS_PER_STEP = 2


def _params(*sem):
    return pltpu.CompilerParams(dimension_semantics=sem, vmem_limit_bytes=V7X_VMEM_LIMIT_BYTES)


def _tile(n, pref, align=128):
    if n <= pref:
        return n
    t = pref - pref % align
    while n % t:
        t -= align
    return t


def _mm_kernel(a_ref, w_ref, o_ref):
    o_ref[...] = jnp.dot(a_ref[...], w_ref[...], preferred_element_type=F32).astype(o_ref.dtype)


def _mm_f32w_kernel(a_ref, w_ref, o_ref, wb_ref):
    @pl.when(pl.program_id(1) == 0)
    def _():
        wb_ref[...] = w_ref[...].astype(BF16)

    o_ref[...] = jnp.dot(a_ref[...], wb_ref[...], preferred_element_type=F32).astype(o_ref.dtype)


def matmul(a, w, layer, out_dtype, tm=1024, tn=1024):
    m, k = a.shape
    n = w.shape[2]
    tm, tn = _tile(m, tm), _tile(n, tn)
    if w.dtype == BF16:
        return pl.pallas_call(
            _mm_kernel,
            grid=(m // tm, n // tn),
            in_specs=[pl.BlockSpec((tm, k), lambda i, j: (i, 0)),
                      pl.BlockSpec((None, k, tn), lambda i, j: (layer, 0, j))],
            out_specs=pl.BlockSpec((tm, tn), lambda i, j: (i, j)),
            out_shape=jax.ShapeDtypeStruct((m, n), out_dtype),
            compiler_params=_params("parallel", "parallel"),
            name="matmul",
        )(a, w)
    return pl.pallas_call(
        _mm_f32w_kernel,
        grid=(n // tn, m // tm),
        in_specs=[pl.BlockSpec((tm, k), lambda j, i: (i, 0)),
                  pl.BlockSpec((None, k, tn), lambda j, i: (layer, 0, j))],
        out_specs=pl.BlockSpec((tm, tn), lambda j, i: (i, j)),
        out_shape=jax.ShapeDtypeStruct((m, n), out_dtype),
        scratch_shapes=[pltpu.VMEM((k, tn), BF16)],
        compiler_params=_params("parallel", "arbitrary"),
        name="matmul_f32w",
    )(a, w)


def _mod_kernel(s_ref, w_ref, b_ref, o_ref):
    s = s_ref[...]
    s = (s * jax.nn.sigmoid(s)).astype(BF16)
    o_ref[0] = jnp.dot(s, w_ref[0].astype(BF16), preferred_element_type=F32) + b_ref[0]


def modulation(cond, w_mod, b_mod):
    depth, d, width = w_mod.shape
    r = cond.shape[0]
    tn = _tile(width, 1024)
    return pl.pallas_call(
        _mod_kernel,
        grid=(depth, width // tn),
        in_specs=[pl.BlockSpec((r, d), lambda l, j: (0, 0)),
                  pl.BlockSpec((1, d, tn), lambda l, j: (l, 0, j)),
                  pl.BlockSpec((1, 1, tn), lambda l, j: (l, 0, j))],
        out_specs=pl.BlockSpec((1, r, tn), lambda l, j: (l, 0, j)),
        out_shape=jax.ShapeDtypeStruct((depth, r, width), F32),
        compiler_params=_params("parallel", "parallel"),
        name="modulation",
    )(cond, w_mod, b_mod.reshape(depth, 1, width))


def _rms(x):
    return x * lax.rsqrt(jnp.mean(x * x, axis=-1, keepdims=True) + EPS)


def _norm_mod_kernel(h_ref, g_ref, mod_ref, o_ref, *, shift_idx, scale_idx):
    y = _rms(h_ref[0]) * g_ref[...]
    shift = mod_ref[0, shift_idx:shift_idx + 1, :]
    scale = mod_ref[0, scale_idx:scale_idx + 1, :]
    o_ref[0] = (y * (1.0 + scale) + shift).astype(o_ref.dtype)


def _mod_index(n_ctx_tiles, n_batch):
    return lambda b, t: (jnp.where(t < n_ctx_tiles, n_batch, b), 0, 0)


def norm_mod(h, g, mods, shift_idx, scale_idx, n_ctx, out_dtype=BF16):
    b, t, d = h.shape
    tt = _tile(n_ctx, 256)
    return pl.pallas_call(
        functools.partial(_norm_mod_kernel, shift_idx=shift_idx, scale_idx=scale_idx),
        grid=(b, t // tt),
        in_specs=[pl.BlockSpec((1, tt, d), lambda i, j: (i, j, 0)),
                  pl.BlockSpec((1, d), lambda i, j: (0, 0)),
                  pl.BlockSpec((1, 6, d), _mod_index(n_ctx // tt, b))],
        out_specs=pl.BlockSpec((1, tt, d), lambda i, j: (i, j, 0)),
        out_shape=jax.ShapeDtypeStruct((b, t, d), out_dtype),
        compiler_params=_params("parallel", "parallel"),
        name="norm_mod",
    )(h, g.reshape(1, d), mods)


def _modulated_norm(x, g_ref, mod_ref, shift_idx, scale_idx):
    scale = mod_ref[0, scale_idx:scale_idx + 1, :]
    shift = mod_ref[0, shift_idx:shift_idx + 1, :]
    return _rms(x) * g_ref[...] * (1.0 + scale) + shift


def _resid_kernel(h_ref, y_ref, g_ref, mod_ref, *rest, gate_idx, then_norm):
    gate = mod_ref[0, gate_idx:gate_idx + 1, :]
    new = h_ref[0] + gate * (_rms(y_ref[0]) * g_ref[...])
    if then_norm:
        g2_ref, mod2_ref, o_ref, a_ref = rest
        a_ref[0] = _modulated_norm(new, g2_ref, mod2_ref, *then_norm).astype(a_ref.dtype)
    else:
        (o_ref,) = rest
    o_ref[0] = new


def residual(h, y, g, mods, gate_idx, n_ctx, latents_only=False, next_norm=None):
    b, t, d = h.shape
    tt = _tile(n_ctx, 256)
    out_index, out_shape, in_place = _stream_out(b, t, d, tt, n_ctx, latents_only)
    row = pl.BlockSpec((1, tt, d), lambda i, j: (i, j, 0))
    vec = pl.BlockSpec((1, d), lambda i, j: (0, 0))
    mod = pl.BlockSpec((1, 6, d), _mod_index(n_ctx // tt, b))
    in_specs, args = [row, row, vec, mod], [h, y.reshape(b, t, d), g.reshape(1, d), mods]
    out_specs, out_shapes = pl.BlockSpec((1, tt, d), out_index), out_shape
    if next_norm is not None:
        g2, mods2 = next_norm[:2]
        in_specs, args = in_specs + [vec, mod], args + [g2.reshape(1, d), mods2]
        out_specs, out_shapes = [out_specs, row], [out_shape, jax.ShapeDtypeStruct((b, t, d), BF16)]
    return pl.pallas_call(
        functools.partial(_resid_kernel, gate_idx=gate_idx,
                          then_norm=None if next_norm is None else tuple(next_norm[2:])),
        grid=(b, t // tt),
        in_specs=in_specs,
        out_specs=out_specs,
        out_shape=out_shapes,
        input_output_aliases={0: 0} if in_place else {},
        compiler_params=_params("parallel", "arbitrary"),
        name="residual",
    )(*args)


def _mm_resid_kernel(a_ref, w_ref, h_ref, g_ref, g2_ref, mod_ref, modc_ref, mod2_ref, mod2c_ref,
                     o_ref, f_ref, acc_ref, *, gate_idx, shift_idx, scale_idx, n_ctx):
    tt = a_ref.shape[1]
    kk = pl.program_id(2)
    part = jnp.dot(a_ref[0], w_ref[...].astype(BF16), preferred_element_type=F32)

    @pl.when(kk == 0)
    def _():
        acc_ref[...] = part

    @pl.when(kk > 0)
    def _():
        acc_ref[...] += part

    @pl.when(kk == pl.num_programs(2) - 1)
    def _():
        is_ctx = pl.program_id(1) * tt + lax.broadcasted_iota(jnp.int32, (tt, 1), 0) < n_ctx

        def pick(batch_ref, ctx_ref, idx):
            return jnp.where(is_ctx, ctx_ref[0, idx:idx + 1, :], batch_ref[0, idx:idx + 1, :])

        new = h_ref[0] + pick(mod_ref, modc_ref, gate_idx) * (_rms(acc_ref[...]) * g_ref[...])
        o_ref[0] = new
        f_ref[0] = (_rms(new) * g2_ref[...] * (1.0 + pick(mod2_ref, mod2c_ref, scale_idx))
                    + pick(mod2_ref, mod2c_ref, shift_idx)).astype(f_ref.dtype)


def matmul_residual(a, w, layer, h, g, mods, gate_idx, n_ctx, next_norm, norm_dtype, tt, tk=None):
    b, t, d = h.shape
    k = a.shape[-1]
    tt = _tile(t, tt, 16)
    tk = k if tk is None else _tile(k, tk)
    g2, mods2, shift_idx, scale_idx = next_norm
    row = pl.BlockSpec((1, tt, d), lambda i, j, q: (i, j, 0))
    vec = pl.BlockSpec((1, d), lambda i, j, q: (0, 0))
    mod_b = pl.BlockSpec((1, 6, d), lambda i, j, q: (i, 0, 0))
    mod_c = pl.BlockSpec((1, 6, d), lambda i, j, q: (b, 0, 0))
    return pl.pallas_call(
        functools.partial(_mm_resid_kernel, gate_idx=gate_idx, shift_idx=shift_idx,
                          scale_idx=scale_idx, n_ctx=n_ctx),
        grid=(b, t // tt, k // tk),
        in_specs=[pl.BlockSpec((1, tt, tk), lambda i, j, q: (i, j, q)),
                  pl.BlockSpec((None, tk, d), lambda i, j, q: (layer, q, 0)),
                  row, vec, vec, mod_b, mod_c, mod_b, mod_c],
        out_specs=[row, row],
        out_shape=[jax.ShapeDtypeStruct((b, t, d), F32), jax.ShapeDtypeStruct((b, t, d), norm_dtype)],
        scratch_shapes=[pltpu.VMEM((tt, d), F32)],
        input_output_aliases={2: 0},
        compiler_params=_params("parallel", "parallel", "arbitrary"),
        name="matmul_residual",
    )(a, w, h, g.reshape(1, d), g2.reshape(1, d), mods, mods, mods2, mods2)


def rope_tables(n_ctx, seq, dim):
    quarter = dim // 4
    inv = ROPE_THETA ** (-jnp.arange(quarter, dtype=F32) / quarter)
    pos = jnp.arange(seq)
    ar = (pos // GRID_W).astype(F32)[:, None] * inv
    ac = (pos % GRID_W).astype(F32)[:, None] * inv
    ang = jnp.concatenate([ar, ar, ac, ac], axis=-1)
    ang = jnp.tile(ang, (1, HEAD // dim))
    ang = jnp.concatenate([jnp.zeros((n_ctx, HEAD), F32), ang], axis=0)
    first = ((jnp.arange(HEAD) % dim) % (dim // 2)) < quarter
    cos, sin = jnp.cos(ang), jnp.sin(ang)
    return cos, jnp.where(first, -sin, 0.0), jnp.where(first, 0.0, sin)


def _rope(x, cos_ref, sa_ref, sb_ref, quarter):
    return (x * cos_ref[...] + pltpu.roll(x, HEAD - quarter, 1) * sa_ref[...]
            + pltpu.roll(x, quarter, 1) * sb_ref[...])


def _qk_prep_kernel(dfq_ref, dfk_ref, gqq_ref, gqk_ref, dcos, dsa, dsb, gcos, gsa, gsb, gq_ref, gk_ref,
                    q1_ref, q2_ref, dk_ref, gqo_ref, gko_ref):
    lane = lax.broadcasted_iota(jnp.int32, dcos.shape, 1)
    for h in range(DF_HEADS):
        sl = slice(h * HEAD, (h + 1) * HEAD)
        y = _rope(dfq_ref[0, :, sl].astype(F32), dcos, dsa, dsb, DF_DIM // 4) * DF_DIM ** -0.5
        q1_ref[0, :, sl] = jnp.where(lane < DF_DIM, y, 0.0).astype(BF16)
        q2_ref[0, :, sl] = jnp.where(lane >= DF_DIM, y, 0.0).astype(BF16)
        dk_ref[0, :, sl] = _rope(dfk_ref[0, :, sl].astype(F32), dcos, dsa, dsb,
                                 DF_DIM // 4).astype(BF16)
    for h in range(GQ_HEADS):
        sl = slice(h * HEAD, (h + 1) * HEAD)
        x = _rms(gqq_ref[0, :, sl].astype(F32)) * gq_ref[...]
        gqo_ref[0, :, sl] = (_rope(x, gcos, gsa, gsb, HEAD // 4) * HEAD ** -0.5).astype(BF16)
    for h in range(GQ_KV_HEADS):
        sl = slice(h * HEAD, (h + 1) * HEAD)
        x = _rms(gqk_ref[0, :, sl].astype(F32)) * gk_ref[...]
        gko_ref[0, :, sl] = _rope(x, gcos, gsa, gsb, HEAD // 4).astype(BF16)


def qk_prep(p, cols, rope_df, rope_gq, g_q, g_k):
    b, t, _ = p.shape
    tt = _tile(t, 1088, 16)
    widths = (DF_HEADS * HEAD, DF_HEADS * HEAD, GQ_HEADS * HEAD, GQ_KV_HEADS * HEAD)
    assert all(c % w == 0 for c, w in zip(cols, widths))

    def col_spec(c0, w):
        return pl.BlockSpec((1, tt, w), lambda i, j: (i, j, c0 // w))

    def out_spec(w):
        return pl.BlockSpec((1, tt, w), lambda i, j: (i, j, 0))

    tab = pl.BlockSpec((tt, HEAD), lambda i, j: (j, 0))
    vec = pl.BlockSpec((1, HEAD), lambda i, j: (0, 0))
    out_w = (widths[0], widths[0], widths[1], widths[2], widths[3])
    return pl.pallas_call(
        _qk_prep_kernel,
        grid=(b, t // tt),
        in_specs=[col_spec(c, w) for c, w in zip(cols, widths)] + [tab] * 6 + [vec, vec],
        out_specs=[out_spec(w) for w in out_w],
        out_shape=[jax.ShapeDtypeStruct((b, t, w), BF16) for w in out_w],
        compiler_params=_params("parallel", "parallel"),
        name="qk_prep",
    )(p, p, p, p, *rope_df, *rope_gq, g_q.reshape(1, HEAD), g_k.reshape(1, HEAD))


ATTN_Q_TILE = 512
ATTN_K_CHUNK = 512


def _attend(q, k_ref, vx_ref, nk):
    kc = ATTN_K_CHUNK
    m = acc = None
    for c0 in range(0, nk, kc):
        c1 = min(c0 + kc, nk)
        s = lax.dot_general(q, k_ref[0, c0:c1, :], (((1,), (1,)), ((), ())),
                            preferred_element_type=F32)
        m_c = jnp.max(s, axis=-1, keepdims=True)
        m_new = m_c if m is None else jnp.maximum(m, m_c)
        pv = jnp.dot(jnp.exp((s - m_new).astype(BF16)), vx_ref[c0:c1, :],
                     preferred_element_type=F32)
        acc = pv if acc is None else acc * jnp.exp(m - m_new) + pv
        m = m_new
    return acc[:, :HEAD] / acc[:, HEAD:]


def _attn_kernel(*refs, diff, n_ctx, tq, post_scale):
    if diff:
        lam_ref, qa_ref, qb_ref, k_ref, v_ref, g_ref, o_ref, vx_ref = refs
    else:
        qa_ref, k_ref, v_ref, o_ref, vx_ref = refs
    n_keys = k_ref.shape[1]

    def rows(r0, n, nk):
        if diff:
            qa, qb = qa_ref[0, pl.ds(r0, n), :], qb_ref[0, pl.ds(r0, n), :]
        else:
            pair = qa_ref[0, pl.ds(r0, n), :]
            qa, qb = pair[:, :HEAD], pair[:, HEAD:]
        oa = _attend(qa, k_ref, vx_ref, nk)
        ob = _attend(qb, k_ref, vx_ref, nk)
        if diff:
            o = oa - lam_ref[...] * ob
            o = _rms(o) * g_ref[...] * post_scale
        else:
            o = jnp.concatenate([oa, ob], axis=1)
        o_ref[0, pl.ds(r0, n), :] = o.astype(o_ref.dtype)

    j = pl.program_id(2)

    @pl.when(j == 0)
    def _():
        vx_ref[:, :HEAD] = v_ref[0]
        vx_ref[:, HEAD:] = jnp.ones((n_keys, HEAD), BF16)
        rows(0, n_ctx, n_ctx)

    rows(pl.multiple_of(n_ctx + j * tq, math.gcd(n_ctx, tq)), tq, n_keys)


def attention(q, k, v, v_col0, n_groups, n_ctx, q2=None, lam=None, g=None, post_scale=1.0):
    b, t, _ = k.shape
    tq = _tile(t - n_ctx, ATTN_Q_TILE, 16)
    vc = v_col0 // HEAD
    diff = q2 is not None
    qw = HEAD if diff else 2 * HEAD
    qspec = pl.BlockSpec((1, t, qw), lambda i, h, j: (i, 0, h))
    kspec = pl.BlockSpec((1, t, HEAD), lambda i, h, j: (i, 0, h))
    vspec = pl.BlockSpec((1, t, HEAD), lambda i, h, j: (i, 0, vc + h))
    if diff:
        one = pl.BlockSpec((1, 1), lambda i, h, j: (0, 0))
        in_specs = [one, qspec, qspec, kspec, vspec, pl.BlockSpec((1, HEAD), lambda i, h, j: (0, 0))]
        args = (lam.reshape(1, 1).astype(F32), q, q2, k, v, g.reshape(1, HEAD))
    else:
        in_specs = [qspec, kspec, vspec]
        args = (q, k, v)
    return pl.pallas_call(
        functools.partial(_attn_kernel, diff=diff, n_ctx=n_ctx, tq=tq, post_scale=post_scale),
        grid=(b, n_groups, (t - n_ctx) // tq),
        in_specs=in_specs,
        out_specs=qspec,
        out_shape=jax.ShapeDtypeStruct((b, t, n_groups * qw), BF16),
        scratch_shapes=[pltpu.VMEM((t, 2 * HEAD), BF16)],
        compiler_params=_params("parallel", "parallel", "arbitrary"),
        name="diff_attention" if diff else "gq_attention",
    )(*args)


def _chunk_prefix(x, row_in_chunk, c):
    s = 1
    while s < c:
        x = x + jnp.where(row_in_chunk >= s, pltpu.roll(x, s, 0), 0.0)
        s *= 2
    return x


def _scan_kernel(q_ref, zf_ref, zb_ref, v_ref, hg_ref, lb_ref, g_ref, o_ref,
                 of_ref, ob_ref, sf_ref, sb_ref, *, n_ctx_blocks):
    rows = SCAN_BLOCK
    c = GLA_CHUNK
    n_chunks = rows // c
    n_blocks = q_ref.shape[1] // rows
    half = c // 2

    r_i = lax.broadcasted_iota(jnp.int32, (rows, rows), 0)
    s_i = lax.broadcasted_iota(jnp.int32, (rows, rows), 1)
    shift = c.bit_length() - 1
    same = (r_i >> shift) == (s_i >> shift)
    mask_f = same & (s_i <= r_i)
    mask_b = same & (s_i >= r_i)
    row_in_chunk = lax.broadcasted_iota(jnp.int32, (rows, HEAD), 0) & (c - 1)

    def chunk_row(x, row):
        return x.reshape(n_chunks, c, HEAD)[:, row:row + 1, :]

    def spread(x3):
        return jnp.broadcast_to(x3, (n_chunks, c, HEAD)).reshape(rows, HEAD)

    def block(blk, hd, z_ref, lb, s_ref, out_ref, mask, reverse):
        r0 = pl.multiple_of(blk * rows, rows)
        z = z_ref[0, pl.ds(r0, rows), hd].astype(F32)
        q = q_ref[0, pl.ds(r0, rows), hd].astype(F32)
        v = v_ref[0, pl.ds(r0, rows), hd]
        kk = (1.0 - lb) * jax.nn.sigmoid(-z)
        lf = jnp.log(1.0 - kk)
        pre = _chunk_prefix(lf, row_in_chunk, c)
        end3 = chunk_row(pre, c - 1)
        if reverse:
            cum = spread(end3) - pre + lf
            mid3 = chunk_row(cum, half)
        else:
            cum = pre
            mid3 = chunk_row(cum, half - 1)
        mid, end = spread(mid3), spread(end3)
        qa = (q * jnp.exp(cum - mid)).astype(BF16)
        ka = (kk * jnp.exp(mid - cum)).astype(BF16)
        att = lax.dot_general(qa, ka, (((1,), (1,)), ((), ())), preferred_element_type=F32)
        att = jnp.where(mask, att, 0.0).astype(BF16)
        o_intra = jnp.dot(att, v, preferred_element_type=F32)
        q_dec = (q * jnp.exp(cum)).astype(BF16)
        k_dec = (kk * jnp.exp(end - cum)).astype(BF16)
        dec = jnp.exp(end3)
        st = s_ref[...]
        outs = [None] * n_chunks
        order = range(n_chunks - 1, -1, -1) if reverse else range(n_chunks)
        for n in order:
            sl = slice(n * c, (n + 1) * c)
            outs[n] = lax.dot_general(q_dec[sl], st.astype(BF16), (((1,), (1,)), ((), ())),
                                      preferred_element_type=F32)
            upd = lax.dot_general(v[sl], k_dec[sl], (((0,), (0,)), ((), ())),
                                  preferred_element_type=F32)
            st = st * dec[n] + upd
        s_ref[...] = st
        out_ref[pl.ds(r0, rows), hd] = o_intra + jnp.concatenate(outs, axis=0)

    sf_ref[...] = jnp.zeros_like(sf_ref)
    sb_ref[...] = jnp.zeros_like(sb_ref)
    heads = [slice(k * HEAD, (k + 1) * HEAD) for k in range(q_ref.shape[2] // HEAD)]

    def step(i, carry):
        j = jnp.where(i < n_ctx_blocks, n_ctx_blocks - 1 - i, n_blocks - 1 - i + n_ctx_blocks)
        for k, hd in enumerate(heads):
            block(i, hd, zf_ref, lb_ref[0:1, hd], sf_ref.at[k], of_ref, mask_f, False)
            block(j, hd, zb_ref, lb_ref[1:2, hd], sb_ref.at[k], ob_ref, mask_b, True)
        return carry

    lax.fori_loop(0, n_blocks, step, 0)

    def finish(i, carry):
        r0 = pl.multiple_of(i * rows, rows)
        for hd in heads:
            o = of_ref[pl.ds(r0, rows), hd] + ob_ref[pl.ds(r0, rows), hd]
            gate = hg_ref[0, pl.ds(r0, rows), hd].astype(F32)
            y = _rms(o) * g_ref[...] * (gate * jax.nn.sigmoid(gate))
            o_ref[0, pl.ds(r0, rows), hd] = y.astype(o_ref.dtype)
        return carry

    lax.fori_loop(0, n_blocks, finish, 0)


def hgrn_scan(p, cols, lb, g, n_ctx):
    b, t, _ = p.shape
    assert t % SCAN_BLOCK == 0 and n_ctx % SCAN_BLOCK == 0
    hp = SCAN_HEADS_PER_STEP
    w = hp * HEAD
    assert HG_HEADS % hp == 0 and all(c0 % w == 0 for c0 in cols)

    def col(c0):
        return pl.BlockSpec((1, t, w), lambda i, h: (i, 0, c0 // w + h))

    return pl.pallas_call(
        functools.partial(_scan_kernel, n_ctx_blocks=n_ctx // SCAN_BLOCK),
        grid=(b, HG_HEADS // hp),
        in_specs=[col(c0) for c0 in cols] + [pl.BlockSpec((2, w), lambda i, h: (0, h)),
                                             pl.BlockSpec((1, HEAD), lambda i, h: (0, 0))],
        out_specs=pl.BlockSpec((1, t, w), lambda i, h: (i, 0, h)),
        out_shape=jax.ShapeDtypeStruct((b, t, HG_HEADS * HEAD), BF16),
        scratch_shapes=[pltpu.VMEM((t, w), F32), pltpu.VMEM((t, w), F32),
                        pltpu.VMEM((hp, HEAD, HEAD), F32), pltpu.VMEM((hp, HEAD, HEAD), F32)],
        compiler_params=_params("parallel", "parallel"),
        name="hgrn_scan",
    )(p, p, p, p, p, lb, g.reshape(1, HEAD))


def dft_matrices(n):
    r = 1 << ((n.bit_length() - 1) // 2) if n & (n - 1) == 0 else 1
    c = n // r
    k = jnp.arange(n, dtype=jnp.int32)[:, None]

    def table(cols):
        ang = ((k * cols[None, :]) % n).astype(F32) * (2.0 * math.pi / n)
        return jnp.cos(ang), jnp.sin(ang)

    (ca, sa), (cb, sb) = table(jnp.arange(r, dtype=jnp.int32) * c), table(jnp.arange(c, dtype=jnp.int32))
    ca, sa, cb, sb = ca[:, :, None], sa[:, :, None], cb[:, None, :], sb[:, None, :]
    s = 1.0 / math.sqrt(n)
    cos = ((ca * cb - sa * sb) * s).reshape(n, n)
    sin = ((sa * cb + ca * sb) * s).reshape(n, n)
    return cos.astype(BF16), sin.astype(BF16)


def _fn_channel_kernel(u_ref, cc_ref, sc_ref, a_ref, b_ref):
    for grp in range(FN_GROUPS):
        sl = slice(grp * HEAD, (grp + 1) * HEAD)
        u = u_ref[0, :, sl]
        a_ref[0, :, sl] = jnp.dot(u, cc_ref[...], preferred_element_type=F32).astype(BF16)
        b_ref[0, :, sl] = jnp.dot(u, sc_ref[...], preferred_element_type=F32).astype(BF16)


def _fn_position_kernel(a_ref, b_ref, cl_ref, sl_ref, cx_ref, sx_ref, o_ref, *, n_ctx):
    i = pl.program_id(1)

    @pl.when(i == 0)
    def _():
        a = a_ref[0, :n_ctx, :]
        b = b_ref[0, :n_ctx, :]
        o_ref[0] = (jnp.dot(cx_ref[...], a, preferred_element_type=F32)
                    - jnp.dot(sx_ref[...], b, preferred_element_type=F32)).astype(o_ref.dtype)

    @pl.when(i > 0)
    def _():
        a = a_ref[0, n_ctx:, :]
        b = b_ref[0, n_ctx:, :]
        o_ref[0] = (jnp.dot(cl_ref[...], a, preferred_element_type=F32)
                    - jnp.dot(sl_ref[...], b, preferred_element_type=F32)).astype(o_ref.dtype)


def fourier_mix(p, col0, n_ctx, dft_chan, dft_ctx, dft_lat):
    b, t, _ = p.shape
    w = FN_GROUPS * HEAD
    tt = _tile(t, 1088, 16)
    assert col0 % w == 0
    blk = pl.BlockSpec((1, tt, w), lambda i, j: (i, j, 0))
    mat = pl.BlockSpec((HEAD, HEAD), lambda i, j: (0, 0))
    ua, ub = pl.pallas_call(
        _fn_channel_kernel,
        grid=(b, t // tt),
        in_specs=[pl.BlockSpec((1, tt, w), lambda i, j: (i, j, col0 // w)), mat, mat],
        out_specs=[blk, blk],
        out_shape=[jax.ShapeDtypeStruct((b, t, w), BF16)] * 2,
        compiler_params=_params("parallel", "parallel"),
        name="fourier_channels",
    )(p, *dft_chan)
    seq = t - n_ctx
    tm = n_ctx
    assert seq % tm == 0 and tm % 8 == 0
    full = pl.BlockSpec((1, t, w), lambda i, j: (i, 0, 0))
    lat = pl.BlockSpec((tm, seq), lambda i, j: (jnp.maximum(j - 1, 0), 0))
    ctx = pl.BlockSpec((n_ctx, n_ctx), lambda i, j: (0, 0))
    return pl.pallas_call(
        functools.partial(_fn_position_kernel, n_ctx=n_ctx),
        grid=(b, 1 + seq // tm),
        in_specs=[full, full, lat, lat, ctx, ctx],
        out_specs=pl.BlockSpec((1, tm, w), lambda i, j: (i, j, 0)),
        out_shape=jax.ShapeDtypeStruct((b, t, w), BF16),
        compiler_params=_params("parallel", "parallel"),
        name="fourier_positions",
    )(ua, ub, *dft_lat, *dft_ctx)


def _merge_kernel(*refs):
    ys, gs, ws, o_ref = refs[:N_BRANCH], refs[N_BRANCH:2 * N_BRANCH], refs[2 * N_BRANCH:-1], refs[-1]
    acc = None
    for y_ref, g_ref, w_ref in zip(ys, gs, ws):
        term = jax.nn.sigmoid(g_ref[...].astype(F32)) * jnp.dot(
            y_ref[...], w_ref[0].astype(BF16), preferred_element_type=F32)
        acc = term if acc is None else acc + term
    o_ref[...] = acc.astype(o_ref.dtype)


def merge(branches, p2, gate_col0, w_branch, layer):
    m = p2.shape[0]
    d = w_branch.shape[-1]
    tm, tn = _tile(m, 1024), _tile(d, 512)
    y_specs = [pl.BlockSpec((tm, BRANCH_W), lambda i, j: (i, 0))] * N_BRANCH
    g_specs = [pl.BlockSpec((tm, tn), functools.partial(
        lambda i, j, off: (i, off + j), off=(gate_col0 + k * d) // tn)) for k in range(N_BRANCH)]
    w_specs = [pl.BlockSpec((None, 1, BRANCH_W, tn),
                            functools.partial(lambda i, j, k: (layer, k, 0, j), k=k))
               for k in range(N_BRANCH)]
    return pl.pallas_call(
        _merge_kernel,
        grid=(m // tm, d // tn),
        in_specs=y_specs + g_specs + w_specs,
        out_specs=pl.BlockSpec((tm, tn), lambda i, j: (i, j)),
        out_shape=jax.ShapeDtypeStruct((m, d), BF16),
        compiler_params=_params("parallel", "parallel"),
        name="merge",
    )(*[y.reshape(m, BRANCH_W) for y in branches], *([p2] * N_BRANCH), *([w_branch] * N_BRANCH))


def _swiglu_kernel(a_ref, wg_ref, wu_ref, o_ref, wgb_ref, wub_ref):
    @pl.when(pl.program_id(1) == 0)
    def _():
        wgb_ref[...] = wg_ref[...].astype(BF16)
        wub_ref[...] = wu_ref[...].astype(BF16)

    a = a_ref[...]
    gate = jnp.dot(a, wgb_ref[...], preferred_element_type=F32)
    up = jnp.dot(a, wub_ref[...], preferred_element_type=F32)
    o_ref[...] = (gate * jax.nn.sigmoid(gate) * up).astype(BF16)


def swiglu_up(a, wg, wu, layer):
    m, d = a.shape
    f = wg.shape[2]
    tm, tn = _tile(m, 1024), _tile(f, 512)
    wspec = pl.BlockSpec((None, d, tn), lambda j, i: (layer, 0, j))
    return pl.pallas_call(
        _swiglu_kernel,
        grid=(f // tn, m // tm),
        in_specs=[pl.BlockSpec((tm, d), lambda j, i: (i, 0)), wspec, wspec],
        out_specs=pl.BlockSpec((tm, tn), lambda j, i: (i, j)),
        out_shape=jax.ShapeDtypeStruct((m, f), BF16),
        scratch_shapes=[pltpu.VMEM((d, tn), BF16), pltpu.VMEM((d, tn), BF16)],
        compiler_params=_params("parallel", "arbitrary"),
        name="swiglu_up",
    )(a, wg, wu)


MOE_ROW_TILE = 512
MOE_INFO_ROWS = 8


def _router_kernel(a_ref, w_ref, b_ref, info_ref, cnt_ref, carry_ref):
    @pl.when(pl.program_id(0) == 0)
    def _():
        carry_ref[...] = jnp.zeros_like(carry_ref)

    logits = lax.dot_general(w_ref[...], a_ref[...].astype(BF16), (((1,), (1,)), ((), ())),
                             preferred_element_type=F32) + b_ref[...]
    n_exp, tm = logits.shape
    e = lax.broadcasted_iota(jnp.int32, logits.shape, 0).astype(F32)
    m1 = jnp.max(logits, axis=0, keepdims=True)
    i1 = jnp.min(jnp.where(logits == m1, e, float(n_exp)), axis=0, keepdims=True)
    rest = jnp.where(e == i1, -jnp.inf, logits)
    m2 = jnp.max(rest, axis=0, keepdims=True)
    i2 = jnp.min(jnp.where(rest == m2, e, float(n_exp)), axis=0, keepdims=True)
    x = jnp.exp(m2 - m1)
    w1 = 1.0 / (1.0 + x)
    w2 = x * w1

    chosen = jnp.where((e == i1) | (e == i2), 1.0, 0.0)
    s_i = lax.broadcasted_iota(jnp.int32, (tm, tm), 0)
    t_i = lax.broadcasted_iota(jnp.int32, (tm, tm), 1)
    earlier = (s_i < t_i).astype(BF16)
    rank = jnp.dot(chosen.astype(BF16), earlier, preferred_element_type=F32) + carry_ref[:, 0:1]
    r1 = jnp.sum(jnp.where(e == i1, rank, 0.0), axis=0, keepdims=True)
    r2 = jnp.sum(jnp.where(e == i2, rank, 0.0), axis=0, keepdims=True)
    carry_ref[...] = carry_ref[...] + jnp.sum(chosen, axis=1, keepdims=True)
    cnt_ref[...] = carry_ref[...]

    row = lax.broadcasted_iota(jnp.int32, (MOE_INFO_ROWS, tm), 0)
    info = jnp.zeros((MOE_INFO_ROWS, tm), F32)
    for k, val in enumerate((i1, i2, w1, w2, r1, r2)):
        info = jnp.where(row == k, val, info)
    info_ref[...] = info


def route(a, w_router, b_router):
    m, d = a.shape
    n_exp = w_router.shape[1]
    tm = _tile(m, 1024)
    return pl.pallas_call(
        _router_kernel,
        grid=(m // tm,),
        in_specs=[pl.BlockSpec((tm, d), lambda i: (i, 0)),
                  pl.BlockSpec((n_exp, d), lambda i: (0, 0)),
                  pl.BlockSpec((n_exp, 1), lambda i: (0, 0))],
        out_specs=[pl.BlockSpec((MOE_INFO_ROWS, tm), lambda i: (0, i)),
                   pl.BlockSpec((n_exp, HEAD), lambda i: (0, 0))],
        out_shape=[jax.ShapeDtypeStruct((MOE_INFO_ROWS, m), F32),
                   jax.ShapeDtypeStruct((n_exp, HEAD), F32)],
        scratch_shapes=[pltpu.VMEM((n_exp, HEAD), F32)],
        compiler_params=_params("arbitrary"),
        name="router",
    )(a, w_router.T.astype(BF16), b_router.reshape(n_exp, 1).astype(F32))


def _row_copies(src_ref, dst_ref, src_row, dst_row, sem):
    return pltpu.make_async_copy(src_ref.at[pl.ds(src_row, 1), :], dst_ref.at[pl.ds(dst_row, 1), :], sem)


def _wait_rows(src_ref, dst_ref, n_rows, sem):
    pltpu.make_async_copy(src_ref.at[pl.ds(0, n_rows), :], dst_ref.at[pl.ds(0, n_rows), :], sem).wait()


def _dispatch_kernel(d1_ref, d2_ref, pt_ref, nu_ref, x_ref, xg_ref, zero_ref, sem, zsem):
    tt = x_ref.shape[0]
    tm = zero_ref.shape[0]
    base = pl.program_id(0) * tt

    def zero_tile(tile):
        return pltpu.make_async_copy(zero_ref, xg_ref.at[pl.ds(tile * tm, tm), :], zsem)

    @pl.when(pl.program_id(0) == 0)
    def _():
        zero_ref[...] = jnp.zeros_like(zero_ref)
        for e in range(pt_ref.shape[0]):
            @pl.when(pt_ref[e] >= 0)
            def _():
                zero_tile(pt_ref[e]).start()
        for e in range(pt_ref.shape[0]):
            @pl.when(pt_ref[e] >= 0)
            def _():
                zero_tile(pt_ref[e]).wait()

        def zero_unused(tile, carry):
            zero_tile(tile).start()
            zero_tile(tile).wait()
            return carry

        lax.fori_loop(nu_ref[0], xg_ref.shape[0] // tm, zero_unused, 0)

    def send(r, carry):
        _row_copies(x_ref, xg_ref, r, d1_ref[base + r], sem).start()
        _row_copies(x_ref, xg_ref, r, d2_ref[base + r], sem).start()
        return carry

    lax.fori_loop(0, tt, send, 0, unroll=8)
    _wait_rows(x_ref, xg_ref, tt, sem)
    _wait_rows(x_ref, xg_ref, tt, sem)


def dispatch(x, dest1, dest2, pad_tile, n_used, n_rows):
    m, d = x.shape
    tt = _tile(m, 256)
    return pl.pallas_call(
        _dispatch_kernel,
        grid_spec=pltpu.PrefetchScalarGridSpec(
            num_scalar_prefetch=4,
            grid=(m // tt,),
            in_specs=[pl.BlockSpec((tt, d), lambda i, d1, d2, pt, nu: (i, 0))],
            out_specs=pl.BlockSpec(memory_space=pl.ANY),
            scratch_shapes=[pltpu.VMEM((MOE_ROW_TILE, d), F32), pltpu.SemaphoreType.DMA(()),
                            pltpu.SemaphoreType.DMA(())]),
        out_shape=jax.ShapeDtypeStruct((n_rows, d), F32),
        compiler_params=_params("arbitrary"),
        name="moe_dispatch",
    )(dest1, dest2, pad_tile, n_used, x)


def _expert_ffn_kernel(te_ref, nu_ref, x_ref, wg_ref, wu_ref, wd_ref, o_ref):
    i, j = pl.program_id(0), pl.program_id(1)
    used = i < nu_ref[0]

    @pl.when(used)
    def _():
        a = x_ref[...].astype(BF16)
        gate = jnp.dot(a, wg_ref[0].astype(BF16), preferred_element_type=F32)
        up = jnp.dot(a, wu_ref[0].astype(BF16), preferred_element_type=F32)
        hidden = (gate * jax.nn.sigmoid(gate) * up).astype(BF16)
        part = jnp.dot(hidden, wd_ref[0].astype(BF16), preferred_element_type=F32)

        @pl.when(j == 0)
        def _():
            o_ref[...] = part

        @pl.when(j > 0)
        def _():
            o_ref[...] += part

    @pl.when(jnp.logical_not(used) & (j == 0))
    def _():
        o_ref[...] = jnp.zeros_like(o_ref)


def expert_ffn(xg, tile_expert, n_used, wg, wu, wd, layer):
    r, d = xg.shape
    f = wg.shape[-1]
    tm, tf = MOE_ROW_TILE, _tile(f, 1024)
    nf = f // tf

    def hidden_block(i, j, nu):
        return jnp.where(i < nu[0], j, nf - 1)

    def up_index(i, j, te, nu):
        return (layer, te[i], 0, hidden_block(i, j, nu))

    return pl.pallas_call(
        _expert_ffn_kernel,
        grid_spec=pltpu.PrefetchScalarGridSpec(
            num_scalar_prefetch=2,
            grid=(r // tm, nf),
            in_specs=[pl.BlockSpec((tm, d), lambda i, j, te, nu: (jnp.where(i < nu[0], i, 0), 0)),
                      pl.BlockSpec((None, 1, d, tf), up_index),
                      pl.BlockSpec((None, 1, d, tf), up_index),
                      pl.BlockSpec((None, 1, tf, d),
                                   lambda i, j, te, nu: (layer, te[i], hidden_block(i, j, nu), 0))],
            out_specs=pl.BlockSpec((tm, d), lambda i, j, te, nu: (i, 0))),
        out_shape=jax.ShapeDtypeStruct((r, d), F32),
        compiler_params=_params("parallel", "arbitrary"),
        name="expert_ffn",
    )(tile_expert, n_used, xg, wg, wu, wd)


def _combine_kernel(d1_ref, d2_ref, h_ref, y_ref, w1_ref, w2_ref, g_ref, mod_ref, *rest,
                    gate_idx, then_norm):
    if then_norm:
        g2_ref, mod2_ref, o_ref, a_ref, y1_ref, y2_ref, sem = rest
    else:
        o_ref, y1_ref, y2_ref, sem = rest
    tt = h_ref.shape[1]
    base = (pl.program_id(0) * pl.num_programs(1) + pl.program_id(1)) * tt

    def fetch(r, carry):
        _row_copies(y_ref, y1_ref, d1_ref[base + r], r, sem).start()
        _row_copies(y_ref, y2_ref, d2_ref[base + r], r, sem).start()
        return carry

    lax.fori_loop(0, tt, fetch, 0, unroll=8)
    _wait_rows(y_ref, y1_ref, tt, sem)
    _wait_rows(y_ref, y2_ref, tt, sem)
    y = w1_ref[...] * y1_ref[...] + w2_ref[...] * y2_ref[...]
    gate = mod_ref[0, gate_idx:gate_idx + 1, :]
    new = h_ref[0] + gate * (_rms(y) * g_ref[...])
    o_ref[0] = new
    if then_norm:
        a_ref[0] = _modulated_norm(new, g2_ref, mod2_ref, *then_norm).astype(a_ref.dtype)


def _stream_out(b, t, d, tt, n_ctx, latents_only):
    if not latents_only:
        return (lambda i, j, *_: (i, j, 0)), jax.ShapeDtypeStruct((b, t, d), F32), True
    skip = n_ctx // tt
    return ((lambda i, j, *_: (i, jnp.maximum(j - skip, 0), 0)),
            jax.ShapeDtypeStruct((b, t - n_ctx, d), F32), False)


def combine_residual(h, yg, dest1, dest2, w1, w2, g, mods, gate_idx, n_ctx, latents_only=False,
                     next_norm=None):
    b, t, d = h.shape
    tt = _tile(n_ctx, 256)
    nt = t // tt
    col = pl.BlockSpec((tt, 1), lambda i, j, d1, d2: (i * nt + j, 0))
    n_ctx_tiles = n_ctx // tt
    out_index, out_shape, in_place = _stream_out(b, t, d, tt, n_ctx, latents_only)
    row = pl.BlockSpec((1, tt, d), lambda i, j, d1, d2: (i, j, 0))
    vec = pl.BlockSpec((1, d), lambda i, j, d1, d2: (0, 0))
    mod = pl.BlockSpec((1, 6, d), lambda i, j, d1, d2: (jnp.where(j < n_ctx_tiles, b, i), 0, 0))
    in_specs = [row, pl.BlockSpec(memory_space=pl.ANY), col, col, vec, mod]
    args = [dest1, dest2, h, yg, w1.reshape(-1, 1), w2.reshape(-1, 1), g.reshape(1, d), mods]
    out_specs, out_shapes = pl.BlockSpec((1, tt, d), out_index), out_shape
    if next_norm is not None:
        g2, mods2 = next_norm[:2]
        in_specs, args = in_specs + [vec, mod], args + [g2.reshape(1, d), mods2]
        out_specs, out_shapes = [out_specs, row], [out_shape, jax.ShapeDtypeStruct((b, t, d), BF16)]
    return pl.pallas_call(
        functools.partial(_combine_kernel, gate_idx=gate_idx,
                          then_norm=None if next_norm is None else tuple(next_norm[2:])),
        grid_spec=pltpu.PrefetchScalarGridSpec(
            num_scalar_prefetch=2,
            grid=(b, nt),
            in_specs=in_specs,
            out_specs=out_specs,
            scratch_shapes=[pltpu.VMEM((tt, d), F32), pltpu.VMEM((tt, d), F32),
                            pltpu.SemaphoreType.DMA(())]),
        out_shape=out_shapes,
        input_output_aliases={2: 0} if in_place else {},
        compiler_params=_params("arbitrary", "arbitrary"),
        name="moe_combine",
    )(*args)


def moe_ffn_residual(h, a, w_router, b_router, wg, wu, wd, layer, g, mods, gate_idx, n_ctx,
                     latents_only=False, next_norm=None):
    m, d = a.shape
    n_exp = w_router.shape[1]
    tm = MOE_ROW_TILE
    info, counts = route(a, w_router, b_router)
    i1, i2 = info[0].astype(jnp.int32), info[1].astype(jnp.int32)
    w1, w2 = info[2], info[3]
    r1, r2 = info[4].astype(jnp.int32), info[5].astype(jnp.int32)

    counts = counts[:, 0].astype(jnp.int32)
    padded = (counts + tm - 1) // tm * tm
    ends = jnp.cumsum(padded)
    starts = ends - padded
    n_tiles = (TOP_K * m) // tm + n_exp
    tile_expert = jnp.sum(jnp.arange(n_tiles)[:, None] >= (ends // tm)[None, :], axis=1)
    tile_expert = jnp.minimum(tile_expert, n_exp - 1).astype(jnp.int32)
    n_used = (ends[-1:] // tm).astype(jnp.int32)
    dest1 = starts[i1] + r1
    dest2 = starts[i2] + r2
    pad_tile = jnp.where(counts > 0, ends // tm - 1, -1).astype(jnp.int32)

    xg = dispatch(a, dest1, dest2, pad_tile, n_used, n_tiles * tm)
    yg = expert_ffn(xg, tile_expert, n_used, wg, wu, wd, layer)
    return combine_residual(h, yg, dest1, dest2, w1, w2, g, mods, gate_idx, n_ctx, latents_only,
                            next_norm)


def kernel(x, c, ctx, c_ctx, w_mod, b_mod, g_pre_mix, g_post_mix, g_pre_ffn, g_post_ffn, w_in,
           hgrn_lb, hgrn_norm_g, diff_lambda, diff_norm_g, qk_norm_q, qk_norm_k, w_branch, w_out,
           w_ff_gate, w_ff_up, w_ff_down, w_router, b_router, w_moe_gate, w_moe_up, w_moe_down):
    n_batch, seq, d = x.shape
    n_ctx = ctx.shape[1]
    depth = w_in.shape[0]
    t = n_ctx + seq
    m = n_batch * t

    hw = HG_HEADS * HEAD
    kv_sizes = (hw, hw, hw, DF_HEADS * 2 * DF_DIM, DF_HEADS * 2 * DF_DIM,
                GQ_KV_HEADS * HEAD, GQ_KV_HEADS * HEAD)
    q_sizes = (hw, hw, DF_HEADS * 2 * DF_DIM, GQ_HEADS * HEAD, FN_GROUPS * HEAD, N_BRANCH * d)
    offs = np.concatenate([[0], np.cumsum(kv_sizes + q_sizes)]).astype(int)
    (c_ff, c_fb, c_hv, c_dfk, c_dfv, c_gqk, c_gqv,
     c_hq, c_hg, c_dfq, c_gqq, c_fu, c_gl) = [int(o) for o in offs[:-1]]

    rope_df = rope_tables(n_ctx, seq, DF_DIM)
    rope_gq = rope_tables(n_ctx, seq, HEAD)
    dft_chan = dft_matrices(HEAD)
    dft_ctx = dft_matrices(n_ctx)
    dft_lat = dft_matrices(seq)

    lb_all = jnp.cumsum(jax.nn.softmax(hgrn_lb.astype(F32), axis=0), axis=0)
    lb_all = lb_all - lb_all[:1]

    n_cond = n_batch + 1
    pad = (-n_cond) % 16
    cond = jnp.concatenate([c, c_ctx[None, :], jnp.zeros((pad, d), F32)], axis=0)
    mods_all = modulation(cond, w_mod, b_mod)[:, :n_cond].reshape(depth, n_cond, 6, d)

    h = jnp.concatenate([ctx, x], axis=1)

    (w_branch, w_out, w_ff_down, w_moe_gate, w_moe_up, w_moe_down) = [
        w.astype(BF16) for w in (w_branch, w_out, w_ff_down, w_moe_gate, w_moe_up, w_moe_down)]

    for l in range(depth):
        mods = mods_all[l]
        lam_init = 0.8 - 0.6 * math.exp(-0.3 * l)
        lv = diff_lambda[l].astype(F32)
        lam = jnp.exp(jnp.sum(lv[0] * lv[1])) - jnp.exp(jnp.sum(lv[2] * lv[3])) + lam_init

        if l == 0:
            a = norm_mod(h, g_pre_mix[l], mods, 0, 1, n_ctx)
        p2 = matmul(a.reshape(m, d), w_in, l, BF16, tm=1024, tn=1536)
        p = p2.reshape(n_batch, t, -1)

        y_hg = hgrn_scan(p, (c_hq, c_ff, c_fb, c_hv, c_hg), lb_all[l], hgrn_norm_g[l], n_ctx)

        df_q1, df_q2, df_k, gq_q, gq_k = qk_prep(p, (c_dfq, c_dfk, c_gqq, c_gqk), rope_df, rope_gq,
                                                 qk_norm_q[l], qk_norm_k[l])
        y_df = attention(df_q1, df_k, p, c_dfv, DF_HEADS, n_ctx, q2=df_q2, lam=lam,
                         g=diff_norm_g[l], post_scale=1.0 - lam_init)

        y_fn = fourier_mix(p, c_fu, n_ctx, dft_chan, dft_ctx, dft_lat)

        assert GQ_HEADS == 2 * GQ_KV_HEADS
        y_gq = attention(gq_q, gq_k, p, c_gqv, GQ_KV_HEADS, n_ctx)

        merged = merge((y_hg, y_df, y_fn, y_gq), p2, c_gl, w_branch, l)
        dense = l % 2 == 0
        h, f_in = matmul_residual(merged.reshape(n_batch, t, d), w_out, l, h, g_post_mix[l], mods, 2,
                                  n_ctx, (g_pre_ffn[l], mods, 3, 4), BF16 if dense else F32, tt=272)
        f_in = f_in.reshape(m, d)

        j = l // 2
        last = l == depth - 1
        next_norm = None if last else (g_pre_mix[l + 1], mods_all[l + 1], 0, 1)
        if dense and not last:
            hidden = swiglu_up(f_in, w_ff_gate, w_ff_up, j).reshape(n_batch, t, -1)
            out = matmul_residual(hidden, w_ff_down, j, h, g_post_ffn[l], mods, 5, n_ctx, next_norm,
                                  BF16, tt=544, tk=hidden.shape[-1] // 4)
        elif dense:
            hidden = swiglu_up(f_in, w_ff_gate, w_ff_up, j)
            y = matmul(hidden, w_ff_down, j, F32, tm=1024, tn=512)
            out = residual(h, y, g_post_ffn[l], mods, 5, n_ctx, last, next_norm)
        else:
            out = moe_ffn_residual(h, f_in, w_router[j], b_router[j], w_moe_gate, w_moe_up,
                                   w_moe_down, j, g_post_ffn[l], mods, 5, n_ctx, last, next_norm)
        h, a = (out, None) if last else out

    return h
```

```python
import functools
import math

import jax
import jax.numpy as jnp
import numpy as np
from jax import lax
from jax.experimental import pallas as pl
from jax.experimental.pallas import tpu as pltpu

F32 = jnp.float32
BF16 = jnp.bfloat16

EPS = 1e-6
GRID_W = 64
ROPE_THETA = 10000.0
HEAD = 128
HG_HEADS = 4
GLA_CHUNK = 32
DF_HEADS = 4
DF_DIM = 64
FN_GROUPS = 4
GQ_HEADS = 4
GQ_KV_HEADS = 2
N_BRANCH = 4
BRANCH_W = 512
TOP_K = 2

V7X_VMEM_LIMIT_BYTES = 56 * 1024 * 1024
SCAN_BLOCK = 128
SCAN_HEADS_PER_STEP = 2


def _params(*sem):
    return pltpu.CompilerParams(dimension_semantics=sem, vmem_limit_bytes=V7X_VMEM_LIMIT_BYTES)


def _tile(n, pref, align=128):
    if n <= pref:
        return n
    t = pref - pref % align
    while n % t:
        t -= align
    return t


def _mm_kernel(a_ref, w_ref, o_ref):
    o_ref[...] = jnp.dot(a_ref[...], w_ref[...], preferred_element_type=F32).astype(o_ref.dtype)


def _mm_f32w_kernel(a_ref, w_ref, o_ref, wb_ref):
    @pl.when(pl.program_id(1) == 0)
    def _():
        wb_ref[...] = w_ref[...].astype(BF16)

    o_ref[...] = jnp.dot(a_ref[...], wb_ref[...], preferred_element_type=F32).astype(o_ref.dtype)


def matmul(a, w, layer, out_dtype, tm=1024, tn=1024):
    m, k = a.shape
    n = w.shape[2]
    tm, tn = _tile(m, tm), _tile(n, tn)
    if w.dtype == BF16:
        return pl.pallas_call(
            _mm_kernel,
            grid=(m // tm, n // tn),
            in_specs=[pl.BlockSpec((tm, k), lambda i, j: (i, 0)),
                      pl.BlockSpec((None, k, tn), lambda i, j: (layer, 0, j))],
            out_specs=pl.BlockSpec((tm, tn), lambda i, j: (i, j)),
            out_shape=jax.ShapeDtypeStruct((m, n), out_dtype),
            compiler_params=_params("parallel", "parallel"),
            name="matmul",
        )(a, w)
    return pl.pallas_call(
        _mm_f32w_kernel,
        grid=(n // tn, m // tm),
        in_specs=[pl.BlockSpec((tm, k), lambda j, i: (i, 0)),
                  pl.BlockSpec((None, k, tn), lambda j, i: (layer, 0, j))],
        out_specs=pl.BlockSpec((tm, tn), lambda j, i: (i, j)),
        out_shape=jax.ShapeDtypeStruct((m, n), out_dtype),
        scratch_shapes=[pltpu.VMEM((k, tn), BF16)],
        compiler_params=_params("parallel", "arbitrary"),
        name="matmul_f32w",
    )(a, w)


def _mod_kernel(s_ref, w_ref, b_ref, o_ref):
    s = s_ref[...]
    s = (s * jax.nn.sigmoid(s)).astype(BF16)
    o_ref[0] = jnp.dot(s, w_ref[0].astype(BF16), preferred_element_type=F32) + b_ref[0]


def modulation(cond, w_mod, b_mod):
    depth, d, width = w_mod.shape
    r = cond.shape[0]
    tn = _tile(width, 1024)
    return pl.pallas_call(
        _mod_kernel,
        grid=(depth, width // tn),
        in_specs=[pl.BlockSpec((r, d), lambda l, j: (0, 0)),
                  pl.BlockSpec((1, d, tn), lambda l, j: (l, 0, j)),
                  pl.BlockSpec((1, 1, tn), lambda l, j: (l, 0, j))],
        out_specs=pl.BlockSpec((1, r, tn), lambda l, j: (l, 0, j)),
        out_shape=jax.ShapeDtypeStruct((depth, r, width), F32),
        compiler_params=_params("parallel", "parallel"),
        name="modulation",
    )(cond, w_mod, b_mod.reshape(depth, 1, width))


def _rms(x):
    return x * lax.rsqrt(jnp.mean(x * x, axis=-1, keepdims=True) + EPS)


def _norm_mod_kernel(h_ref, g_ref, mod_ref, o_ref, *, shift_idx, scale_idx):
    y = _rms(h_ref[0]) * g_ref[...]
    shift = mod_ref[0, shift_idx:shift_idx + 1, :]
    scale = mod_ref[0, scale_idx:scale_idx + 1, :]
    o_ref[0] = (y * (1.0 + scale) + shift).astype(o_ref.dtype)


def _mod_index(n_ctx_tiles, n_batch):
    return lambda b, t: (jnp.where(t < n_ctx_tiles, n_batch, b), 0, 0)


def norm_mod(h, g, mods, shift_idx, scale_idx, n_ctx, out_dtype=BF16):
    b, t, d = h.shape
    tt = _tile(n_ctx, 256)
    return pl.pallas_call(
        functools.partial(_norm_mod_kernel, shift_idx=shift_idx, scale_idx=scale_idx),
        grid=(b, t // tt),
        in_specs=[pl.BlockSpec((1, tt, d), lambda i, j: (i, j, 0)),
                  pl.BlockSpec((1, d), lambda i, j: (0, 0)),
                  pl.BlockSpec((1, 6, d), _mod_index(n_ctx // tt, b))],
        out_specs=pl.BlockSpec((1, tt, d), lambda i, j: (i, j, 0)),
        out_shape=jax.ShapeDtypeStruct((b, t, d), out_dtype),
        compiler_params=_params("parallel", "parallel"),
        name="norm_mod",
    )(h, g.reshape(1, d), mods)


def _modulated_norm(x, g_ref, mod_ref, shift_idx, scale_idx):
    scale = mod_ref[0, scale_idx:scale_idx + 1, :]
    shift = mod_ref[0, shift_idx:shift_idx + 1, :]
    return _rms(x) * g_ref[...] * (1.0 + scale) + shift


def _resid_kernel(h_ref, y_ref, g_ref, mod_ref, *rest, gate_idx, then_norm):
    gate = mod_ref[0, gate_idx:gate_idx + 1, :]
    new = h_ref[0] + gate * (_rms(y_ref[0]) * g_ref[...])
    if then_norm:
        g2_ref, mod2_ref, o_ref, a_ref = rest
        a_ref[0] = _modulated_norm(new, g2_ref, mod2_ref, *then_norm).astype(a_ref.dtype)
    else:
        (o_ref,) = rest
    o_ref[0] = new


def residual(h, y, g, mods, gate_idx, n_ctx, latents_only=False, next_norm=None):
    b, t, d = h.shape
    tt = _tile(n_ctx, 256)
    out_index, out_shape, in_place = _stream_out(b, t, d, tt, n_ctx, latents_only)
    row = pl.BlockSpec((1, tt, d), lambda i, j: (i, j, 0))
    vec = pl.BlockSpec((1, d), lambda i, j: (0, 0))
    mod = pl.BlockSpec((1, 6, d), _mod_index(n_ctx // tt, b))
    in_specs, args = [row, row, vec, mod], [h, y.reshape(b, t, d), g.reshape(1, d), mods]
    out_specs, out_shapes = pl.BlockSpec((1, tt, d), out_index), out_shape
    if next_norm is not None:
        g2, mods2 = next_norm[:2]
        in_specs, args = in_specs + [vec, mod], args + [g2.reshape(1, d), mods2]
        out_specs, out_shapes = [out_specs, row], [out_shape, jax.ShapeDtypeStruct((b, t, d), BF16)]
    return pl.pallas_call(
        functools.partial(_resid_kernel, gate_idx=gate_idx,
                          then_norm=None if next_norm is None else tuple(next_norm[2:])),
        grid=(b, t // tt),
        in_specs=in_specs,
        out_specs=out_specs,
        out_shape=out_shapes,
        input_output_aliases={0: 0} if in_place else {},
        compiler_params=_params("parallel", "arbitrary"),
        name="residual",
    )(*args)


def _mm_resid_kernel(a_ref, w_ref, h_ref, g_ref, g2_ref, mod_ref, modc_ref, o_ref, f_ref, *,
                     gate_idx, shift_idx, scale_idx, n_ctx):
    tt = a_ref.shape[1]
    y = jnp.dot(a_ref[0], w_ref[...].astype(BF16), preferred_element_type=F32)
    is_ctx = pl.program_id(1) * tt + lax.broadcasted_iota(jnp.int32, (tt, 1), 0) < n_ctx

    def pick(idx):
        return jnp.where(is_ctx, modc_ref[0, idx:idx + 1, :], mod_ref[0, idx:idx + 1, :])

    new = h_ref[0] + pick(gate_idx) * (_rms(y) * g_ref[...])
    o_ref[0] = new
    f_ref[0] = (_rms(new) * g2_ref[...] * (1.0 + pick(scale_idx)) + pick(shift_idx)).astype(f_ref.dtype)


def matmul_residual(a, w, layer, h, g, mods, gate_idx, n_ctx, g2, shift_idx, scale_idx, norm_dtype):
    b, t, d = h.shape
    k = a.shape[-1]
    tt = _tile(t, 272, 16)
    row = pl.BlockSpec((1, tt, d), lambda i, j: (i, j, 0))
    vec = pl.BlockSpec((1, d), lambda i, j: (0, 0))
    return pl.pallas_call(
        functools.partial(_mm_resid_kernel, gate_idx=gate_idx, shift_idx=shift_idx,
                          scale_idx=scale_idx, n_ctx=n_ctx),
        grid=(b, t // tt),
        in_specs=[pl.BlockSpec((1, tt, k), lambda i, j: (i, j, 0)),
                  pl.BlockSpec((None, k, d), lambda i, j: (layer, 0, 0)),
                  row, vec, vec,
                  pl.BlockSpec((1, 6, d), lambda i, j: (i, 0, 0)),
                  pl.BlockSpec((1, 6, d), lambda i, j: (b, 0, 0))],
        out_specs=[row, row],
        out_shape=[jax.ShapeDtypeStruct((b, t, d), F32), jax.ShapeDtypeStruct((b, t, d), norm_dtype)],
        input_output_aliases={2: 0},
        compiler_params=_params("parallel", "parallel"),
        name="matmul_residual",
    )(a, w, h, g.reshape(1, d), g2.reshape(1, d), mods, mods)


def rope_tables(n_ctx, seq, dim):
    quarter = dim // 4
    inv = ROPE_THETA ** (-jnp.arange(quarter, dtype=F32) / quarter)
    pos = jnp.arange(seq)
    ar = (pos // GRID_W).astype(F32)[:, None] * inv
    ac = (pos % GRID_W).astype(F32)[:, None] * inv
    ang = jnp.concatenate([ar, ar, ac, ac], axis=-1)
    ang = jnp.tile(ang, (1, HEAD // dim))
    ang = jnp.concatenate([jnp.zeros((n_ctx, HEAD), F32), ang], axis=0)
    first = ((jnp.arange(HEAD) % dim) % (dim // 2)) < quarter
    cos, sin = jnp.cos(ang), jnp.sin(ang)
    return cos, jnp.where(first, -sin, 0.0), jnp.where(first, 0.0, sin)


def _rope(x, cos_ref, sa_ref, sb_ref, quarter):
    return (x * cos_ref[...] + pltpu.roll(x, HEAD - quarter, 1) * sa_ref[...]
            + pltpu.roll(x, quarter, 1) * sb_ref[...])


def _qk_prep_kernel(dfq_ref, dfk_ref, gqq_ref, gqk_ref, dcos, dsa, dsb, gcos, gsa, gsb, gq_ref, gk_ref,
                    q1_ref, q2_ref, dk_ref, gqo_ref, gko_ref):
    lane = lax.broadcasted_iota(jnp.int32, dcos.shape, 1)
    for h in range(DF_HEADS):
        sl = slice(h * HEAD, (h + 1) * HEAD)
        y = _rope(dfq_ref[0, :, sl].astype(F32), dcos, dsa, dsb, DF_DIM // 4) * DF_DIM ** -0.5
        q1_ref[0, :, sl] = jnp.where(lane < DF_DIM, y, 0.0).astype(BF16)
        q2_ref[0, :, sl] = jnp.where(lane >= DF_DIM, y, 0.0).astype(BF16)
        dk_ref[0, :, sl] = _rope(dfk_ref[0, :, sl].astype(F32), dcos, dsa, dsb,
                                 DF_DIM // 4).astype(BF16)
    for h in range(GQ_HEADS):
        sl = slice(h * HEAD, (h + 1) * HEAD)
        x = _rms(gqq_ref[0, :, sl].astype(F32)) * gq_ref[...]
        gqo_ref[0, :, sl] = (_rope(x, gcos, gsa, gsb, HEAD // 4) * HEAD ** -0.5).astype(BF16)
    for h in range(GQ_KV_HEADS):
        sl = slice(h * HEAD, (h + 1) * HEAD)
        x = _rms(gqk_ref[0, :, sl].astype(F32)) * gk_ref[...]
        gko_ref[0, :, sl] = _rope(x, gcos, gsa, gsb, HEAD // 4).astype(BF16)


def qk_prep(p, cols, rope_df, rope_gq, g_q, g_k):
    b, t, _ = p.shape
    tt = _tile(t, 1088, 16)
    widths = (DF_HEADS * HEAD, DF_HEADS * HEAD, GQ_HEADS * HEAD, GQ_KV_HEADS * HEAD)
    assert all(c % w == 0 for c, w in zip(cols, widths))

    def col_spec(c0, w):
        return pl.BlockSpec((1, tt, w), lambda i, j: (i, j, c0 // w))

    def out_spec(w):
        return pl.BlockSpec((1, tt, w), lambda i, j: (i, j, 0))

    tab = pl.BlockSpec((tt, HEAD), lambda i, j: (j, 0))
    vec = pl.BlockSpec((1, HEAD), lambda i, j: (0, 0))
    out_w = (widths[0], widths[0], widths[1], widths[2], widths[3])
    return pl.pallas_call(
        _qk_prep_kernel,
        grid=(b, t // tt),
        in_specs=[col_spec(c, w) for c, w in zip(cols, widths)] + [tab] * 6 + [vec, vec],
        out_specs=[out_spec(w) for w in out_w],
        out_shape=[jax.ShapeDtypeStruct((b, t, w), BF16) for w in out_w],
        compiler_params=_params("parallel", "parallel"),
        name="qk_prep",
    )(p, p, p, p, *rope_df, *rope_gq, g_q.reshape(1, HEAD), g_k.reshape(1, HEAD))


ATTN_Q_TILE = 512
ATTN_K_CHUNK = 512


def _attend(q, k_ref, vx_ref, nk):
    kc = ATTN_K_CHUNK
    m = acc = None
    for c0 in range(0, nk, kc):
        c1 = min(c0 + kc, nk)
        s = lax.dot_general(q, k_ref[0, c0:c1, :], (((1,), (1,)), ((), ())),
                            preferred_element_type=F32)
        m_c = jnp.max(s, axis=-1, keepdims=True)
        m_new = m_c if m is None else jnp.maximum(m, m_c)
        pv = jnp.dot(jnp.exp((s - m_new).astype(BF16)), vx_ref[c0:c1, :],
                     preferred_element_type=F32)
        acc = pv if acc is None else acc * jnp.exp(m - m_new) + pv
        m = m_new
    return acc[:, :HEAD] / acc[:, HEAD:]


def _attn_kernel(*refs, diff, n_ctx, tq, post_scale):
    if diff:
        lam_ref, qa_ref, qb_ref, k_ref, v_ref, g_ref, o_ref, vx_ref = refs
    else:
        qa_ref, k_ref, v_ref, o_ref, vx_ref = refs
    n_keys = k_ref.shape[1]

    def rows(r0, n, nk):
        if diff:
            qa, qb = qa_ref[0, pl.ds(r0, n), :], qb_ref[0, pl.ds(r0, n), :]
        else:
            pair = qa_ref[0, pl.ds(r0, n), :]
            qa, qb = pair[:, :HEAD], pair[:, HEAD:]
        oa = _attend(qa, k_ref, vx_ref, nk)
        ob = _attend(qb, k_ref, vx_ref, nk)
        if diff:
            o = oa - lam_ref[...] * ob
            o = _rms(o) * g_ref[...] * post_scale
        else:
            o = jnp.concatenate([oa, ob], axis=1)
        o_ref[0, pl.ds(r0, n), :] = o.astype(o_ref.dtype)

    j = pl.program_id(2)

    @pl.when(j == 0)
    def _():
        vx_ref[:, :HEAD] = v_ref[0]
        vx_ref[:, HEAD:] = jnp.ones((n_keys, HEAD), BF16)
        rows(0, n_ctx, n_ctx)

    rows(pl.multiple_of(n_ctx + j * tq, math.gcd(n_ctx, tq)), tq, n_keys)


def attention(q, k, v, v_col0, n_groups, n_ctx, q2=None, lam=None, g=None, post_scale=1.0):
    b, t, _ = k.shape
    tq = _tile(t - n_ctx, ATTN_Q_TILE, 16)
    vc = v_col0 // HEAD
    diff = q2 is not None
    qw = HEAD if diff else 2 * HEAD
    qspec = pl.BlockSpec((1, t, qw), lambda i, h, j: (i, 0, h))
    kspec = pl.BlockSpec((1, t, HEAD), lambda i, h, j: (i, 0, h))
    vspec = pl.BlockSpec((1, t, HEAD), lambda i, h, j: (i, 0, vc + h))
    if diff:
        one = pl.BlockSpec((1, 1), lambda i, h, j: (0, 0))
        in_specs = [one, qspec, qspec, kspec, vspec, pl.BlockSpec((1, HEAD), lambda i, h, j: (0, 0))]
        args = (lam.reshape(1, 1).astype(F32), q, q2, k, v, g.reshape(1, HEAD))
    else:
        in_specs = [qspec, kspec, vspec]
        args = (q, k, v)
    return pl.pallas_call(
        functools.partial(_attn_kernel, diff=diff, n_ctx=n_ctx, tq=tq, post_scale=post_scale),
        grid=(b, n_groups, (t - n_ctx) // tq),
        in_specs=in_specs,
        out_specs=qspec,
        out_shape=jax.ShapeDtypeStruct((b, t, n_groups * qw), BF16),
        scratch_shapes=[pltpu.VMEM((t, 2 * HEAD), BF16)],
        compiler_params=_params("parallel", "parallel", "arbitrary"),
        name="diff_attention" if diff else "gq_attention",
    )(*args)


def _chunk_prefix(x, row_in_chunk, c):
    s = 1
    while s < c:
        x = x + jnp.where(row_in_chunk >= s, pltpu.roll(x, s, 0), 0.0)
        s *= 2
    return x


def _scan_kernel(q_ref, zf_ref, zb_ref, v_ref, hg_ref, lb_ref, g_ref, o_ref,
                 of_ref, ob_ref, sf_ref, sb_ref, *, n_ctx_blocks):
    rows = SCAN_BLOCK
    c = GLA_CHUNK
    n_chunks = rows // c
    n_blocks = q_ref.shape[1] // rows
    half = c // 2

    r_i = lax.broadcasted_iota(jnp.int32, (rows, rows), 0)
    s_i = lax.broadcasted_iota(jnp.int32, (rows, rows), 1)
    shift = c.bit_length() - 1
    same = (r_i >> shift) == (s_i >> shift)
    mask_f = same & (s_i <= r_i)
    mask_b = same & (s_i >= r_i)
    row_in_chunk = lax.broadcasted_iota(jnp.int32, (rows, HEAD), 0) & (c - 1)

    def chunk_row(x, row):
        return x.reshape(n_chunks, c, HEAD)[:, row:row + 1, :]

    def spread(x3):
        return jnp.broadcast_to(x3, (n_chunks, c, HEAD)).reshape(rows, HEAD)

    def block(blk, hd, z_ref, lb, s_ref, out_ref, mask, reverse):
        r0 = pl.multiple_of(blk * rows, rows)
        z = z_ref[0, pl.ds(r0, rows), hd].astype(F32)
        q = q_ref[0, pl.ds(r0, rows), hd].astype(F32)
        v = v_ref[0, pl.ds(r0, rows), hd]
        kk = (1.0 - lb) * jax.nn.sigmoid(-z)
        lf = jnp.log(1.0 - kk)
        pre = _chunk_prefix(lf, row_in_chunk, c)
        end3 = chunk_row(pre, c - 1)
        if reverse:
            cum = spread(end3) - pre + lf
            mid3 = chunk_row(cum, half)
        else:
            cum = pre
            mid3 = chunk_row(cum, half - 1)
        mid, end = spread(mid3), spread(end3)
        qa = (q * jnp.exp(cum - mid)).astype(BF16)
        ka = (kk * jnp.exp(mid - cum)).astype(BF16)
        att = lax.dot_general(qa, ka, (((1,), (1,)), ((), ())), preferred_element_type=F32)
        att = jnp.where(mask, att, 0.0).astype(BF16)
        o_intra = jnp.dot(att, v, preferred_element_type=F32)
        q_dec = (q * jnp.exp(cum)).astype(BF16)
        k_dec = (kk * jnp.exp(end - cum)).astype(BF16)
        dec = jnp.exp(end3)
        st = s_ref[...]
        outs = [None] * n_chunks
        order = range(n_chunks - 1, -1, -1) if reverse else range(n_chunks)
        for n in order:
            sl = slice(n * c, (n + 1) * c)
            outs[n] = lax.dot_general(q_dec[sl], st.astype(BF16), (((1,), (1,)), ((), ())),
                                      preferred_element_type=F32)
            upd = lax.dot_general(v[sl], k_dec[sl], (((0,), (0,)), ((), ())),
                                  preferred_element_type=F32)
            st = st * dec[n] + upd
        s_ref[...] = st
        out_ref[pl.ds(r0, rows), hd] = o_intra + jnp.concatenate(outs, axis=0)

    sf_ref[...] = jnp.zeros_like(sf_ref)
    sb_ref[...] = jnp.zeros_like(sb_ref)
    heads = [slice(k * HEAD, (k + 1) * HEAD) for k in range(q_ref.shape[2] // HEAD)]

    def step(i, carry):
        j = jnp.where(i < n_ctx_blocks, n_ctx_blocks - 1 - i, n_blocks - 1 - i + n_ctx_blocks)
        for k, hd in enumerate(heads):
            block(i, hd, zf_ref, lb_ref[0:1, hd], sf_ref.at[k], of_ref, mask_f, False)
            block(j, hd, zb_ref, lb_ref[1:2, hd], sb_ref.at[k], ob_ref, mask_b, True)
        return carry

    lax.fori_loop(0, n_blocks, step, 0)

    def finish(i, carry):
        r0 = pl.multiple_of(i * rows, rows)
        for hd in heads:
            o = of_ref[pl.ds(r0, rows), hd] + ob_ref[pl.ds(r0, rows), hd]
            gate = hg_ref[0, pl.ds(r0, rows), hd].astype(F32)
            y = _rms(o) * g_ref[...] * (gate * jax.nn.sigmoid(gate))
            o_ref[0, pl.ds(r0, rows), hd] = y.astype(o_ref.dtype)
        return carry

    lax.fori_loop(0, n_blocks, finish, 0)


def hgrn_scan(p, cols, lb, g, n_ctx):
    b, t, _ = p.shape
    assert t % SCAN_BLOCK == 0 and n_ctx % SCAN_BLOCK == 0
    hp = SCAN_HEADS_PER_STEP
    w = hp * HEAD
    assert HG_HEADS % hp == 0 and all(c0 % w == 0 for c0 in cols)

    def col(c0):
        return pl.BlockSpec((1, t, w), lambda i, h: (i, 0, c0 // w + h))

    return pl.pallas_call(
        functools.partial(_scan_kernel, n_ctx_blocks=n_ctx // SCAN_BLOCK),
        grid=(b, HG_HEADS // hp),
        in_specs=[col(c0) for c0 in cols] + [pl.BlockSpec((2, w), lambda i, h: (0, h)),
                                             pl.BlockSpec((1, HEAD), lambda i, h: (0, 0))],
        out_specs=pl.BlockSpec((1, t, w), lambda i, h: (i, 0, h)),
        out_shape=jax.ShapeDtypeStruct((b, t, HG_HEADS * HEAD), BF16),
        scratch_shapes=[pltpu.VMEM((t, w), F32), pltpu.VMEM((t, w), F32),
                        pltpu.VMEM((hp, HEAD, HEAD), F32), pltpu.VMEM((hp, HEAD, HEAD), F32)],
        compiler_params=_params("parallel", "parallel"),
        name="hgrn_scan",
    )(p, p, p, p, p, lb, g.reshape(1, HEAD))


def dft_matrices(n):
    r = 1 << ((n.bit_length() - 1) // 2) if n & (n - 1) == 0 else 1
    c = n // r
    k = jnp.arange(n, dtype=jnp.int32)[:, None]

    def table(cols):
        ang = ((k * cols[None, :]) % n).astype(F32) * (2.0 * math.pi / n)
        return jnp.cos(ang), jnp.sin(ang)

    (ca, sa), (cb, sb) = table(jnp.arange(r, dtype=jnp.int32) * c), table(jnp.arange(c, dtype=jnp.int32))
    ca, sa, cb, sb = ca[:, :, None], sa[:, :, None], cb[:, None, :], sb[:, None, :]
    s = 1.0 / math.sqrt(n)
    cos = ((ca * cb - sa * sb) * s).reshape(n, n)
    sin = ((sa * cb + ca * sb) * s).reshape(n, n)
    return cos.astype(BF16), sin.astype(BF16)


def _fn_channel_kernel(u_ref, cc_ref, sc_ref, a_ref, b_ref):
    for grp in range(FN_GROUPS):
        sl = slice(grp * HEAD, (grp + 1) * HEAD)
        u = u_ref[0, :, sl]
        a_ref[0, :, sl] = jnp.dot(u, cc_ref[...], preferred_element_type=F32).astype(BF16)
        b_ref[0, :, sl] = jnp.dot(u, sc_ref[...], preferred_element_type=F32).astype(BF16)


def _fn_position_kernel(a_ref, b_ref, cl_ref, sl_ref, cx_ref, sx_ref, o_ref, *, n_ctx):
    i = pl.program_id(1)

    @pl.when(i == 0)
    def _():
        a = a_ref[0, :n_ctx, :]
        b = b_ref[0, :n_ctx, :]
        o_ref[0] = (jnp.dot(cx_ref[...], a, preferred_element_type=F32)
                    - jnp.dot(sx_ref[...], b, preferred_element_type=F32)).astype(o_ref.dtype)

    @pl.when(i > 0)
    def _():
        a = a_ref[0, n_ctx:, :]
        b = b_ref[0, n_ctx:, :]
        o_ref[0] = (jnp.dot(cl_ref[...], a, preferred_element_type=F32)
                    - jnp.dot(sl_ref[...], b, preferred_element_type=F32)).astype(o_ref.dtype)


def fourier_mix(p, col0, n_ctx, dft_chan, dft_ctx, dft_lat):
    b, t, _ = p.shape
    w = FN_GROUPS * HEAD
    tt = _tile(t, 1088, 16)
    assert col0 % w == 0
    blk = pl.BlockSpec((1, tt, w), lambda i, j: (i, j, 0))
    mat = pl.BlockSpec((HEAD, HEAD), lambda i, j: (0, 0))
    ua, ub = pl.pallas_call(
        _fn_channel_kernel,
        grid=(b, t // tt),
        in_specs=[pl.BlockSpec((1, tt, w), lambda i, j: (i, j, col0 // w)), mat, mat],
        out_specs=[blk, blk],
        out_shape=[jax.ShapeDtypeStruct((b, t, w), BF16)] * 2,
        compiler_params=_params("parallel", "parallel"),
        name="fourier_channels",
    )(p, *dft_chan)
    seq = t - n_ctx
    tm = n_ctx
    assert seq % tm == 0 and tm % 8 == 0
    full = pl.BlockSpec((1, t, w), lambda i, j: (i, 0, 0))
    lat = pl.BlockSpec((tm, seq), lambda i, j: (jnp.maximum(j - 1, 0), 0))
    ctx = pl.BlockSpec((n_ctx, n_ctx), lambda i, j: (0, 0))
    return pl.pallas_call(
        functools.partial(_fn_position_kernel, n_ctx=n_ctx),
        grid=(b, 1 + seq // tm),
        in_specs=[full, full, lat, lat, ctx, ctx],
        out_specs=pl.BlockSpec((1, tm, w), lambda i, j: (i, j, 0)),
        out_shape=jax.ShapeDtypeStruct((b, t, w), BF16),
        compiler_params=_params("parallel", "parallel"),
        name="fourier_positions",
    )(ua, ub, *dft_lat, *dft_ctx)


def _merge_kernel(*refs):
    ys, gs, ws, o_ref = refs[:N_BRANCH], refs[N_BRANCH:2 * N_BRANCH], refs[2 * N_BRANCH:-1], refs[-1]
    acc = None
    for y_ref, g_ref, w_ref in zip(ys, gs, ws):
        term = jax.nn.sigmoid(g_ref[...].astype(F32)) * jnp.dot(
            y_ref[...], w_ref[0].astype(BF16), preferred_element_type=F32)
        acc = term if acc is None else acc + term
    o_ref[...] = acc.astype(o_ref.dtype)


def merge(branches, p2, gate_col0, w_branch, layer):
    m = p2.shape[0]
    d = w_branch.shape[-1]
    tm, tn = _tile(m, 1024), _tile(d, 512)
    y_specs = [pl.BlockSpec((tm, BRANCH_W), lambda i, j: (i, 0))] * N_BRANCH
    g_specs = [pl.BlockSpec((tm, tn), functools.partial(
        lambda i, j, off: (i, off + j), off=(gate_col0 + k * d) // tn)) for k in range(N_BRANCH)]
    w_specs = [pl.BlockSpec((None, 1, BRANCH_W, tn),
                            functools.partial(lambda i, j, k: (layer, k, 0, j), k=k))
               for k in range(N_BRANCH)]
    return pl.pallas_call(
        _merge_kernel,
        grid=(m // tm, d // tn),
        in_specs=y_specs + g_specs + w_specs,
        out_specs=pl.BlockSpec((tm, tn), lambda i, j: (i, j)),
        out_shape=jax.ShapeDtypeStruct((m, d), BF16),
        compiler_params=_params("parallel", "parallel"),
        name="merge",
    )(*[y.reshape(m, BRANCH_W) for y in branches], *([p2] * N_BRANCH), *([w_branch] * N_BRANCH))


def _swiglu_kernel(a_ref, wg_ref, wu_ref, o_ref, wgb_ref, wub_ref):
    @pl.when(pl.program_id(1) == 0)
    def _():
        wgb_ref[...] = wg_ref[...].astype(BF16)
        wub_ref[...] = wu_ref[...].astype(BF16)

    a = a_ref[...]
    gate = jnp.dot(a, wgb_ref[...], preferred_element_type=F32)
    up = jnp.dot(a, wub_ref[...], preferred_element_type=F32)
    o_ref[...] = (gate * jax.nn.sigmoid(gate) * up).astype(BF16)


def swiglu_up(a, wg, wu, layer):
    m, d = a.shape
    f = wg.shape[2]
    tm, tn = _tile(m, 1024), _tile(f, 512)
    wspec = pl.BlockSpec((None, d, tn), lambda j, i: (layer, 0, j))
    return pl.pallas_call(
        _swiglu_kernel,
        grid=(f // tn, m // tm),
        in_specs=[pl.BlockSpec((tm, d), lambda j, i: (i, 0)), wspec, wspec],
        out_specs=pl.BlockSpec((tm, tn), lambda j, i: (i, j)),
        out_shape=jax.ShapeDtypeStruct((m, f), BF16),
        scratch_shapes=[pltpu.VMEM((d, tn), BF16), pltpu.VMEM((d, tn), BF16)],
        compiler_params=_params("parallel", "arbitrary"),
        name="swiglu_up",
    )(a, wg, wu)


MOE_ROW_TILE = 512
MOE_INFO_ROWS = 8


def _router_kernel(a_ref, w_ref, b_ref, info_ref, cnt_ref, carry_ref):
    @pl.when(pl.program_id(0) == 0)
    def _():
        carry_ref[...] = jnp.zeros_like(carry_ref)

    logits = lax.dot_general(w_ref[...], a_ref[...].astype(BF16), (((1,), (1,)), ((), ())),
                             preferred_element_type=F32) + b_ref[...]
    n_exp, tm = logits.shape
    e = lax.broadcasted_iota(jnp.int32, logits.shape, 0).astype(F32)
    m1 = jnp.max(logits, axis=0, keepdims=True)
    i1 = jnp.min(jnp.where(logits == m1, e, float(n_exp)), axis=0, keepdims=True)
    rest = jnp.where(e == i1, -jnp.inf, logits)
    m2 = jnp.max(rest, axis=0, keepdims=True)
    i2 = jnp.min(jnp.where(rest == m2, e, float(n_exp)), axis=0, keepdims=True)
    x = jnp.exp(m2 - m1)
    w1 = 1.0 / (1.0 + x)
    w2 = x * w1

    chosen = jnp.where((e == i1) | (e == i2), 1.0, 0.0)
    s_i = lax.broadcasted_iota(jnp.int32, (tm, tm), 0)
    t_i = lax.broadcasted_iota(jnp.int32, (tm, tm), 1)
    earlier = (s_i < t_i).astype(BF16)
    rank = jnp.dot(chosen.astype(BF16), earlier, preferred_element_type=F32) + carry_ref[:, 0:1]
    r1 = jnp.sum(jnp.where(e == i1, rank, 0.0), axis=0, keepdims=True)
    r2 = jnp.sum(jnp.where(e == i2, rank, 0.0), axis=0, keepdims=True)
    carry_ref[...] = carry_ref[...] + jnp.sum(chosen, axis=1, keepdims=True)
    cnt_ref[...] = carry_ref[...]

    row = lax.broadcasted_iota(jnp.int32, (MOE_INFO_ROWS, tm), 0)
    info = jnp.zeros((MOE_INFO_ROWS, tm), F32)
    for k, val in enumerate((i1, i2, w1, w2, r1, r2)):
        info = jnp.where(row == k, val, info)
    info_ref[...] = info


def route(a, w_router, b_router):
    m, d = a.shape
    n_exp = w_router.shape[1]
    tm = _tile(m, 1024)
    return pl.pallas_call(
        _router_kernel,
        grid=(m // tm,),
        in_specs=[pl.BlockSpec((tm, d), lambda i: (i, 0)),
                  pl.BlockSpec((n_exp, d), lambda i: (0, 0)),
                  pl.BlockSpec((n_exp, 1), lambda i: (0, 0))],
        out_specs=[pl.BlockSpec((MOE_INFO_ROWS, tm), lambda i: (0, i)),
                   pl.BlockSpec((n_exp, HEAD), lambda i: (0, 0))],
        out_shape=[jax.ShapeDtypeStruct((MOE_INFO_ROWS, m), F32),
                   jax.ShapeDtypeStruct((n_exp, HEAD), F32)],
        scratch_shapes=[pltpu.VMEM((n_exp, HEAD), F32)],
        compiler_params=_params("arbitrary"),
        name="router",
    )(a, w_router.T.astype(BF16), b_router.reshape(n_exp, 1).astype(F32))


def _row_copies(src_ref, dst_ref, src_row, dst_row, sem):
    return pltpu.make_async_copy(src_ref.at[pl.ds(src_row, 1), :], dst_ref.at[pl.ds(dst_row, 1), :], sem)


def _wait_rows(src_ref, dst_ref, n_rows, sem):
    pltpu.make_async_copy(src_ref.at[pl.ds(0, n_rows), :], dst_ref.at[pl.ds(0, n_rows), :], sem).wait()


def _dispatch_kernel(d1_ref, d2_ref, pt_ref, nu_ref, x_ref, xg_ref, zero_ref, sem, zsem):
    tt = x_ref.shape[0]
    tm = zero_ref.shape[0]
    base = pl.program_id(0) * tt

    def zero_tile(tile):
        return pltpu.make_async_copy(zero_ref, xg_ref.at[pl.ds(tile * tm, tm), :], zsem)

    @pl.when(pl.program_id(0) == 0)
    def _():
        zero_ref[...] = jnp.zeros_like(zero_ref)
        for e in range(pt_ref.shape[0]):
            @pl.when(pt_ref[e] >= 0)
            def _():
                zero_tile(pt_ref[e]).start()
        for e in range(pt_ref.shape[0]):
            @pl.when(pt_ref[e] >= 0)
            def _():
                zero_tile(pt_ref[e]).wait()

        def zero_unused(tile, carry):
            zero_tile(tile).start()
            zero_tile(tile).wait()
            return carry

        lax.fori_loop(nu_ref[0], xg_ref.shape[0] // tm, zero_unused, 0)

    def send(r, carry):
        _row_copies(x_ref, xg_ref, r, d1_ref[base + r], sem).start()
        _row_copies(x_ref, xg_ref, r, d2_ref[base + r], sem).start()
        return carry

    lax.fori_loop(0, tt, send, 0, unroll=8)
    _wait_rows(x_ref, xg_ref, tt, sem)
    _wait_rows(x_ref, xg_ref, tt, sem)


def dispatch(x, dest1, dest2, pad_tile, n_used, n_rows):
    m, d = x.shape
    tt = _tile(m, 256)
    return pl.pallas_call(
        _dispatch_kernel,
        grid_spec=pltpu.PrefetchScalarGridSpec(
            num_scalar_prefetch=4,
            grid=(m // tt,),
            in_specs=[pl.BlockSpec((tt, d), lambda i, d1, d2, pt, nu: (i, 0))],
            out_specs=pl.BlockSpec(memory_space=pl.ANY),
            scratch_shapes=[pltpu.VMEM((MOE_ROW_TILE, d), F32), pltpu.SemaphoreType.DMA(()),
                            pltpu.SemaphoreType.DMA(())]),
        out_shape=jax.ShapeDtypeStruct((n_rows, d), F32),
        compiler_params=_params("arbitrary"),
        name="moe_dispatch",
    )(dest1, dest2, pad_tile, n_used, x)


def _expert_ffn_kernel(te_ref, nu_ref, x_ref, wg_ref, wu_ref, wd_ref, o_ref):
    i, j = pl.program_id(0), pl.program_id(1)
    used = i < nu_ref[0]

    @pl.when(used)
    def _():
        a = x_ref[...].astype(BF16)
        gate = jnp.dot(a, wg_ref[0].astype(BF16), preferred_element_type=F32)
        up = jnp.dot(a, wu_ref[0].astype(BF16), preferred_element_type=F32)
        hidden = (gate * jax.nn.sigmoid(gate) * up).astype(BF16)
        part = jnp.dot(hidden, wd_ref[0].astype(BF16), preferred_element_type=F32)

        @pl.when(j == 0)
        def _():
            o_ref[...] = part

        @pl.when(j > 0)
        def _():
            o_ref[...] += part

    @pl.when(jnp.logical_not(used) & (j == 0))
    def _():
        o_ref[...] = jnp.zeros_like(o_ref)


def expert_ffn(xg, tile_expert, n_used, wg, wu, wd, layer):
    r, d = xg.shape
    f = wg.shape[-1]
    tm, tf = MOE_ROW_TILE, _tile(f, 1024)
    nf = f // tf

    def hidden_block(i, j, nu):
        return jnp.where(i < nu[0], j, nf - 1)

    def up_index(i, j, te, nu):
        return (layer, te[i], 0, hidden_block(i, j, nu))

    return pl.pallas_call(
        _expert_ffn_kernel,
        grid_spec=pltpu.PrefetchScalarGridSpec(
            num_scalar_prefetch=2,
            grid=(r // tm, nf),
            in_specs=[pl.BlockSpec((tm, d), lambda i, j, te, nu: (jnp.where(i < nu[0], i, 0), 0)),
                      pl.BlockSpec((None, 1, d, tf), up_index),
                      pl.BlockSpec((None, 1, d, tf), up_index),
                      pl.BlockSpec((None, 1, tf, d),
                                   lambda i, j, te, nu: (layer, te[i], hidden_block(i, j, nu), 0))],
            out_specs=pl.BlockSpec((tm, d), lambda i, j, te, nu: (i, 0))),
        out_shape=jax.ShapeDtypeStruct((r, d), F32),
        compiler_params=_params("parallel", "arbitrary"),
        name="expert_ffn",
    )(tile_expert, n_used, xg, wg, wu, wd)


def _combine_kernel(d1_ref, d2_ref, h_ref, y_ref, w1_ref, w2_ref, g_ref, mod_ref, *rest,
                    gate_idx, then_norm):
    if then_norm:
        g2_ref, mod2_ref, o_ref, a_ref, y1_ref, y2_ref, sem = rest
    else:
        o_ref, y1_ref, y2_ref, sem = rest
    tt = h_ref.shape[1]
    base = (pl.program_id(0) * pl.num_programs(1) + pl.program_id(1)) * tt

    def fetch(r, carry):
        _row_copies(y_ref, y1_ref, d1_ref[base + r], r, sem).start()
        _row_copies(y_ref, y2_ref, d2_ref[base + r], r, sem).start()
        return carry

    lax.fori_loop(0, tt, fetch, 0, unroll=8)
    _wait_rows(y_ref, y1_ref, tt, sem)
    _wait_rows(y_ref, y2_ref, tt, sem)
    y = w1_ref[...] * y1_ref[...] + w2_ref[...] * y2_ref[...]
    gate = mod_ref[0, gate_idx:gate_idx + 1, :]
    new = h_ref[0] + gate * (_rms(y) * g_ref[...])
    o_ref[0] = new
    if then_norm:
        a_ref[0] = _modulated_norm(new, g2_ref, mod2_ref, *then_norm).astype(a_ref.dtype)


def _stream_out(b, t, d, tt, n_ctx, latents_only):
    if not latents_only:
        return (lambda i, j, *_: (i, j, 0)), jax.ShapeDtypeStruct((b, t, d), F32), True
    skip = n_ctx // tt
    return ((lambda i, j, *_: (i, jnp.maximum(j - skip, 0), 0)),
            jax.ShapeDtypeStruct((b, t - n_ctx, d), F32), False)


def combine_residual(h, yg, dest1, dest2, w1, w2, g, mods, gate_idx, n_ctx, latents_only=False,
                     next_norm=None):
    b, t, d = h.shape
    tt = _tile(n_ctx, 256)
    nt = t // tt
    col = pl.BlockSpec((tt, 1), lambda i, j, d1, d2: (i * nt + j, 0))
    n_ctx_tiles = n_ctx // tt
    out_index, out_shape, in_place = _stream_out(b, t, d, tt, n_ctx, latents_only)
    row = pl.BlockSpec((1, tt, d), lambda i, j, d1, d2: (i, j, 0))
    vec = pl.BlockSpec((1, d), lambda i, j, d1, d2: (0, 0))
    mod = pl.BlockSpec((1, 6, d), lambda i, j, d1, d2: (jnp.where(j < n_ctx_tiles, b, i), 0, 0))
    in_specs = [row, pl.BlockSpec(memory_space=pl.ANY), col, col, vec, mod]
    args = [dest1, dest2, h, yg, w1.reshape(-1, 1), w2.reshape(-1, 1), g.reshape(1, d), mods]
    out_specs, out_shapes = pl.BlockSpec((1, tt, d), out_index), out_shape
    if next_norm is not None:
        g2, mods2 = next_norm[:2]
        in_specs, args = in_specs + [vec, mod], args + [g2.reshape(1, d), mods2]
        out_specs, out_shapes = [out_specs, row], [out_shape, jax.ShapeDtypeStruct((b, t, d), BF16)]
    return pl.pallas_call(
        functools.partial(_combine_kernel, gate_idx=gate_idx,
                          then_norm=None if next_norm is None else tuple(next_norm[2:])),
        grid_spec=pltpu.PrefetchScalarGridSpec(
            num_scalar_prefetch=2,
            grid=(b, nt),
            in_specs=in_specs,
            out_specs=out_specs,
            scratch_shapes=[pltpu.VMEM((tt, d), F32), pltpu.VMEM((tt, d), F32),
                            pltpu.SemaphoreType.DMA(())]),
        out_shape=out_shapes,
        input_output_aliases={2: 0} if in_place else {},
        compiler_params=_params("arbitrary", "arbitrary"),
        name="moe_combine",
    )(*args)


def moe_ffn_residual(h, a, w_router, b_router, wg, wu, wd, layer, g, mods, gate_idx, n_ctx,
                     latents_only=False, next_norm=None):
    m, d = a.shape
    n_exp = w_router.shape[1]
    tm = MOE_ROW_TILE
    info, counts = route(a, w_router, b_router)
    i1, i2 = info[0].astype(jnp.int32), info[1].astype(jnp.int32)
    w1, w2 = info[2], info[3]
    r1, r2 = info[4].astype(jnp.int32), info[5].astype(jnp.int32)

    counts = counts[:, 0].astype(jnp.int32)
    padded = (counts + tm - 1) // tm * tm
    ends = jnp.cumsum(padded)
    starts = ends - padded
    n_tiles = (TOP_K * m) // tm + n_exp
    tile_expert = jnp.sum(jnp.arange(n_tiles)[:, None] >= (ends // tm)[None, :], axis=1)
    tile_expert = jnp.minimum(tile_expert, n_exp - 1).astype(jnp.int32)
    n_used = (ends[-1:] // tm).astype(jnp.int32)
    dest1 = starts[i1] + r1
    dest2 = starts[i2] + r2
    pad_tile = jnp.where(counts > 0, ends // tm - 1, -1).astype(jnp.int32)

    xg = dispatch(a, dest1, dest2, pad_tile, n_used, n_tiles * tm)
    yg = expert_ffn(xg, tile_expert, n_used, wg, wu, wd, layer)
    return combine_residual(h, yg, dest1, dest2, w1, w2, g, mods, gate_idx, n_ctx, latents_only,
                            next_norm)


def kernel(x, c, ctx, c_ctx, w_mod, b_mod, g_pre_mix, g_post_mix, g_pre_ffn, g_post_ffn, w_in,
           hgrn_lb, hgrn_norm_g, diff_lambda, diff_norm_g, qk_norm_q, qk_norm_k, w_branch, w_out,
           w_ff_gate, w_ff_up, w_ff_down, w_router, b_router, w_moe_gate, w_moe_up, w_moe_down):
    n_batch, seq, d = x.shape
    n_ctx = ctx.shape[1]
    depth = w_in.shape[0]
    t = n_ctx + seq
    m = n_batch * t

    hw = HG_HEADS * HEAD
    kv_sizes = (hw, hw, hw, DF_HEADS * 2 * DF_DIM, DF_HEADS * 2 * DF_DIM,
                GQ_KV_HEADS * HEAD, GQ_KV_HEADS * HEAD)
    q_sizes = (hw, hw, DF_HEADS * 2 * DF_DIM, GQ_HEADS * HEAD, FN_GROUPS * HEAD, N_BRANCH * d)
    offs = np.concatenate([[0], np.cumsum(kv_sizes + q_sizes)]).astype(int)
    (c_ff, c_fb, c_hv, c_dfk, c_dfv, c_gqk, c_gqv,
     c_hq, c_hg, c_dfq, c_gqq, c_fu, c_gl) = [int(o) for o in offs[:-1]]

    rope_df = rope_tables(n_ctx, seq, DF_DIM)
    rope_gq = rope_tables(n_ctx, seq, HEAD)
    dft_chan = dft_matrices(HEAD)
    dft_ctx = dft_matrices(n_ctx)
    dft_lat = dft_matrices(seq)

    lb_all = jnp.cumsum(jax.nn.softmax(hgrn_lb.astype(F32), axis=0), axis=0)
    lb_all = lb_all - lb_all[:1]

    n_cond = n_batch + 1
    pad = (-n_cond) % 16
    cond = jnp.concatenate([c, c_ctx[None, :], jnp.zeros((pad, d), F32)], axis=0)
    mods_all = modulation(cond, w_mod, b_mod)[:, :n_cond].reshape(depth, n_cond, 6, d)

    h = jnp.concatenate([ctx, x], axis=1)

    (w_branch, w_out, w_ff_down, w_moe_gate, w_moe_up, w_moe_down) = [
        w.astype(BF16) for w in (w_branch, w_out, w_ff_down, w_moe_gate, w_moe_up, w_moe_down)]

    for l in range(depth):
        mods = mods_all[l]
        lam_init = 0.8 - 0.6 * math.exp(-0.3 * l)
        lv = diff_lambda[l].astype(F32)
        lam = jnp.exp(jnp.sum(lv[0] * lv[1])) - jnp.exp(jnp.sum(lv[2] * lv[3])) + lam_init

        if l == 0:
            a = norm_mod(h, g_pre_mix[l], mods, 0, 1, n_ctx)
        p2 = matmul(a.reshape(m, d), w_in, l, BF16, tm=1024, tn=1536)
        p = p2.reshape(n_batch, t, -1)

        y_hg = hgrn_scan(p, (c_hq, c_ff, c_fb, c_hv, c_hg), lb_all[l], hgrn_norm_g[l], n_ctx)

        df_q1, df_q2, df_k, gq_q, gq_k = qk_prep(p, (c_dfq, c_dfk, c_gqq, c_gqk), rope_df, rope_gq,
                                                 qk_norm_q[l], qk_norm_k[l])
        y_df = attention(df_q1, df_k, p, c_dfv, DF_HEADS, n_ctx, q2=df_q2, lam=lam,
                         g=diff_norm_g[l], post_scale=1.0 - lam_init)

        y_fn = fourier_mix(p, c_fu, n_ctx, dft_chan, dft_ctx, dft_lat)

        assert GQ_HEADS == 2 * GQ_KV_HEADS
        y_gq = attention(gq_q, gq_k, p, c_gqv, GQ_KV_HEADS, n_ctx)

        merged = merge((y_hg, y_df, y_fn, y_gq), p2, c_gl, w_branch, l)
        dense = l % 2 == 0
        h, f_in = matmul_residual(merged.reshape(n_batch, t, d), w_out, l, h, g_post_mix[l], mods, 2,
                                  n_ctx, g_pre_ffn[l], 3, 4, BF16 if dense else F32)
        f_in = f_in.reshape(m, d)

        j = l // 2
        last = l == depth - 1
        next_norm = None if last else (g_pre_mix[l + 1], mods_all[l + 1], 0, 1)
        if dense:
            hidden = swiglu_up(f_in, w_ff_gate, w_ff_up, j)
            y = matmul(hidden, w_ff_down, j, F32, tm=1024, tn=512)
            out = residual(h, y, g_post_ffn[l], mods, 5, n_ctx, last, next_norm)
        else:
            out = moe_ffn_residual(h, f_in, w_router[j], b_router[j], w_moe_gate, w_moe_up,
                                   w_moe_down, j, g_post_ffn[l], mods, 5, n_ctx, last, next_norm)
        h, a = (out, None) if last else out

    return h
```
